```python
import math
import jax
import jax.numpy as jnp
from jax import lax
import numpy as np

D_MODEL = 1024
BATCH = 2
SEQ = 8192
DEPTH = 2

GRID_W = 64
CTX_LEN = 256
HEAD_DIM = 64
ATTN_HEADS = 8
ATTN_KV_HEADS = 2
MLSTM_HEADS = 4
GDN_HEADS = 4
ATTN_W = ATTN_HEADS * HEAD_DIM
KV_W = ATTN_KV_HEADS * HEAD_DIM
MLSTM_W = MLSTM_HEADS * HEAD_DIM
GDN_W = GDN_HEADS * HEAD_DIM
MIX_W = ATTN_W + MLSTM_W + GDN_W
N_DIR = 2
CHUNK = 64
Q_BLOCK = 128
CONV_K = 5
ROPE_THETA = 10000.0
N_EXPERTS = 16
CAPACITY_FACTOR = 2
EXPERT_FF = 2048
N_MOD = 6
EPS = 1e-6
IN_SPLITS = (ATTN_W, KV_W, KV_W,
             MLSTM_W, MLSTM_W, MLSTM_W, MLSTM_W, N_DIR * MLSTM_HEADS, N_DIR * MLSTM_HEADS,
             GDN_W, GDN_W, GDN_W, GDN_W, N_DIR * GDN_HEADS, N_DIR * GDN_HEADS)
IN_W = ATTN_W + 2 * KV_W + 4 * MLSTM_W + 2 * N_DIR * MLSTM_HEADS + 4 * GDN_W + 2 * N_DIR * GDN_HEADS

kernel_name = 'hybrid_gqa_mlstm_gdn_ecmoe_dit'


def rms_norm(x, g):
    xf = x.astype(jnp.float32)
    y = xf * lax.rsqrt(jnp.mean(xf * xf, axis=-1, keepdims=True) + EPS)
    return (y * g.astype(jnp.float32)).astype(x.dtype)


def l2_norm(x):
    xf = x.astype(jnp.float32)
    return (xf * lax.rsqrt(jnp.sum(xf * xf, axis=-1, keepdims=True) + EPS)).astype(x.dtype)


def modulate(h, shift, scale):
    return h * (1 + scale) + shift


def split_cols(p):
    return jnp.split(p, np.cumsum(IN_SPLITS)[:-1].tolist(), axis=-1)


def flip_seq(t, on):
    return jnp.flip(t, axis=1) if on else t


def to_chunks(t):
    b, n, h = t.shape[:3]
    t = t.reshape(b, n // CHUNK, CHUNK, h, *t.shape[3:])
    return jnp.moveaxis(t, (1, 2), (0, 3))


def from_chunks(t):
    t = jnp.moveaxis(t, (0, 3), (1, 2))
    b, nc, l, h, d = t.shape
    return t.reshape(b, nc * l, h, d)


def axial_rope_tables(n):
    rows = n // GRID_W
    row = jnp.broadcast_to(jnp.arange(rows)[:, None], (rows, GRID_W)).reshape(-1)
    col = jnp.broadcast_to(jnp.arange(GRID_W)[None, :], (rows, GRID_W)).reshape(-1)
    n_freq = HEAD_DIM // 4
    inv = ROPE_THETA ** (-jnp.arange(n_freq, dtype=jnp.float32) / n_freq)
    ang = jnp.stack([row, col], axis=-1).astype(jnp.float32)[..., None] * inv
    return jnp.cos(ang), jnp.sin(ang)


def apply_axial_rope(x, cos, sin):
    b, n, h, d = x.shape
    xf = x.astype(jnp.float32).reshape(b, n, h, 2, 2, d // 4)
    x1, x2 = xf[..., 0, :], xf[..., 1, :]
    cs, sn = cos[:, None], sin[:, None]
    out = jnp.stack([x1 * cs - x2 * sn, x2 * cs + x1 * sn], axis=-2)
    return out.reshape(b, n, h, d).astype(x.dtype)


def attention_mixer(parts_x, parts_c, qn_g, kn_g, need_ctx):
    q_x, k_x, v_x = parts_x
    q_c, k_c, v_c = parts_c
    b, n, _ = q_x.shape
    m = q_c.shape[1]
    rep = ATTN_HEADS // ATTN_KV_HEADS
    scale = HEAD_DIM ** -0.5

    def heads(t, h):
        return t.reshape(t.shape[0], t.shape[1], h, HEAD_DIM)

    cos, sin = axial_rope_tables(n)
    qx = apply_axial_rope(rms_norm(heads(q_x, ATTN_HEADS), qn_g), cos, sin)
    kx = apply_axial_rope(rms_norm(heads(k_x, ATTN_KV_HEADS), kn_g), cos, sin)
    vx = heads(v_x, ATTN_KV_HEADS)
    kc = rms_norm(heads(k_c, ATTN_KV_HEADS), kn_g)
    vc = heads(v_c, ATTN_KV_HEADS)
    k_all = jnp.concatenate([kc, kx], axis=1)
    v_all = jnp.concatenate([vc, vx], axis=1)

    def attend(qb, kk, vv):
        s = jnp.einsum('bqgrd,bkgd->bgrqk', qb, kk).astype(jnp.float32) * scale
        p = jax.nn.softmax(s, axis=-1).astype(vv.dtype)
        return jnp.einsum('bgrqk,bkgd->bqgrd', p, vv)

    qxb = qx.reshape(b, n // Q_BLOCK, Q_BLOCK, ATTN_KV_HEADS, rep, HEAD_DIM).transpose(1, 0, 2, 3, 4, 5)
    out_x = lax.map(lambda qb: attend(qb, k_all, v_all), qxb)
    out_x = out_x.transpose(1, 0, 2, 3, 4, 5).reshape(b, n, ATTN_W)
    out_c = None
    if need_ctx:
        qc = rms_norm(heads(q_c, ATTN_HEADS), qn_g).reshape(b, m, ATTN_KV_HEADS, rep, HEAD_DIM)
        out_c = attend(qc, kc, vc).reshape(b, m, ATTN_W)
    return out_x, out_c


def mlstm_scan(q, k, v, i_pre, f_pre, state, with_out):
    causal = jnp.tril(jnp.ones((CHUNK, CHUNK), dtype=bool))

    def step(carry, inp):
        C, nv, m = carry
        qc, kc, vc, ic, fc = inp
        bcum = jnp.cumsum(jax.nn.log_sigmoid(fc), axis=-1)
        b_last = bcum[..., -1]
        w_end = b_last[..., None] - bcum + ic
        m_new = jnp.maximum(b_last + m, jnp.max(w_end, axis=-1))
        a_end = jnp.exp(w_end - m_new[..., None])
        dec = jnp.exp(b_last + m - m_new)
        C_new = dec[..., None, None] * C + jnp.einsum('bhl,bhld,bhle->bhde', a_end, kc, vc)
        n_new = dec[..., None] * nv + jnp.einsum('bhl,bhld->bhd', a_end, kc)
        if not with_out:
            return (C_new, n_new, m_new), None
        dmat = jnp.where(causal, bcum[..., :, None] - bcum[..., None, :] + ic[..., None, :], -jnp.inf)
        inter = bcum + m[..., None]
        m_t = jnp.maximum(inter, jnp.max(dmat, axis=-1))
        w_in = jnp.exp(inter - m_t)
        s = jnp.einsum('bhtd,bhsd->bhts', qc, kc) * jnp.exp(dmat - m_t[..., None])
        num = w_in[..., None] * jnp.einsum('bhtd,bhde->bhte', qc, C) + jnp.einsum('bhts,bhse->bhte', s, vc)
        den = w_in * jnp.einsum('bhtd,bhd->bht', qc, nv) + jnp.sum(s, axis=-1)
        h = num / jnp.maximum(jnp.abs(den), jnp.exp(-m_t))[..., None]
        return (C_new, n_new, m_new), h

    xs = (to_chunks(q), to_chunks(k), to_chunks(v), to_chunks(i_pre), to_chunks(f_pre))
    final, hs = lax.scan(step, state, xs)
    return final, (from_chunks(hs) if with_out else None)


def mlstm_mixer(parts_x, parts_c, i_bias, f_bias, out_g, need_ctx):
    dtype = parts_x[0].dtype

    def prep(parts):
        q, k, v, o, ig, fg = [p.astype(jnp.float32) for p in parts]
        b, n, _ = q.shape
        hd = lambda t: t.reshape(b, n, MLSTM_HEADS, HEAD_DIM)
        ig = ig.reshape(b, n, N_DIR, MLSTM_HEADS) + i_bias.astype(jnp.float32)
        fg = fg.reshape(b, n, N_DIR, MLSTM_HEADS) + f_bias.astype(jnp.float32)
        return hd(q), hd(k) * HEAD_DIM ** -0.5, hd(v), o, ig, fg

    qx, kx, vx, ox, ix, fx = prep(parts_x)
    qc, kc, vc, oc, ic, fc = prep(parts_c)
    b = qx.shape[0]
    zero = (jnp.zeros((b, MLSTM_HEADS, HEAD_DIM, HEAD_DIM), jnp.float32),
            jnp.zeros((b, MLSTM_HEADS, HEAD_DIM), jnp.float32),
            jnp.zeros((b, MLSTM_HEADS), jnp.float32))
    hx_sum, hc_sum = 0.0, 0.0
    for d in range(N_DIR):
        rv = d == 1
        st, hc = mlstm_scan(flip_seq(qc, rv), flip_seq(kc, rv), flip_seq(vc, rv),
                            flip_seq(ic[:, :, d], rv), flip_seq(fc[:, :, d], rv), zero, need_ctx)
        _, hx = mlstm_scan(flip_seq(qx, rv), flip_seq(kx, rv), flip_seq(vx, rv),
                           flip_seq(ix[:, :, d], rv), flip_seq(fx[:, :, d], rv), st, True)
        hx_sum = hx_sum + flip_seq(hx, rv)
        if need_ctx:
            hc_sum = hc_sum + flip_seq(hc, rv)

    def finish(h, o):
        bb, nn = h.shape[:2]
        return (jax.nn.sigmoid(o) * rms_norm(h, out_g).reshape(bb, nn, MLSTM_W)).astype(dtype)

    return finish(hx_sum, ox), (finish(hc_sum, oc) if need_ctx else None)


def short_conv(t, w):
    k, ch = w.shape
    return lax.conv_general_dilated(t, w.reshape(k, 1, ch).astype(t.dtype), window_strides=(1,),
                                    padding=[(k // 2, k // 2)], dimension_numbers=('NWC', 'WIO', 'NWC'),
                                    feature_group_count=ch)


def gdn_scan(q, k, v, g, beta, state, with_out):
    incl = jnp.tril(jnp.ones((CHUNK, CHUNK), dtype=bool))
    strict = jnp.tril(jnp.ones((CHUNK, CHUNK), dtype=bool), -1)
    eye = jnp.eye(CHUNK, dtype=jnp.float32)

    def step(S, inp):
        qc, kc, vc, gc, bc = inp
        G = jnp.cumsum(gc, axis=-1)
        decay = jnp.where(incl, jnp.exp(jnp.where(incl, G[..., :, None] - G[..., None, :], 0.0)), 0.0)
        kb = kc * bc[..., None]
        A = eye + jnp.where(strict, jnp.einsum('bhid,bhjd->bhij', kb, kc) * decay, 0.0)
        rhs = jnp.concatenate([vc * bc[..., None], kb * jnp.exp(G)[..., None]], axis=-1)
        sol = lax.linalg.triangular_solve(A, rhs, left_side=True, lower=True, unit_diagonal=True)
        u, w = sol[..., :HEAD_DIM], sol[..., HEAD_DIM:]
        v_new = u - jnp.einsum('bhld,bhde->bhle', w, S)
        G_last = G[..., -1]
        S_new = jnp.exp(G_last)[..., None, None] * S + jnp.einsum(
            'bhld,bhle->bhde', kc * jnp.exp(G_last[..., None] - G)[..., None], v_new)
        if not with_out:
            return S_new, None
        o = jnp.einsum('bhld,bhde->bhle', qc * jnp.exp(G)[..., None], S) + jnp.einsum(
            'bhts,bhse->bhte', jnp.einsum('bhtd,bhsd->bhts', qc, kc) * decay, v_new)
        return S_new, o

    xs = (to_chunks(q), to_chunks(k), to_chunks(v), to_chunks(g), to_chunks(beta))
    final, os_ = lax.scan(step, state, xs)
    return final, (from_chunks(os_) if with_out else None)


def gdn_mixer(parts_x, parts_c, conv_w, a_log, dt_bias, out_g, need_ctx):
    dtype = parts_x[0].dtype

    def prep(parts):
        q, k, v, z, a, bt = parts
        b, n, _ = q.shape
        qkv = jax.nn.silu(short_conv(jnp.concatenate([q, k, v], axis=-1), conv_w)).astype(jnp.float32)
        q, k, v = jnp.split(qkv, 3, axis=-1)
        hd = lambda t: t.reshape(b, n, GDN_HEADS, HEAD_DIM)
        q = l2_norm(hd(q)) * HEAD_DIM ** -0.5
        k = l2_norm(hd(k))
        g = -jnp.exp(a_log.astype(jnp.float32)) * jax.nn.softplus(
            a.astype(jnp.float32).reshape(b, n, N_DIR, GDN_HEADS) + dt_bias.astype(jnp.float32))
        beta = jax.nn.sigmoid(bt.astype(jnp.float32).reshape(b, n, N_DIR, GDN_HEADS))
        return q, k, hd(v), z, g, beta

    qx, kx, vx, zx, gx, bx = prep(parts_x)
    qc, kc, vc, zc, gc, bc = prep(parts_c)
    b = qx.shape[0]
    zero = jnp.zeros((b, GDN_HEADS, HEAD_DIM, HEAD_DIM), jnp.float32)
    ox_sum, oc_sum = 0.0, 0.0
    for d in range(N_DIR):
        rv = d == 1
        st, oc = gdn_scan(flip_seq(qc, rv), flip_seq(kc, rv), flip_seq(vc, rv),
                          flip_seq(gc[:, :, d], rv), flip_seq(bc[:, :, d], rv), zero, need_ctx)
        _, ox = gdn_scan(flip_seq(qx, rv), flip_seq(kx, rv), flip_seq(vx, rv),
                         flip_seq(gx[:, :, d], rv), flip_seq(bx[:, :, d], rv), st, True)
        ox_sum = ox_sum + flip_seq(ox, rv)
        if need_ctx:
            oc_sum = oc_sum + flip_seq(oc, rv)

    def finish(o, z):
        bb, nn = o.shape[:2]
        return (rms_norm(o, out_g).reshape(bb, nn, GDN_W) * jax.nn.silu(z.astype(jnp.float32))).astype(dtype)

    return finish(ox_sum, zx), (finish(oc_sum, zc) if need_ctx else None)


def expert_choice_ffn(h, router_w, w1, w3, w2):
    b, n, _ = h.shape
    cap = CAPACITY_FACTOR * n // N_EXPERTS
    aff = jax.nn.softmax((h @ router_w).astype(jnp.float32), axis=-1)
    vals, idx = lax.top_k(jnp.swapaxes(aff, 1, 2), cap)
    bidx = jnp.arange(b)[:, None, None]
    xg = h[bidx, idx]
    hid = jax.nn.silu(jnp.einsum('becd,edf->becf', xg, w1)) * jnp.einsum('becd,edf->becf', xg, w3)
    y = jnp.einsum('becf,efd->becd', hid, w2) * vals[..., None].astype(h.dtype)
    return jnp.zeros_like(h).at[bidx, idx].add(y)


def setup_inputs(seed: int = 0) -> dict:
    key = jax.random.key(seed)
    ks = jax.random.split(key, 26)
    f32 = jnp.float32
    L, D, E, F = DEPTH, D_MODEL, N_EXPERTS, EXPERT_FF
    nrm = lambda k, shape, s: jax.random.normal(k, shape, f32) * s
    gain = lambda k, shape: 1.0 + 0.02 * jax.random.normal(k, shape, f32)
    dt = jnp.exp(jax.random.uniform(ks[14], (L, N_DIR, GDN_HEADS), f32, math.log(1e-3), math.log(1e-1)))
    return {
        'x': nrm(ks[0], (BATCH, SEQ, D), 1.0),
        'c': nrm(ks[1], (BATCH, D), 1.0),
        'ctx': nrm(ks[2], (BATCH, CTX_LEN, D), 1.0),
        'c_ctx': nrm(ks[3], (D,), 1.0),
        'mod_w': nrm(ks[4], (L, D, N_MOD * D), 0.5 * D ** -0.5),
        'mod_b': nrm(ks[5], (L, N_MOD * D), 0.02),
        'norm1_g': gain(ks[6], (L, D)),
        'w_in': nrm(ks[7], (L, D, IN_W), D ** -0.5),
        'q_norm_g': gain(ks[8], (L, HEAD_DIM)),
        'k_norm_g': gain(ks[9], (L, HEAD_DIM)),
        'mlstm_i_bias': nrm(ks[10], (L, N_DIR, MLSTM_HEADS), 0.1),
        'mlstm_f_bias': jnp.linspace(3.0, 6.0, MLSTM_HEADS, dtype=f32) + nrm(ks[11], (L, N_DIR, MLSTM_HEADS), 0.1),
        'mlstm_out_g': gain(ks[12], (L, MLSTM_HEADS, HEAD_DIM)),
        'gdn_conv_w': nrm(ks[13], (L, CONV_K, 3 * GDN_W), CONV_K ** -0.5),
        'gdn_a_log': jnp.log(jax.random.uniform(ks[15], (L, N_DIR, GDN_HEADS), f32, 1.0, 16.0)),
        'gdn_dt_bias': dt + jnp.log(-jnp.expm1(-dt)),
        'gdn_out_g': gain(ks[16], (L, HEAD_DIM)),
        'w_out': nrm(ks[17], (L, MIX_W, D), MIX_W ** -0.5),
        'norm2_g': gain(ks[18], (L, D)),
        'router_w': nrm(ks[19], (L, D, E), D ** -0.5),
        'w1': nrm(ks[20], (L, E, D, F), D ** -0.5),
        'w3': nrm(ks[21], (L, E, D, F), D ** -0.5),
        'w2': nrm(ks[22], (L, E, F, D), F ** -0.5),
    }


def reference(x, c, ctx, c_ctx, mod_w, mod_b, norm1_g, w_in, q_norm_g, k_norm_g,
              mlstm_i_bias, mlstm_f_bias, mlstm_out_g, gdn_conv_w, gdn_a_log, gdn_dt_bias,
              gdn_out_g, w_out, norm2_g, router_w, w1, w3, w2):
    xc = ctx
    for l in range(DEPTH):
        need_ctx = l < DEPTH - 1
        mod = (jax.nn.silu(c) @ mod_w[l] + mod_b[l])[:, None, :]
        modc = jax.nn.silu(c_ctx) @ mod_w[l] + mod_b[l]
        sh1, sc1, g1, sh2, sc2, g2 = jnp.split(mod, N_MOD, axis=-1)
        sh1c, sc1c, g1c, sh2c, sc2c, g2c = jnp.split(modc, N_MOD, axis=-1)

        px = split_cols(modulate(rms_norm(x, norm1_g[l]), sh1, sc1) @ w_in[l])
        pc = split_cols(modulate(rms_norm(xc, norm1_g[l]), sh1c, sc1c) @ w_in[l])
        ax, ac = attention_mixer(px[0:3], pc[0:3], q_norm_g[l], k_norm_g[l], need_ctx)
        mx, mc = mlstm_mixer(px[3:9], pc[3:9], mlstm_i_bias[l], mlstm_f_bias[l], mlstm_out_g[l], need_ctx)
        gx, gc = gdn_mixer(px[9:15], pc[9:15], gdn_conv_w[l], gdn_a_log[l], gdn_dt_bias[l], gdn_out_g[l], need_ctx)

        x = x + g1 * (jnp.concatenate([ax, mx, gx], axis=-1) @ w_out[l])
        x = x + g2 * expert_choice_ffn(modulate(rms_norm(x, norm2_g[l]), sh2, sc2),
                                       router_w[l], w1[l], w3[l], w2[l])
        if need_ctx:
            xc = xc + g1c * (jnp.concatenate([ac, mc, gc], axis=-1) @ w_out[l])
            xc = xc + g2c * expert_choice_ffn(modulate(rms_norm(xc, norm2_g[l]), sh2c, sc2c),
                                              router_w[l], w1[l], w3[l], w2[l])
    return x
```

```python
import functools
import math

import jax
import jax.numpy as jnp
from jax import lax
from jax.experimental import pallas as pl
from jax.experimental.pallas import tpu as pltpu

F32 = jnp.float32
BF16 = jnp.bfloat16
I32 = jnp.int32
U32 = jnp.uint32
HIGHEST = lax.Precision.HIGHEST

HEAD_DIM = 64
ATTN_HEADS = 8
ATTN_KV_HEADS = 2
ATTN_REP = ATTN_HEADS // ATTN_KV_HEADS
MLSTM_HEADS = 4
GDN_HEADS = 4
N_DIR = 2
CHUNK = 64
CONV_K = 5
GRID_W = 64
ROPE_THETA = 10000.0
N_EXPERTS = 16
CAPACITY_FACTOR = 2
N_MOD = 6
EPS = 1e-6
ATTN_W = ATTN_HEADS * HEAD_DIM
KV_W = ATTN_KV_HEADS * HEAD_DIM
MLSTM_W = MLSTM_HEADS * HEAD_DIM
GDN_W = GDN_HEADS * HEAD_DIM
LANES = 128
SUBLANES = 8
GATE_W = LANES
GC_MI, GC_MF, GC_GA, GC_GB = 0, 8, 16, 24
VMEM_LIMIT = 56 * 1024 * 1024
ATTN_TK = 512
FFN_TF = 512
CONV_ROWS = 256
PAD_ROWS = 8


def _cparams(sem):
    return pltpu.CompilerParams(dimension_semantics=sem, vmem_limit_bytes=VMEM_LIMIT)


def _sigmoid(x):
    return 1.0 / (1.0 + jnp.exp(-x))


def _silu(x):
    return x * _sigmoid(x)


def _log_sigmoid(x):
    return jnp.minimum(x, 0.0) - jnp.log1p(jnp.exp(-jnp.abs(x)))


def _softplus(x):
    return jnp.maximum(x, 0.0) + jnp.log1p(jnp.exp(-jnp.abs(x)))


def _dot(a, b, precision=None):
    return jnp.dot(a, b, preferred_element_type=F32, precision=precision)


def _dot_nt(a, b, precision=None):
    return lax.dot_general(a, b, (((1,), (1,)), ((), ())), preferred_element_type=F32, precision=precision)


def _split_bf16(a):
    hi = a.astype(BF16)
    lo = (a - hi.astype(F32)).astype(BF16)
    return hi, lo


def _dot3(a, b):
    ah, al = _split_bf16(a)
    bh, bl = _split_bf16(b)
    return _dot(ah, bh) + (_dot(ah, bl) + _dot(al, bh))


def _iota(shape, dim):
    return lax.broadcasted_iota(I32, shape, dim)


def _eye_rows(rows, cols, first):
    return (_iota((rows, cols), 0) + first == _iota((rows, cols), 1)).astype(F32)


def _mod_kernel(c_ref, w_ref, b_ref, o_ref):
    s = _silu(c_ref[...])
    o_ref[0] = _dot(s, w_ref[0], precision=HIGHEST) + b_ref[0]


def _modulation(cvec, mod_w, mod_b):
    depth, d, n = mod_w.shape
    rows = cvec.shape[0]
    tn = d
    return pl.pallas_call(
        _mod_kernel,
        grid=(depth, n // tn),
        in_specs=[
            pl.BlockSpec((rows, d), lambda l, j: (0, 0)),
            pl.BlockSpec((1, d, tn), lambda l, j: (l, 0, j)),
            pl.BlockSpec((1, 1, tn), lambda l, j: (l, 0, j)),
        ],
        out_specs=pl.BlockSpec((1, rows, tn), lambda l, j: (l, 0, j)),
        out_shape=jax.ShapeDtypeStruct((depth, rows, n), F32),
        compiler_params=_cparams(("parallel", "parallel")),
        name="modulation",
    )(cvec, mod_w, mod_b.reshape(depth, 1, n))


def _inproj_kernel(x_ref, mod_ref, g_ref, w_ref, cos_ref, sin_ref, qg_ref, kg_ref, bd_ref,
                   q_ref, k_ref, v_ref, ml_ref, mo_ref, gd_ref, gz_ref, gate_ref, *, d_model):
    x = x_ref[0]
    sh = mod_ref[:, 0:d_model]
    sc = mod_ref[:, d_model:2 * d_model]
    xn = x * lax.rsqrt(jnp.mean(x * x, axis=-1, keepdims=True) + EPS) * g_ref[...]
    h = (xn * (1.0 + sc) + sh).astype(BF16)
    p = _dot(h, w_ref[...])

    cos = cos_ref[...]
    sin = sin_ref[...]
    first_half = (_iota(cos.shape, 1) % 32) < 16

    def norm_rope(xs, g, scale):
        ms = _dot(xs * xs, bd_ref[...], precision=HIGHEST)
        xn_ = xs * lax.rsqrt(ms + EPS) * g
        sw = jnp.where(first_half, pltpu.roll(xn_, LANES - 16, 1), pltpu.roll(xn_, 16, 1))
        return (xn_ * cos + sw * sin) * scale

    for j in range(ATTN_W // LANES):
        qs = norm_rope(p[:, j * LANES:(j + 1) * LANES], qg_ref[...], HEAD_DIM ** -0.5).astype(BF16)
        q_ref[0, 2 * j] = qs[:, 0:HEAD_DIM]
        q_ref[0, 2 * j + 1] = qs[:, HEAD_DIM:LANES]
    ks = norm_rope(p[:, ATTN_W:ATTN_W + KV_W], kg_ref[...], 1.0).astype(BF16)
    k_ref[0, 0] = ks[:, 0:HEAD_DIM]
    k_ref[0, 1] = ks[:, HEAD_DIM:LANES]
    vs = p[:, ATTN_W + KV_W:ATTN_W + 2 * KV_W].astype(BF16)
    v_ref[0, 0] = vs[:, 0:HEAD_DIM]
    v_ref[0, 1] = vs[:, HEAD_DIM:LANES]

    off = ATTN_W + 2 * KV_W

    def head(j):
        return p[:, off + j * HEAD_DIM: off + (j + 1) * HEAD_DIM]

    for j in range(12):
        ml_ref[0, j] = head(j) * HEAD_DIM ** -0.5 if 4 <= j < 8 else head(j)
    for j in range(4):
        mo_ref[0, j] = head(12 + j)
    off += 4 * MLSTM_W
    for j in range(12):
        gd_ref[0, j] = head(j)
    for j in range(4):
        gz_ref[0, j] = head(12 + j)
    off += 4 * GDN_W
    gate_ref[0] = p[:, off:off + GATE_W]


def _inproj(xa, modsel, g1, w_r, cos_t, sin_t, qg, kg, bd, tm):
    b, t, d = xa.shape
    nb = t // tm
    n = w_r.shape[1]
    kern = functools.partial(_inproj_kernel, d_model=d)
    hm_shape = lambda nh, dt: jax.ShapeDtypeStruct((b, nh, t, HEAD_DIM), dt)
    out_shapes = (hm_shape(ATTN_HEADS, BF16), hm_shape(ATTN_KV_HEADS, BF16), hm_shape(ATTN_KV_HEADS, BF16),
                  hm_shape(12, F32), hm_shape(4, F32), hm_shape(12, F32), hm_shape(4, F32),
                  jax.ShapeDtypeStruct((b, t, GATE_W), F32))
    hm = lambda nh: pl.BlockSpec((1, nh, tm, HEAD_DIM), lambda bi, i: (bi, 0, i, 0))
    return pl.pallas_call(
        kern,
        grid=(b, nb),
        in_specs=[
            pl.BlockSpec((1, tm, d), lambda bi, i: (bi, i, 0)),
            pl.BlockSpec((None, None, 1, N_MOD * d), lambda bi, i: (bi, jnp.minimum(i, 1), 0, 0)),
            pl.BlockSpec((1, d), lambda bi, i: (0, 0)),
            pl.BlockSpec((d, n), lambda bi, i: (0, 0)),
            pl.BlockSpec((tm, LANES), lambda bi, i: (i, 0)),
            pl.BlockSpec((tm, LANES), lambda bi, i: (i, 0)),
            pl.BlockSpec((1, LANES), lambda bi, i: (0, 0)),
            pl.BlockSpec((1, LANES), lambda bi, i: (0, 0)),
            pl.BlockSpec((LANES, LANES), lambda bi, i: (0, 0)),
        ],
        out_specs=(hm(ATTN_HEADS), hm(ATTN_KV_HEADS), hm(ATTN_KV_HEADS), hm(12), hm(4), hm(12), hm(4),
                   pl.BlockSpec((1, tm, GATE_W), lambda bi, i: (bi, i, 0))),
        out_shape=out_shapes,
        compiler_params=_cparams(("parallel", "parallel")),
        name="inproj",
    )(xa, modsel, g1, w_r, cos_t, sin_t, qg, kg, bd)


def _attn_kernel(q_ref, k_ref, v_ref, o_ref, m_sc, l_sc, acc_sc, *, tq, ctx_len, tk, n_chunks, blk0):
    i = pl.program_id(2) + blk0
    q = q_ref[0].reshape(ATTN_REP * tq, HEAD_DIM)

    m_sc[...] = jnp.full(m_sc.shape, -jnp.inf, F32)
    l_sc[...] = jnp.zeros(l_sc.shape, F32)
    acc_sc[...] = jnp.zeros(acc_sc.shape, F32)

    def chunk(start, size):
        k = k_ref[0, 0, pl.ds(start, size), :]
        v = v_ref[0, 0, pl.ds(start, size), :]
        s = _dot_nt(q, k)
        m_old = m_sc[...]
        m_new = jnp.maximum(m_old, jnp.max(s, axis=-1, keepdims=True))
        alpha = jnp.exp(m_old - m_new)
        p = jnp.exp(s - m_new)
        l_sc[...] = alpha * l_sc[...] + jnp.sum(p, axis=-1, keepdims=True)
        acc_sc[...] = alpha * acc_sc[...] + _dot(p.astype(BF16), v)
        m_sc[...] = m_new

    chunk(0, ctx_len)

    def body(j, carry):
        chunk(pl.multiple_of(ctx_len + j * tk, math.gcd(ctx_len, tk)), tk)
        return carry

    lax.fori_loop(0, jnp.where(i == 0, 0, n_chunks), body, 0)
    o = acc_sc[...] / l_sc[...]
    o_ref[0] = o.reshape(ATTN_REP, tq, HEAD_DIM).astype(o_ref.dtype)


def _attention(q, k, v, ctx_len, tq, tk, blk0):
    b, _, t, _ = q.shape
    nq = t // tq - blk0
    n_chunks = (t - ctx_len) // tk
    kern = functools.partial(_attn_kernel, tq=tq, ctx_len=ctx_len, tk=tk, n_chunks=n_chunks, blk0=blk0)
    rows = ATTN_REP * tq
    return pl.pallas_call(
        kern,
        grid=(b, ATTN_KV_HEADS, nq),
        in_specs=[
            pl.BlockSpec((1, ATTN_REP, tq, HEAD_DIM), lambda bi, g, i: (bi, g, i + blk0, 0)),
            pl.BlockSpec((1, 1, t, HEAD_DIM), lambda bi, g, i: (bi, g, 0, 0)),
            pl.BlockSpec((1, 1, t, HEAD_DIM), lambda bi, g, i: (bi, g, 0, 0)),
        ],
        out_specs=pl.BlockSpec((1, ATTN_REP, tq, HEAD_DIM), lambda bi, g, i: (bi, g, i + blk0, 0)),
        out_shape=jax.ShapeDtypeStruct((b, ATTN_HEADS, t, HEAD_DIM), BF16),
        scratch_shapes=[pltpu.VMEM((rows, 1), F32), pltpu.VMEM((rows, 1), F32), pltpu.VMEM((rows, HEAD_DIM), F32)],
        compiler_params=_cparams(("parallel", "parallel", "arbitrary")),
        name="attention",
    )(q, k, v)


def _chunk_maps(nc_c, nc_x):
    fwd = lambda j: j
    rev = lambda j: jnp.where(j < nc_c, nc_c - 1 - j, 2 * nc_c + nc_x - 1 - j)
    return fwd, rev


def _dir_masks(d):
    r = _iota((CHUNK, CHUNK), 0)
    c = _iota((CHUNK, CHUNK), 1)
    incl = (r >= c) if d == 0 else (r <= c)
    strict = (r > c) if d == 0 else (r < c)
    incl_t = (r <= c) if d == 0 else (r >= c)
    return incl, strict, incl.astype(F32), incl_t.astype(F32)


def _mlstm_kernel(xf_ref, xr_ref, gf_ref, gr_ref, brow_ref, bcol_ref, hf_ref, hr_ref, c_sc, n_sc, m_sc, *, nb):
    @pl.when(pl.program_id(0) == 0)
    def _():
        c_sc[...] = jnp.zeros(c_sc.shape, F32)
        n_sc[...] = jnp.zeros(n_sc.shape, F32)
        m_sc[...] = jnp.zeros(m_sc.shape, F32)

    eye_g = _eye_rows(16, GATE_W, GC_MI)
    eye_h = _eye_rows(HEAD_DIM, HEAD_DIM, 0)
    for d in range(N_DIR):
        x_ref, g_ref, h_ref = ((xf_ref, gf_ref, hf_ref), (xr_ref, gr_ref, hr_ref))[d]
        incl, _, tri, tri_t = _dir_masks(d)
        for b in range(nb):
            g = g_ref[b]
            cum = _dot(tri, _log_sigmoid(g + brow_ref[...]), precision=HIGHEST)
            g_t = _dot_nt(eye_g, g, precision=HIGHEST) + bcol_ref[...]
            lf_t = _log_sigmoid(g_t)
            cum_t = _dot(lf_t, tri_t, precision=HIGHEST)
            for h in range(MLSTM_HEADS):
                p = (d * nb + b) * MLSTM_HEADS + h
                ri = d * MLSTM_HEADS + h
                rf = GC_MF + ri
                bcum_col = cum[:, rf:rf + 1]
                bcum_row = cum_t[rf:rf + 1, :]
                i_row = g_t[ri:ri + 1, :]
                b_last = jnp.sum(lf_t[rf:rf + 1, :], axis=-1, keepdims=True)
                m_old = m_sc[p, 0:1, 0:1]
                w_end = b_last - bcum_row + i_row
                m_new = jnp.maximum(b_last + m_old, jnp.max(w_end, axis=-1, keepdims=True))
                a_row = jnp.exp(w_end - m_new)
                dec = jnp.exp(b_last + m_old - m_new)
                q = x_ref[b, h]
                k = x_ref[b, MLSTM_HEADS + h]
                v = x_ref[b, 2 * MLSTM_HEADS + h]
                k_t = _dot_nt(eye_h, k)
                c_old = c_sc[p]
                n_old = n_sc[p]
                dmat = jnp.where(incl, bcum_col - bcum_row + i_row, -jnp.inf)
                inter = bcum_col + m_old
                m_t = jnp.maximum(inter, jnp.max(dmat, axis=-1, keepdims=True))
                w_in = jnp.exp(inter - m_t)
                s = _dot(q, k_t) * jnp.exp(dmat - m_t)
                num = w_in * _dot(q, c_old) + _dot(s, v)
                den = (w_in * jnp.sum(q * n_old[0:1, :], axis=-1, keepdims=True)
                       + jnp.sum(s, axis=-1, keepdims=True))
                h_ref[b, h] = num / jnp.maximum(jnp.abs(den), jnp.exp(-m_t))
                c_sc[p] = dec * c_old + _dot(k_t * a_row, v)
                n_sc[p] = dec * n_old + _dot(jnp.broadcast_to(a_row, (SUBLANES, CHUNK)), k)
                m_sc[p] = jnp.broadcast_to(m_new, (SUBLANES, LANES))


def _mlstm_scan(mqkv, gates, brow, bcol, ctx_len):
    b, _, t, _ = mqkv.shape
    nc_c, nc_x = ctx_len // CHUNK, (t - ctx_len) // CHUNK
    fwd, rev = _chunk_maps(nc_c, nc_x)
    nprob = N_DIR * b * MLSTM_HEADS
    xs = lambda f: pl.BlockSpec((b, 12, CHUNK, HEAD_DIM), lambda j: (0, 0, f(j), 0))
    gs = lambda f: pl.BlockSpec((b, CHUNK, GATE_W), lambda j: (0, f(j), 0))
    hs = lambda f: pl.BlockSpec((b, MLSTM_HEADS, CHUNK, HEAD_DIM), lambda j: (0, 0, f(j), 0))
    out = jax.ShapeDtypeStruct((b, MLSTM_HEADS, t, HEAD_DIM), F32)
    return pl.pallas_call(
        functools.partial(_mlstm_kernel, nb=b),
        grid=(nc_c + nc_x,),
        in_specs=[xs(fwd), xs(rev), gs(fwd), gs(rev),
                  pl.BlockSpec((1, GATE_W), lambda j: (0, 0)),
                  pl.BlockSpec((16, CHUNK), lambda j: (0, 0))],
        out_specs=(hs(fwd), hs(rev)),
        out_shape=(out, out),
        scratch_shapes=[pltpu.VMEM((nprob, HEAD_DIM, HEAD_DIM), F32),
                        pltpu.VMEM((nprob, SUBLANES, HEAD_DIM), F32),
                        pltpu.VMEM((nprob, SUBLANES, LANES), F32)],
        compiler_params=_cparams(("arbitrary",)),
        name="mlstm_scan",
    )(mqkv, mqkv, gates, gates, brow, bcol)


def _gdn_prep_kernel(x_ref, w_ref, o_ref, pad_sc, *, ctx_len, t):
    part = pl.program_id(1) // GDN_HEADS
    zeros = jnp.zeros((PAD_ROWS, HEAD_DIM), F32)
    pad_sc[0:PAD_ROWS] = zeros
    pad_sc[PAD_ROWS:PAD_ROWS + ctx_len] = x_ref[0, 0, 0:ctx_len]
    pad_sc[PAD_ROWS + ctx_len:2 * PAD_ROWS + ctx_len] = zeros
    pad_sc[2 * PAD_ROWS + ctx_len:2 * PAD_ROWS + t] = x_ref[0, 0, ctx_len:t]
    pad_sc[2 * PAD_ROWS + t:3 * PAD_ROWS + t] = zeros
    w = w_ref[0]
    is_qk = part < 2
    scale = jnp.where(part == 0, HEAD_DIM ** -0.5, 1.0)
    for c in range(t // CONV_ROWS):
        r0 = c * CONV_ROWS
        base = r0 + (PAD_ROWS if r0 < ctx_len else 2 * PAD_ROWS) - CONV_K // 2
        y = w[0:1, :] * pad_sc[base:base + CONV_ROWS]
        for j in range(1, CONV_K):
            y = y + w[j:j + 1, :] * pad_sc[base + j:base + j + CONV_ROWS]
        y = _silu(y)
        yn = y * lax.rsqrt(jnp.sum(y * y, axis=-1, keepdims=True) + EPS) * scale
        o_ref[0, 0, r0:r0 + CONV_ROWS] = jnp.where(is_qk, yn, y)


def _gdn_prep(gqkv, conv_w12, ctx_len):
    b, np_, t, _ = gqkv.shape
    return pl.pallas_call(
        functools.partial(_gdn_prep_kernel, ctx_len=ctx_len, t=t),
        grid=(b, np_),
        in_specs=[pl.BlockSpec((1, 1, t, HEAD_DIM), lambda bi, p: (bi, p, 0, 0)),
                  pl.BlockSpec((1, CONV_K, HEAD_DIM), lambda bi, p: (p, 0, 0))],
        out_specs=pl.BlockSpec((1, 1, t, HEAD_DIM), lambda bi, p: (bi, p, 0, 0)),
        out_shape=jax.ShapeDtypeStruct(gqkv.shape, F32),
        scratch_shapes=[pltpu.VMEM((t + 3 * PAD_ROWS, HEAD_DIM), F32)],
        compiler_params=_cparams(("parallel", "parallel")),
        name="gdn_prep",
    )(gqkv, conv_w12)


def _gdn_kernel(xf_ref, xr_ref, gf_ref, gr_ref, arow_ref, drow_ref, acol_ref, dcol_ref, of_ref, or_ref, s_sc, *, nb):
    @pl.when(pl.program_id(0) == 0)
    def _():
        s_sc[...] = jnp.zeros(s_sc.shape, F32)

    eye_g = _eye_rows(16, GATE_W, GC_GA)
    eye_h = _eye_rows(HEAD_DIM, HEAD_DIM, 0)
    for d in range(N_DIR):
        x_ref, g_ref, o_ref = ((xf_ref, gf_ref, of_ref), (xr_ref, gr_ref, or_ref))[d]
        incl, strict, tri, tri_t = _dir_masks(d)
        for b in range(nb):
            g = g_ref[b]
            gval = arow_ref[...] * _softplus(g + drow_ref[...])
            beta_all = _sigmoid(g)
            gcum = _dot(tri, gval, precision=HIGHEST)
            g_t = _dot_nt(eye_g, g, precision=HIGHEST)
            gval_t = acol_ref[...] * _softplus(g_t + dcol_ref[...])
            gcum_t = _dot(gval_t, tri_t, precision=HIGHEST)
            for h in range(GDN_HEADS):
                p = (d * nb + b) * GDN_HEADS + h
                ra = d * GDN_HEADS + h
                g_col = gcum[:, GC_GA + ra:GC_GA + ra + 1]
                g_row = gcum_t[ra:ra + 1, :]
                beta = beta_all[:, GC_GB + ra:GC_GB + ra + 1]
                g_last = jnp.sum(gval_t[ra:ra + 1, :], axis=-1, keepdims=True)
                q = x_ref[b, h]
                k = x_ref[b, GDN_HEADS + h]
                v = x_ref[b, 2 * GDN_HEADS + h]
                k_t = _dot_nt(eye_h, k, precision=HIGHEST)
                decay = jnp.where(incl, jnp.exp(jnp.where(incl, g_col - g_row, 0.0)), 0.0)
                kb = k * beta
                pw = -jnp.where(strict, _dot3(kb, k_t) * decay, 0.0)
                u = v * beta
                w = kb * jnp.exp(g_col)
                n_fac = CHUNK.bit_length() - 1
                for it in range(n_fac):
                    u = u + _dot3(pw, u)
                    w = w + _dot3(pw, w)
                    if it < n_fac - 1:
                        pw = _dot3(pw, pw)
                s_old = s_sc[p]
                v_new = u - _dot(w, s_old)
                o_ref[b, h] = _dot(q * jnp.exp(g_col), s_old) + _dot(_dot(q, k_t) * decay, v_new)
                s_sc[p] = jnp.exp(g_last) * s_old + _dot(k_t * jnp.exp(g_last - g_row), v_new)


def _gdn_scan(gp, gates, arow, drow, acol, dcol, ctx_len):
    b, _, t, _ = gp.shape
    nc_c, nc_x = ctx_len // CHUNK, (t - ctx_len) // CHUNK
    fwd, rev = _chunk_maps(nc_c, nc_x)
    nprob = N_DIR * b * GDN_HEADS
    xs = lambda f: pl.BlockSpec((b, 12, CHUNK, HEAD_DIM), lambda j: (0, 0, f(j), 0))
    gs = lambda f: pl.BlockSpec((b, CHUNK, GATE_W), lambda j: (0, f(j), 0))
    hs = lambda f: pl.BlockSpec((b, GDN_HEADS, CHUNK, HEAD_DIM), lambda j: (0, 0, f(j), 0))
    row = pl.BlockSpec((1, GATE_W), lambda j: (0, 0))
    col = pl.BlockSpec((16, CHUNK), lambda j: (0, 0))
    out = jax.ShapeDtypeStruct((b, GDN_HEADS, t, HEAD_DIM), F32)
    return pl.pallas_call(
        functools.partial(_gdn_kernel, nb=b),
        grid=(nc_c + nc_x,),
        in_specs=[xs(fwd), xs(rev), gs(fwd), gs(rev), row, row, col, col],
        out_specs=(hs(fwd), hs(rev)),
        out_shape=(out, out),
        scratch_shapes=[pltpu.VMEM((nprob, HEAD_DIM, HEAD_DIM), F32)],
        compiler_params=_cparams(("arbitrary",)),
        name="gdn_scan",
    )(gp, gp, gates, gates, arow, drow, acol, dcol)


def _outproj_kernel(x_ref, mod_ref, ao_ref, mhf_ref, mhr_ref, mo_ref, mg_ref, ghf_ref, ghr_ref, gz_ref, gg_ref,
                    w_ref, n2_ref, rw_ref, x1_ref, hp_ref, aff_ref, *, d_model):
    d = d_model
    acc = jnp.zeros(x_ref.shape[1:], F32)
    for h in range(ATTN_HEADS):
        acc = acc + _dot(ao_ref[0, h], w_ref[h])
    for h in range(MLSTM_HEADS):
        hh = mhf_ref[0, h] + mhr_ref[0, h]
        hn = hh * lax.rsqrt(jnp.mean(hh * hh, axis=-1, keepdims=True) + EPS) * mg_ref[h:h + 1, :]
        acc = acc + _dot((_sigmoid(mo_ref[0, h]) * hn).astype(BF16), w_ref[ATTN_HEADS + h])
    for h in range(GDN_HEADS):
        oo = ghf_ref[0, h] + ghr_ref[0, h]
        on = oo * lax.rsqrt(jnp.mean(oo * oo, axis=-1, keepdims=True) + EPS) * gg_ref[...]
        acc = acc + _dot((on * _silu(gz_ref[0, h])).astype(BF16), w_ref[ATTN_HEADS + MLSTM_HEADS + h])
    x1 = x_ref[0] + mod_ref[:, 2 * d:3 * d] * acc
    x1_ref[0] = x1
    xn = x1 * lax.rsqrt(jnp.mean(x1 * x1, axis=-1, keepdims=True) + EPS) * n2_ref[...]
    h2 = xn * (1.0 + mod_ref[:, 4 * d:5 * d]) + mod_ref[:, 3 * d:4 * d]
    logits = _dot_nt(rw_ref[...], h2, precision=HIGHEST)
    e = jnp.exp(logits - jnp.max(logits, axis=0, keepdims=True))
    aff_ref[0] = e / jnp.sum(e, axis=0, keepdims=True)
    hp_ref[0] = h2


def _outproj(xa, modsel, ao, mhf, mhr, mo, mg, ghf, ghr, gz, gg, w_hm, n2, rw_t, tm, blk0):
    b, t, d = xa.shape
    nb = t // tm - blk0
    hm = lambda nh: pl.BlockSpec((1, nh, tm, HEAD_DIM), lambda bi, i: (bi, 0, i + blk0, 0))
    full = lambda a: pl.BlockSpec(a.shape, lambda bi, i: (0,) * a.ndim)
    return pl.pallas_call(
        functools.partial(_outproj_kernel, d_model=d),
        grid=(b, nb),
        in_specs=[
            pl.BlockSpec((1, tm, d), lambda bi, i: (bi, i + blk0, 0)),
            pl.BlockSpec((None, None, 1, N_MOD * d), lambda bi, i: (bi, jnp.minimum(i + blk0, 1), 0, 0)),
            hm(ATTN_HEADS), hm(4), hm(4), hm(4), full(mg), hm(4), hm(4), hm(4), full(gg),
            full(w_hm), full(n2), full(rw_t),
        ],
        out_specs=(pl.BlockSpec((1, tm, d), lambda bi, i: (bi, i + blk0, 0)),
                   pl.BlockSpec((1, tm, d), lambda bi, i: (bi, i + blk0, 0)),
                   pl.BlockSpec((1, N_EXPERTS, tm), lambda bi, i: (bi, 0, i + blk0))),
        out_shape=(jax.ShapeDtypeStruct((b, t, d), F32),
                   jax.ShapeDtypeStruct((b, t, d), F32),
                   jax.ShapeDtypeStruct((b, N_EXPERTS, t), F32)),
        compiler_params=_cparams(("parallel", "parallel")),
        name="outproj",
    )(xa, modsel, ao, mhf, mhr, mo, mg, ghf, ghr, gz, gg, w_hm, n2, rw_t)


def _route_kernel(aff_ref, idx_ref, val_ref, *, cap, rows):
    ne = N_EXPERTS
    a = aff_ref[0]

    def count(mask):
        return jnp.sum(jnp.sum(mask.astype(I32), axis=2, keepdims=True), axis=1, keepdims=True)

    tau_bits = jnp.zeros((ne, 1, 1), I32)
    for bit in range(30, -1, -1):
        cand = tau_bits | (1 << bit)
        keep = count(a >= lax.bitcast_convert_type(cand, F32)) >= cap
        tau_bits = jnp.where(keep, cand, tau_bits)
    tau = lax.bitcast_convert_type(tau_bits, F32)
    gt = a > tau
    eq = a == tau
    need = cap - count(gt)

    triu = (_iota((LANES, LANES), 0) <= _iota((LANES, LANES), 1)).astype(BF16)
    strict_lower = (_iota((rows, rows), 0) > _iota((rows, rows), 1)).astype(BF16)
    triu_r = (_iota((rows, rows), 0) <= _iota((rows, rows), 1)).astype(BF16)
    ones_r = jnp.ones((SUBLANES, LANES), BF16)

    def prefix(mask2d):
        m = mask2d.astype(BF16)
        within = _dot(m, triu)
        tot = jnp.broadcast_to(within[:, LANES - 1:LANES], (rows, LANES)).astype(BF16)
        return within, _dot(strict_lower, tot)

    lane_r = _iota((cap, rows), 1).astype(F32)
    lane_l = _iota((cap, LANES), 1).astype(F32)
    slot = _iota((cap, 1), 0).astype(F32)
    for e in range(ne):
        eq_e = eq[e]
        w_eq, before_eq = prefix(eq_e)
        rank_eq = w_eq - eq_e.astype(F32) + before_eq
        sel = gt[e] | (eq_e & (rank_eq < need[e].astype(F32)))
        rel, _ = prefix(sel)
        sel_b = sel.astype(BF16)
        row_tot = _dot_nt(ones_r, sel_b)
        row_incl = _dot(row_tot.astype(BF16), triu_r)
        row_excl = row_incl - row_tot
        kstar = jnp.sum((row_incl[0:1, :] <= slot).astype(F32), axis=-1, keepdims=True)
        onehot = (lane_r == kstar).astype(F32)
        base = jnp.sum(onehot * row_excl[0:1, :], axis=-1, keepdims=True)
        g_rel = _dot(onehot.astype(BF16), rel.astype(BF16))
        within = jnp.sum((g_rel <= slot - base).astype(F32), axis=-1, keepdims=True)
        g_aff = _dot(onehot, a[e], precision=HIGHEST)
        val_ref[0, e] = jnp.sum(jnp.where(lane_l == within, g_aff, 0.0), axis=-1, keepdims=True)
        idx_ref[0, e] = (kstar * LANES + within).astype(I32)


def _route(aff_tiles, cap):
    b, ne, rows, _ = aff_tiles.shape
    return pl.pallas_call(
        functools.partial(_route_kernel, cap=cap, rows=rows),
        grid=(b,),
        in_specs=[pl.BlockSpec((1, ne, rows, LANES), lambda bi: (bi, 0, 0, 0))],
        out_specs=(pl.BlockSpec((1, ne, cap, 1), lambda bi: (bi, 0, 0, 0)),
                   pl.BlockSpec((1, ne, cap, 1), lambda bi: (bi, 0, 0, 0))),
        out_shape=(jax.ShapeDtypeStruct((b, ne, cap, 1), I32), jax.ShapeDtypeStruct((b, ne, cap, 1), F32)),
        compiler_params=_cparams(("parallel",)),
        name="route",
    )(aff_tiles)


def _gather_kernel(idx_ref, h_ref, o_ref, rows_sc, *, cap, row_off):
    base = (pl.program_id(0) * pl.num_programs(2) + pl.program_id(2)) * cap

    def body(s, carry):
        t = idx_ref[base + s] + row_off
        rows_sc[pl.ds(s, 1), :] = h_ref[0, pl.ds(t, 1), :]
        return carry

    lax.fori_loop(0, cap, body, 0, unroll=8)
    o_ref[0, 0] = rows_sc[...].astype(BF16)


def _gather(idx_flat, h2, cap, rows_block, row_off):
    b, _, d = h2.shape
    dh = d // 2
    return pl.pallas_call(
        functools.partial(_gather_kernel, cap=cap, row_off=row_off),
        grid_spec=pltpu.PrefetchScalarGridSpec(
            num_scalar_prefetch=1,
            grid=(b, 2, N_EXPERTS),
            in_specs=[pl.BlockSpec((1, rows_block, dh), lambda bi, c, e, idx: (bi, 0, c))],
            out_specs=pl.BlockSpec((1, 1, cap, dh), lambda bi, c, e, idx: (e, bi, 0, c)),
            scratch_shapes=[pltpu.VMEM((cap, dh), F32)],
        ),
        out_shape=jax.ShapeDtypeStruct((N_EXPERTS, b, cap, d), BF16),
        compiler_params=_cparams(("arbitrary", "arbitrary", "arbitrary")),
        name="moe_gather",
    )(idx_flat, h2)


def _ffn_kernel(*refs, n_streams):
    xg_refs = refs[0:n_streams]
    val_refs = refs[n_streams:2 * n_streams]
    w1_ref, w3_ref, w2_ref = refs[2 * n_streams:2 * n_streams + 3]
    y_refs = refs[2 * n_streams + 3:]
    f = pl.program_id(1)
    w1 = w1_ref[0]
    w3 = w3_ref[0]
    w2 = w2_ref[0]
    for xg_ref, val_ref, y_ref in zip(xg_refs, val_refs, y_refs):
        xg = xg_ref[0].reshape(-1, w1.shape[0])
        y = _dot((_silu(_dot(xg, w1)) * _dot(xg, w3)).astype(BF16), w2)

        @pl.when(f == 0)
        def _():
            y_ref[0] = y

        @pl.when(f > 0)
        def _():
            y_ref[0] = y_ref[0] + y

        @pl.when(f == pl.num_programs(1) - 1)
        def _():
            y_ref[0] = y_ref[0] * val_ref[0]


def _ffn(xgs, vals, w1, w3, w2, tf):
    ne, d, ff = w1.shape
    n_streams = len(xgs)
    in_specs, out_specs, out_shapes = [], [], []
    for xg in xgs:
        in_specs.append(pl.BlockSpec((1,) + xg.shape[1:], lambda e, f: (e, 0, 0, 0)))
    for v in vals:
        in_specs.append(pl.BlockSpec((1,) + v.shape[1:], lambda e, f: (e, 0, 0)))
    in_specs += [pl.BlockSpec((1, d, tf), lambda e, f: (e, 0, f)),
                 pl.BlockSpec((1, d, tf), lambda e, f: (e, 0, f)),
                 pl.BlockSpec((1, tf, d), lambda e, f: (e, f, 0))]
    for xg in xgs:
        m = xg.shape[1] * xg.shape[2]
        out_specs.append(pl.BlockSpec((1, m, d), lambda e, f: (e, 0, 0)))
        out_shapes.append(jax.ShapeDtypeStruct((ne, m, d), F32))
    return pl.pallas_call(
        functools.partial(_ffn_kernel, n_streams=n_streams),
        grid=(ne, ff // tf),
        in_specs=in_specs,
        out_specs=tuple(out_specs),
        out_shape=tuple(out_shapes),
        compiler_params=_cparams(("parallel", "arbitrary")),
        name="moe_ffn",
    )(*xgs, *vals, w1, w3, w2)


def _combine_kernel(idx_ref, y_ref, o_ref, *, cap):
    e = pl.program_id(2)

    @pl.when(e == 0)
    def _():
        o_ref[...] = jnp.zeros(o_ref.shape, F32)

    base = (pl.program_id(0) * pl.num_programs(2) + e) * cap

    def body(s, carry):
        t = idx_ref[base + s]
        o_ref[0, pl.ds(t, 1), :] = o_ref[0, pl.ds(t, 1), :] + y_ref[0, pl.ds(s, 1), :]
        return carry

    lax.fori_loop(0, cap, body, 0, unroll=8)


def _combine(idx_flat, y, b, n_tok, cap):
    ne, _, d = y.shape
    dh = d // 2
    return pl.pallas_call(
        functools.partial(_combine_kernel, cap=cap),
        grid_spec=pltpu.PrefetchScalarGridSpec(
            num_scalar_prefetch=1,
            grid=(b, 2, ne),
            in_specs=[pl.BlockSpec((1, cap, dh), lambda bi, c, e, idx: (e, bi, c))],
            out_specs=pl.BlockSpec((1, n_tok, dh), lambda bi, c, e, idx: (bi, 0, c)),
        ),
        out_shape=jax.ShapeDtypeStruct((b, n_tok, d), F32),
        compiler_params=_cparams(("arbitrary", "arbitrary", "arbitrary")),
        name="moe_combine",
    )(idx_flat, y)


def _residual_kernel(x1_ref, mod_ref, mx_ref, mc_ref, o_ref, *, d_model, blk0):
    i = pl.program_id(1) + blk0
    moe = jnp.where(i == 0, mc_ref[0], mx_ref[0])
    o_ref[0] = x1_ref[0] + mod_ref[:, 5 * d_model:6 * d_model] * moe


def _residual(x1, modsel, moe_x, moe_c, tm, blk0):
    b, t, d = x1.shape
    nb = t // tm - blk0
    return pl.pallas_call(
        functools.partial(_residual_kernel, d_model=d, blk0=blk0),
        grid=(b, nb),
        in_specs=[pl.BlockSpec((1, tm, d), lambda bi, i: (bi, i + blk0, 0)),
                  pl.BlockSpec((None, None, 1, N_MOD * d), lambda bi, i: (bi, jnp.minimum(i + blk0, 1), 0, 0)),
                  pl.BlockSpec((1, tm, d), lambda bi, i: (bi, jnp.maximum(i + blk0 - 1, 0), 0)),
                  pl.BlockSpec((1, tm, d), lambda bi, i: (bi, 0, 0))],
        out_specs=pl.BlockSpec((1, tm, d), lambda bi, i: (bi, i, 0)),
        out_shape=jax.ShapeDtypeStruct((b, nb * tm, d), F32),
        compiler_params=_cparams(("parallel", "parallel")),
        name="moe_residual",
    )(x1, modsel, moe_x, moe_c)


def _rope_tables(ctx_len, seq):
    n = jnp.arange(seq)
    pos = jnp.stack([n // GRID_W, n % GRID_W], axis=-1).astype(F32)
    lane = jnp.arange(LANES) % HEAD_DIM
    axis = lane // 32
    n_freq = HEAD_DIM // 4
    inv = ROPE_THETA ** (-(lane % n_freq).astype(F32) / n_freq)
    ang = pos[:, axis] * inv
    sign = jnp.where((lane % 32) < 16, -1.0, 1.0)
    cos_t = jnp.concatenate([jnp.ones((ctx_len, LANES), F32), jnp.cos(ang)], axis=0)
    sin_t = jnp.concatenate([jnp.zeros((ctx_len, LANES), F32), jnp.sin(ang) * sign], axis=0)
    return cos_t, sin_t


def _reorder_w_in(w):
    d = w.shape[0]
    o = ATTN_W + 2 * KV_W
    attn = w[:, :o]
    ml = w[:, o:o + 4 * MLSTM_W]
    o += 4 * MLSTM_W
    mgate = w[:, o:o + 2 * N_DIR * MLSTM_HEADS]
    o += 2 * N_DIR * MLSTM_HEADS
    gd = w[:, o:o + 4 * GDN_W]
    o += 4 * GDN_W
    ggate = w[:, o:o + 2 * N_DIR * GDN_HEADS]
    pad = jnp.zeros((d, GATE_W - mgate.shape[1] - ggate.shape[1]), w.dtype)
    return jnp.concatenate([attn, ml, gd, mgate, ggate, pad], axis=1)


def _gate_row(vals, col0):
    flat = vals.reshape(-1).astype(F32)
    return jnp.zeros((1, GATE_W), F32).at[0, col0:col0 + flat.shape[0]].set(flat)


def _gate_col(first, second):
    flat = jnp.concatenate([first.reshape(-1), second.reshape(-1)]).astype(F32)
    return jnp.broadcast_to(flat[:, None], (flat.shape[0], CHUNK))


def kernel(x, c, ctx, c_ctx, mod_w, mod_b, norm1_g, w_in, q_norm_g, k_norm_g, mlstm_i_bias, mlstm_f_bias,
           mlstm_out_g, gdn_conv_w, gdn_a_log, gdn_dt_bias, gdn_out_g, w_out, norm2_g, router_w, w1, w3, w2):
    b, seq, d = x.shape
    ctx_len = ctx.shape[1]
    depth = mod_w.shape[0]
    tm = ctx_len
    t = ctx_len + seq
    ne = N_EXPERTS
    cap_x = CAPACITY_FACTOR * seq // ne
    cap_c = CAPACITY_FACTOR * ctx_len // ne
    rows_c = 2 * SUBLANES

    cvec = jnp.concatenate([c, c_ctx[None, :], jnp.zeros((SUBLANES - b - 1, d), F32)], axis=0)
    mod = _modulation(cvec, mod_w, mod_b)
    cos_t, sin_t = _rope_tables(ctx_len, seq)
    bd = jnp.kron(jnp.eye(LANES // HEAD_DIM, dtype=F32), jnp.full((HEAD_DIM, HEAD_DIM), 1.0 / HEAD_DIM, F32))
    xa = jnp.concatenate([ctx, x], axis=1)
    zeros8 = jnp.zeros((N_DIR, MLSTM_HEADS), F32)

    for l in range(depth):
        need_ctx = l < depth - 1
        blk0 = 0 if need_ctx else 1
        modsel = jnp.stack([jnp.broadcast_to(mod[l, b], (b, N_MOD * d)), mod[l, :b]], axis=1)[:, :, None, :]
        w_r = _reorder_w_in(w_in[l]).astype(BF16)
        qg = jnp.tile(q_norm_g[l], LANES // HEAD_DIM)[None, :]
        kg = jnp.tile(k_norm_g[l], LANES // HEAD_DIM)[None, :]
        q, k, v, mqkv, mo, gqkv, gz, gates = _inproj(xa, modsel, norm1_g[l][None, :], w_r, cos_t, sin_t, qg, kg, bd, tm)

        ao = _attention(q, k, v, ctx_len, tm, min(ATTN_TK, seq), blk0)

        brow = _gate_row(mlstm_i_bias[l], GC_MI) + _gate_row(mlstm_f_bias[l], GC_MF)
        bcol = _gate_col(mlstm_i_bias[l], mlstm_f_bias[l])
        mhf, mhr = _mlstm_scan(mqkv, gates, brow, bcol, ctx_len)

        conv12 = gdn_conv_w[l].reshape(CONV_K, 3 * GDN_HEADS, HEAD_DIM).transpose(1, 0, 2)
        gp = _gdn_prep(gqkv, conv12, ctx_len)
        neg_a = -jnp.exp(gdn_a_log[l].astype(F32))
        ghf, ghr = _gdn_scan(gp, gates, _gate_row(neg_a, GC_GA), _gate_row(gdn_dt_bias[l], GC_GA),
                             _gate_col(neg_a, zeros8), _gate_col(gdn_dt_bias[l], zeros8), ctx_len)

        w_hm = w_out[l].astype(BF16).reshape(ATTN_HEADS + MLSTM_HEADS + GDN_HEADS, HEAD_DIM, d)
        x1, hp, aff = _outproj(xa, modsel, ao, mhf, mhr, mo, mlstm_out_g[l], ghf, ghr, gz, gdn_out_g[l][None, :],
                               w_hm, norm2_g[l][None, :], router_w[l].T, tm, blk0)

        w1b, w3b, w2b = w1[l].astype(BF16), w3[l].astype(BF16), w2[l].astype(BF16)
        idx_x, val_x = _route(aff[:, :, ctx_len:].reshape(b, ne, seq // LANES, LANES), cap_x)
        idx_xf = idx_x.reshape(-1)
        xgs = [_gather(idx_xf, hp, cap_x, t, ctx_len)]
        vals = [val_x.transpose(1, 0, 2, 3).reshape(ne, b * cap_x, 1)]
        if need_ctx:
            aff_c = jnp.pad(aff[:, :, :ctx_len], ((0, 0), (0, 0), (0, rows_c * LANES - ctx_len)), constant_values=-1.0)
            idx_c, val_c = _route(aff_c.reshape(b, ne, rows_c, LANES), cap_c)
            idx_cf = idx_c.reshape(-1)
            xgs.append(_gather(idx_cf, hp, cap_c, ctx_len, 0))
            vals.append(val_c.transpose(1, 0, 2, 3).reshape(ne, b * cap_c, 1))
        ys = _ffn(xgs, vals, w1b, w3b, w2b, min(FFN_TF, w1b.shape[2]))
        moe_x = _combine(idx_xf, ys[0], b, seq, cap_x)
        moe_c = _combine(idx_cf, ys[1], b, ctx_len, cap_c) if need_ctx else moe_x
        xa = _residual(x1, modsel, moe_x, moe_c, tm, blk0)
    return xa
```

```python
import functools
import math

import jax
import jax.numpy as jnp
from jax import lax
from jax.experimental import pallas as pl
from jax.experimental.pallas import tpu as pltpu

F32 = jnp.float32
BF16 = jnp.bfloat16
I32 = jnp.int32
U32 = jnp.uint32
HIGHEST = lax.Precision.HIGHEST

HEAD_DIM = 64
ATTN_HEADS = 8
ATTN_KV_HEADS = 2
ATTN_REP = ATTN_HEADS // ATTN_KV_HEADS
MLSTM_HEADS = 4
GDN_HEADS = 4
N_DIR = 2
CHUNK = 64
CONV_K = 5
GRID_W = 64
ROPE_THETA = 10000.0
N_EXPERTS = 16
CAPACITY_FACTOR = 2
N_MOD = 6
EPS = 1e-6
ATTN_W = ATTN_HEADS * HEAD_DIM
KV_W = ATTN_KV_HEADS * HEAD_DIM
MLSTM_W = MLSTM_HEADS * HEAD_DIM
GDN_W = GDN_HEADS * HEAD_DIM
LANES = 128
SUBLANES = 8
GATE_W = LANES
GATE_ROWS = 32
GC_MI, GC_MF, GC_GA, GC_GB = 0, 8, 16, 24
VMEM_LIMIT = 56 * 1024 * 1024
ATTN_COLS = 256
LOG2E = 1.4426950408889634
FFN_TF = 512
CONV_ROWS = 256
PAD_ROWS = 8


def _cparams(sem):
    return pltpu.CompilerParams(dimension_semantics=sem, vmem_limit_bytes=VMEM_LIMIT)


def _sigmoid(x):
    return 1.0 / (1.0 + jnp.exp(-x))


def _silu(x):
    return x * _sigmoid(x)


def _log_sigmoid(x):
    return jnp.minimum(x, 0.0) - jnp.log1p(jnp.exp(-jnp.abs(x)))


def _softplus(x):
    return jnp.maximum(x, 0.0) + jnp.log1p(jnp.exp(-jnp.abs(x)))


def _dot(a, b, precision=None):
    return jnp.dot(a, b, preferred_element_type=F32, precision=precision)


def _dot_nt(a, b, precision=None):
    return lax.dot_general(a, b, (((1,), (1,)), ((), ())), preferred_element_type=F32, precision=precision)


def _split_bf16(a):
    hi = a.astype(BF16)
    lo = (a - hi.astype(F32)).astype(BF16)
    return hi, lo


def _dot3(a, b):
    ah, al = _split_bf16(a)
    bh, bl = _split_bf16(b)
    return _dot(ah, bh) + (_dot(ah, bl) + _dot(al, bh))


def _iota(shape, dim):
    return lax.broadcasted_iota(I32, shape, dim)


def _eye_rows(rows, cols, first):
    return (_iota((rows, cols), 0) + first == _iota((rows, cols), 1)).astype(F32)


def _mod_kernel(c_ref, w_ref, b_ref, o_ref):
    s = _silu(c_ref[...])
    o_ref[0] = _dot(s, w_ref[0], precision=HIGHEST) + b_ref[0]


def _modulation(cvec, mod_w, mod_b):
    depth, d, n = mod_w.shape
    rows = cvec.shape[0]
    tn = d
    return pl.pallas_call(
        _mod_kernel,
        grid=(depth, n // tn),
        in_specs=[
            pl.BlockSpec((rows, d), lambda l, j: (0, 0)),
            pl.BlockSpec((1, d, tn), lambda l, j: (l, 0, j)),
            pl.BlockSpec((1, 1, tn), lambda l, j: (l, 0, j)),
        ],
        out_specs=pl.BlockSpec((1, rows, tn), lambda l, j: (l, 0, j)),
        out_shape=jax.ShapeDtypeStruct((depth, rows, n), F32),
        compiler_params=_cparams(("parallel", "parallel")),
        name="modulation",
    )(cvec, mod_w, mod_b.reshape(depth, 1, n))


def _inproj_kernel(x_ref, mod_ref, g_ref, w_ref, cos_ref, sin_ref, qg_ref, kg_ref, bd_ref,
                   q_ref, k_ref, v_ref, ml_ref, mo_ref, gd_ref, gz_ref, gate_ref, *, d_model):
    x = x_ref[0]
    sh = mod_ref[:, 0:d_model]
    sc = mod_ref[:, d_model:2 * d_model]
    xn = x * lax.rsqrt(jnp.mean(x * x, axis=-1, keepdims=True) + EPS) * g_ref[...]
    h = (xn * (1.0 + sc) + sh).astype(BF16)
    p = _dot(h, w_ref[...])

    cos = cos_ref[...]
    sin = sin_ref[...]
    first_half = (_iota(cos.shape, 1) % 32) < 16

    def norm_rope(xs, g, scale):
        ms = _dot(xs * xs, bd_ref[...], precision=HIGHEST)
        xn_ = xs * lax.rsqrt(ms + EPS) * g
        sw = jnp.where(first_half, pltpu.roll(xn_, LANES - 16, 1), pltpu.roll(xn_, 16, 1))
        return (xn_ * cos + sw * sin) * scale

    for j in range(ATTN_W // LANES):
        qs = norm_rope(p[:, j * LANES:(j + 1) * LANES], qg_ref[...], LOG2E * HEAD_DIM ** -0.5).astype(BF16)
        q_ref[0, 2 * j] = qs[:, 0:HEAD_DIM]
        q_ref[0, 2 * j + 1] = qs[:, HEAD_DIM:LANES]
    ks = norm_rope(p[:, ATTN_W:ATTN_W + KV_W], kg_ref[...], 1.0).astype(BF16)
    k_ref[0, 0] = ks[:, 0:HEAD_DIM]
    k_ref[0, 1] = ks[:, HEAD_DIM:LANES]
    vs = p[:, ATTN_W + KV_W:ATTN_W + 2 * KV_W].astype(BF16)
    v_ref[0, 0] = vs[:, 0:HEAD_DIM]
    v_ref[0, 1] = vs[:, HEAD_DIM:LANES]

    off = ATTN_W + 2 * KV_W

    def head(j):
        return p[:, off + j * HEAD_DIM: off + (j + 1) * HEAD_DIM]

    for j in range(12):
        ml_ref[0, j] = head(j) * HEAD_DIM ** -0.5 if 4 <= j < 8 else head(j)
    for j in range(4):
        mo_ref[0, j] = head(12 + j)
    off += 4 * MLSTM_W
    for j in range(12):
        gd_ref[0, j] = head(j)
    for j in range(4):
        gz_ref[0, j] = head(12 + j)
    off += 4 * GDN_W
    gate_ref[0] = p[:, off:off + GATE_W]


def _inproj(xa, modsel, g1, w_r, cos_t, sin_t, qg, kg, bd, tm):
    b, t, d = xa.shape
    nb = t // tm
    n = w_r.shape[1]
    kern = functools.partial(_inproj_kernel, d_model=d)
    hm_shape = lambda nh, dt: jax.ShapeDtypeStruct((b, nh, t, HEAD_DIM), dt)
    out_shapes = (hm_shape(ATTN_HEADS, BF16), hm_shape(ATTN_KV_HEADS, BF16), hm_shape(ATTN_KV_HEADS, BF16),
                  hm_shape(12, F32), hm_shape(4, F32), hm_shape(12, F32), hm_shape(4, F32),
                  jax.ShapeDtypeStruct((b, t, GATE_W), F32))
    hm = lambda nh: pl.BlockSpec((1, nh, tm, HEAD_DIM), lambda bi, i: (bi, 0, i, 0))
    return pl.pallas_call(
        kern,
        grid=(b, nb),
        in_specs=[
            pl.BlockSpec((1, tm, d), lambda bi, i: (bi, i, 0)),
            pl.BlockSpec((None, None, 1, N_MOD * d), lambda bi, i: (bi, jnp.minimum(i, 1), 0, 0)),
            pl.BlockSpec((1, d), lambda bi, i: (0, 0)),
            pl.BlockSpec((d, n), lambda bi, i: (0, 0)),
            pl.BlockSpec((tm, LANES), lambda bi, i: (i, 0)),
            pl.BlockSpec((tm, LANES), lambda bi, i: (i, 0)),
            pl.BlockSpec((1, LANES), lambda bi, i: (0, 0)),
            pl.BlockSpec((1, LANES), lambda bi, i: (0, 0)),
            pl.BlockSpec((LANES, LANES), lambda bi, i: (0, 0)),
        ],
        out_specs=(hm(ATTN_HEADS), hm(ATTN_KV_HEADS), hm(ATTN_KV_HEADS), hm(12), hm(4), hm(12), hm(4),
                   pl.BlockSpec((1, tm, GATE_W), lambda bi, i: (bi, i, 0))),
        out_shape=out_shapes,
        compiler_params=_cparams(("parallel", "parallel")),
        name="inproj",
    )(xa, modsel, g1, w_r, cos_t, sin_t, qg, kg, bd)


def _attn_kernel(q_ref, k_ref, vt_ref, o_ref, sa_sc, sb_sc, m_sc, l_sc, acc_sc, *, tq, tk, n_pairs, blk0):
    i = pl.program_id(2) + blk0
    q = q_ref[0].reshape(ATTN_REP * tq, HEAD_DIM)
    cols = [slice(c * ATTN_COLS, (c + 1) * ATTN_COLS) for c in range(ATTN_REP * tq // ATTN_COLS)]

    m_sc[...] = jnp.full(m_sc.shape, -jnp.inf, F32)
    l_sc[...] = jnp.zeros(l_sc.shape, F32)
    acc_sc[...] = jnp.zeros(acc_sc.shape, F32)

    def score(s_ref, tile):
        start = pl.multiple_of(tile * tk, tk)
        k = k_ref[0, 0, pl.ds(start, tk), :]
        for cs in cols:
            s_ref[:, cs] = _dot_nt(k, q[cs])

    def consume(s_ref, tile):
        start = pl.multiple_of(tile * tk, tk)
        vt = vt_ref[0, 0, :, pl.ds(start, tk)]
        m_old = [m_sc[:, cs] for cs in cols]
        l_old = [l_sc[:, cs] for cs in cols]
        acc_old = [acc_sc[:, cs] for cs in cols]
        out = []
        for cs, mo, lo, ao in zip(cols, m_old, l_old, acc_old):
            s = s_ref[:, cs]
            mn = jnp.maximum(mo, jnp.max(s, axis=0, keepdims=True))
            alpha = jnp.exp2(mo - mn)
            p = jnp.exp2(s - mn)
            out.append((mn, alpha * lo + jnp.sum(p, axis=0, keepdims=True), alpha * ao + _dot(vt, p.astype(BF16))))
        for cs, (mn, ln, an) in zip(cols, out):
            m_sc[:, cs] = mn
            l_sc[:, cs] = ln
            acc_sc[:, cs] = an

    score(sa_sc, 0)

    def body(j, carry):
        consume(sa_sc, 2 * j)
        score(sb_sc, 2 * j + 1)
        consume(sb_sc, 2 * j + 1)
        score(sa_sc, 2 * j + 2)
        return carry

    n = jnp.where(i == 0, 0, n_pairs)
    lax.fori_loop(0, n, body, 0)
    consume(sa_sc, 2 * n)
    o_t = (acc_sc[...] / l_sc[...]).astype(BF16)
    eye = (_iota((tq, tq), 0) == _iota((tq, tq), 1)).astype(BF16)
    for r in range(ATTN_REP):
        o_ref[0, r] = _dot_nt(eye, o_t[:, r * tq:(r + 1) * tq]).astype(o_ref.dtype)


def _attention(q, k, vt, ctx_len, tq, blk0):
    b, _, t, _ = q.shape
    tk = ctx_len
    assert (t // tk) % 2 == 1
    nq = t // tq - blk0
    kern = functools.partial(_attn_kernel, tq=tq, tk=tk, n_pairs=(t // tk - 1) // 2, blk0=blk0)
    rows = ATTN_REP * tq
    return pl.pallas_call(
        kern,
        grid=(b, ATTN_KV_HEADS, nq),
        in_specs=[
            pl.BlockSpec((1, ATTN_REP, tq, HEAD_DIM), lambda bi, g, i: (bi, g, i + blk0, 0)),
            pl.BlockSpec((1, 1, t, HEAD_DIM), lambda bi, g, i: (bi, g, 0, 0)),
            pl.BlockSpec((1, 1, HEAD_DIM, t), lambda bi, g, i: (bi, g, 0, 0)),
        ],
        out_specs=pl.BlockSpec((1, ATTN_REP, tq, HEAD_DIM), lambda bi, g, i: (bi, g, i + blk0, 0)),
        out_shape=jax.ShapeDtypeStruct((b, ATTN_HEADS, t, HEAD_DIM), BF16),
        scratch_shapes=[pltpu.VMEM((tk, rows), F32), pltpu.VMEM((tk, rows), F32),
                        pltpu.VMEM((1, rows), F32), pltpu.VMEM((1, rows), F32), pltpu.VMEM((HEAD_DIM, rows), F32)],
        compiler_params=_cparams(("parallel", "parallel", "arbitrary")),
        name="attention",
    )(q, k, vt)


def _chunk_maps(nc_c, nc_x):
    fwd = lambda j: j
    rev = lambda j: jnp.where(j < nc_c, nc_c - 1 - j, 2 * nc_c + nc_x - 1 - j)
    return fwd, rev


def _dir_masks(d):
    r = _iota((CHUNK, CHUNK), 0)
    c = _iota((CHUNK, CHUNK), 1)
    incl = (r >= c) if d == 0 else (r <= c)
    strict = (r > c) if d == 0 else (r < c)
    incl_t = (r <= c) if d == 0 else (r >= c)
    return incl, strict, incl.astype(F32), incl_t.astype(F32)


def _mlstm_kernel(xf_ref, xr_ref, ktf_ref, ktr_ref, gf_ref, gr_ref, gtf_ref, gtr_ref, brow_ref, bcol_ref,
                  hf_ref, hr_ref, c_sc, n_sc, m_sc, *, nb):
    @pl.when(pl.program_id(0) == 0)
    def _():
        c_sc[...] = jnp.zeros(c_sc.shape, F32)
        n_sc[...] = jnp.zeros(n_sc.shape, F32)
        m_sc[...] = jnp.zeros(m_sc.shape, F32)

    refs = ((xf_ref, ktf_ref, gf_ref, gtf_ref, hf_ref), (xr_ref, ktr_ref, gr_ref, gtr_ref, hr_ref))
    masks = [_dir_masks(d) for d in range(N_DIR)]
    gate = {}
    for d in range(N_DIR):
        _, _, g_ref, gt_ref, _ = refs[d]
        _, _, tri, tri_t = masks[d]
        for b in range(nb):
            cum = _dot(tri, _log_sigmoid(g_ref[b] + brow_ref[...]), precision=HIGHEST)
            g_t = gt_ref[b, 0, GC_MI:GC_MI + 16, :] + bcol_ref[...]
            lf_t = _log_sigmoid(g_t)
            gate[d, b] = (cum, g_t, lf_t, _dot(lf_t, tri_t, precision=HIGHEST))

    probs = [(d, b, h) for d in range(N_DIR) for b in range(nb) for h in range(MLSTM_HEADS)]
    st = []
    for d, b, h in probs:
        p = (d * nb + b) * MLSTM_HEADS + h
        x_ref, kt_ref = refs[d][0], refs[d][1]
        cum, g_t, lf_t, cum_t = gate[d, b]
        ri = d * MLSTM_HEADS + h
        rf = GC_MF + ri
        bcum_col = cum[:, rf:rf + 1]
        bcum_row = cum_t[rf - GC_MI:rf - GC_MI + 1, :]
        i_row = g_t[ri:ri + 1, :]
        b_last = jnp.sum(lf_t[rf - GC_MI:rf - GC_MI + 1, :], axis=-1, keepdims=True)
        m_old = m_sc[p, 0:1, 0:1]
        w_end = b_last - bcum_row + i_row
        m_new = jnp.maximum(b_last + m_old, jnp.max(w_end, axis=-1, keepdims=True))
        a_row = jnp.exp(w_end - m_new)
        dec = jnp.exp(b_last + m_old - m_new)
        dmat = jnp.where(masks[d][0], bcum_col - bcum_row + i_row, -jnp.inf)
        inter = bcum_col + m_old
        m_t = jnp.maximum(inter, jnp.max(dmat, axis=-1, keepdims=True))
        st.append(dict(p=p, q=x_ref[b, h], k=x_ref[b, MLSTM_HEADS + h], v=x_ref[b, 2 * MLSTM_HEADS + h],
                       k_t=kt_ref[b, h, 0], c_old=c_sc[p], n_old=n_sc[p], a_row=a_row, dec=dec, m_new=m_new,
                       m_t=m_t, w_in=jnp.exp(inter - m_t), dexp=jnp.exp(dmat - m_t), out=refs[d][4], b=b, h=h))
    for e in st:
        e["s"] = _dot(e["q"], e["k_t"]) * e["dexp"]
    for e in st:
        e["qc"] = _dot(e["q"], e["c_old"])
    for e in st:
        e["sv"] = _dot(e["s"], e["v"])
    for e in st:
        e["kv"] = _dot(e["k_t"] * e["a_row"], e["v"])
    for e in st:
        e["ak"] = _dot(jnp.broadcast_to(e["a_row"], (SUBLANES, CHUNK)), e["k"])
    for e in st:
        num = e["w_in"] * e["qc"] + e["sv"]
        den = (e["w_in"] * jnp.sum(e["q"] * e["n_old"][0:1, :], axis=-1, keepdims=True)
               + jnp.sum(e["s"], axis=-1, keepdims=True))
        e["out"][e["b"], e["h"]] = num / jnp.maximum(jnp.abs(den), jnp.exp(-e["m_t"]))
    for e in st:
        p = e["p"]
        c_sc[p] = e["dec"] * e["c_old"] + e["kv"]
        n_sc[p] = e["dec"] * e["n_old"] + e["ak"]
        m_sc[p] = jnp.broadcast_to(e["m_new"], (SUBLANES, LANES))


def _mlstm_scan(mqkv, k_t, gates, gates_t, brow, bcol, ctx_len):
    b, _, t, _ = mqkv.shape
    nc_c, nc_x = ctx_len // CHUNK, (t - ctx_len) // CHUNK
    fwd, rev = _chunk_maps(nc_c, nc_x)
    nprob = N_DIR * b * MLSTM_HEADS
    xs = lambda f: pl.BlockSpec((b, 12, CHUNK, HEAD_DIM), lambda j: (0, 0, f(j), 0))
    ks = lambda f: pl.BlockSpec((b, MLSTM_HEADS, 1, HEAD_DIM, CHUNK), lambda j: (0, 0, f(j), 0, 0))
    gs = lambda f: pl.BlockSpec((b, CHUNK, GATE_W), lambda j: (0, f(j), 0))
    gts = lambda f: pl.BlockSpec((b, 1, GATE_ROWS, CHUNK), lambda j: (0, f(j), 0, 0))
    hs = lambda f: pl.BlockSpec((b, MLSTM_HEADS, CHUNK, HEAD_DIM), lambda j: (0, 0, f(j), 0))
    out = jax.ShapeDtypeStruct((b, MLSTM_HEADS, t, HEAD_DIM), F32)
    return pl.pallas_call(
        functools.partial(_mlstm_kernel, nb=b),
        grid=(nc_c + nc_x,),
        in_specs=[xs(fwd), xs(rev), ks(fwd), ks(rev), gs(fwd), gs(rev), gts(fwd), gts(rev),
                  pl.BlockSpec((1, GATE_W), lambda j: (0, 0)),
                  pl.BlockSpec((16, CHUNK), lambda j: (0, 0))],
        out_specs=(hs(fwd), hs(rev)),
        out_shape=(out, out),
        scratch_shapes=[pltpu.VMEM((nprob, HEAD_DIM, HEAD_DIM), F32),
                        pltpu.VMEM((nprob, SUBLANES, HEAD_DIM), F32),
                        pltpu.VMEM((nprob, SUBLANES, LANES), F32)],
        compiler_params=_cparams(("arbitrary",)),
        name="mlstm_scan",
    )(mqkv, mqkv, k_t, k_t, gates, gates, gates_t, gates_t, brow, bcol)


def _gdn_prep_kernel(x_ref, w_ref, o_ref, pad_sc, *, ctx_len, t):
    part = pl.program_id(1) // GDN_HEADS
    zeros = jnp.zeros((PAD_ROWS, HEAD_DIM), F32)
    pad_sc[0:PAD_ROWS] = zeros
    pad_sc[PAD_ROWS:PAD_ROWS + ctx_len] = x_ref[0, 0, 0:ctx_len]
    pad_sc[PAD_ROWS + ctx_len:2 * PAD_ROWS + ctx_len] = zeros
    pad_sc[2 * PAD_ROWS + ctx_len:2 * PAD_ROWS + t] = x_ref[0, 0, ctx_len:t]
    pad_sc[2 * PAD_ROWS + t:3 * PAD_ROWS + t] = zeros
    w = w_ref[0]
    is_qk = part < 2
    scale = jnp.where(part == 0, HEAD_DIM ** -0.5, 1.0)
    for c in range(t // CONV_ROWS):
        r0 = c * CONV_ROWS
        base = r0 + (PAD_ROWS if r0 < ctx_len else 2 * PAD_ROWS) - CONV_K // 2
        y = w[0:1, :] * pad_sc[base:base + CONV_ROWS]
        for j in range(1, CONV_K):
            y = y + w[j:j + 1, :] * pad_sc[base + j:base + j + CONV_ROWS]
        y = _silu(y)
        yn = y * lax.rsqrt(jnp.sum(y * y, axis=-1, keepdims=True) + EPS) * scale
        o_ref[0, 0, r0:r0 + CONV_ROWS] = jnp.where(is_qk, yn, y)


def _gdn_prep(gqkv, conv_w12, ctx_len):
    b, np_, t, _ = gqkv.shape
    return pl.pallas_call(
        functools.partial(_gdn_prep_kernel, ctx_len=ctx_len, t=t),
        grid=(b, np_),
        in_specs=[pl.BlockSpec((1, 1, t, HEAD_DIM), lambda bi, p: (bi, p, 0, 0)),
                  pl.BlockSpec((1, CONV_K, HEAD_DIM), lambda bi, p: (p, 0, 0))],
        out_specs=pl.BlockSpec((1, 1, t, HEAD_DIM), lambda bi, p: (bi, p, 0, 0)),
        out_shape=jax.ShapeDtypeStruct(gqkv.shape, F32),
        scratch_shapes=[pltpu.VMEM((t + 3 * PAD_ROWS, HEAD_DIM), F32)],
        compiler_params=_cparams(("parallel", "parallel")),
        name="gdn_prep",
    )(gqkv, conv_w12)


def _gdn_kernel(xf_ref, xr_ref, vkf_ref, vkr_ref, ktf_ref, ktr_ref, gf_ref, gr_ref, gtf_ref, gtr_ref,
                arow_ref, drow_ref, acol_ref, dcol_ref, of_ref, or_ref, s_sc, *, nb):
    @pl.when(pl.program_id(0) == 0)
    def _():
        s_sc[...] = jnp.zeros(s_sc.shape, F32)

    refs = ((xf_ref, vkf_ref, ktf_ref, gf_ref, gtf_ref, of_ref), (xr_ref, vkr_ref, ktr_ref, gr_ref, gtr_ref, or_ref))
    masks = [_dir_masks(d) for d in range(N_DIR)]
    gate = {}
    for d in range(N_DIR):
        g_ref, gt_ref = refs[d][3], refs[d][4]
        _, _, tri, tri_t = masks[d]
        for b in range(nb):
            g = g_ref[b]
            gval = arow_ref[...] * _softplus(g + drow_ref[...])
            g_t = gt_ref[b, 0, GC_GA:GC_GA + 16, :]
            gval_t = acol_ref[...] * _softplus(g_t + dcol_ref[...])
            gate[d, b] = (_dot(tri, gval, precision=HIGHEST), _sigmoid(g), gval_t,
                          _dot(gval_t, tri_t, precision=HIGHEST))

    probs = [(d, b, h) for d in range(N_DIR) for b in range(nb) for h in range(GDN_HEADS)]
    second_half = _iota((CHUNK, 2 * HEAD_DIM), 1) >= HEAD_DIM
    st = []
    for d, b, h in probs:
        p = (d * nb + b) * GDN_HEADS + h
        x_ref, vk_ref, kt_ref = refs[d][0], refs[d][1], refs[d][2]
        incl, strict, _, _ = masks[d]
        gcum, beta_all, gval_t, gcum_t = gate[d, b]
        ra = d * GDN_HEADS + h
        g_col = gcum[:, GC_GA + ra:GC_GA + ra + 1]
        g_row = gcum_t[ra:ra + 1, :]
        beta = beta_all[:, GC_GB + ra:GC_GB + ra + 1]
        g_last = jnp.sum(gval_t[ra:ra + 1, :], axis=-1, keepdims=True)
        k_t = kt_ref[b, h, 0]
        eg = jnp.exp(g_col)
        st.append(dict(p=p, b=b, h=h, out=refs[d][5], strict=strict, k_t=k_t, eg_last=jnp.exp(g_last),
                       decay=jnp.where(incl, jnp.exp(jnp.where(incl, g_col - g_row, 0.0)), 0.0),
                       kb=x_ref[b, GDN_HEADS + h] * beta, qg=x_ref[b, h] * eg, q=x_ref[b, h],
                       ktg=k_t * jnp.exp(g_last - g_row), s_old=s_sc[p],
                       sol=vk_ref[b, h] * (beta * jnp.where(second_half, eg, 1.0))))
    for e in st:
        e["pw"] = -jnp.where(e["strict"], _dot3(e["kb"], e["k_t"]) * e["decay"], 0.0)
    n_fac = CHUNK.bit_length() - 1
    for it in range(n_fac):
        for e in st:
            e["sol"] = e["sol"] + _dot3(e["pw"], e["sol"])
        if it < n_fac - 1:
            for e in st:
                e["pw"] = _dot3(e["pw"], e["pw"])
    for e in st:
        e["ws"] = _dot(e["sol"][:, HEAD_DIM:], e["s_old"])
    for e in st:
        e["qs"] = _dot(e["qg"], e["s_old"])
    for e in st:
        e["qk"] = _dot(e["q"], e["k_t"]) * e["decay"]
    for e in st:
        e["v_new"] = e["sol"][:, :HEAD_DIM] - e["ws"]
    for e in st:
        e["out"][e["b"], e["h"]] = e["qs"] + _dot(e["qk"], e["v_new"])
    for e in st:
        s_sc[e["p"]] = e["eg_last"] * e["s_old"] + _dot(e["ktg"], e["v_new"])


def _gdn_scan(gqk, vk, k_t, gates, gates_t, arow, drow, acol, dcol, ctx_len):
    b, _, t, _ = gqk.shape
    nc_c, nc_x = ctx_len // CHUNK, (t - ctx_len) // CHUNK
    fwd, rev = _chunk_maps(nc_c, nc_x)
    nprob = N_DIR * b * GDN_HEADS
    xs = lambda f: pl.BlockSpec((b, 2 * GDN_HEADS, CHUNK, HEAD_DIM), lambda j: (0, 0, f(j), 0))
    vks = lambda f: pl.BlockSpec((b, GDN_HEADS, CHUNK, 2 * HEAD_DIM), lambda j: (0, 0, f(j), 0))
    ks = lambda f: pl.BlockSpec((b, GDN_HEADS, 1, HEAD_DIM, CHUNK), lambda j: (0, 0, f(j), 0, 0))
    gs = lambda f: pl.BlockSpec((b, CHUNK, GATE_W), lambda j: (0, f(j), 0))
    gts = lambda f: pl.BlockSpec((b, 1, GATE_ROWS, CHUNK), lambda j: (0, f(j), 0, 0))
    hs = lambda f: pl.BlockSpec((b, GDN_HEADS, CHUNK, HEAD_DIM), lambda j: (0, 0, f(j), 0))
    row = pl.BlockSpec((1, GATE_W), lambda j: (0, 0))
    col = pl.BlockSpec((16, CHUNK), lambda j: (0, 0))
    out = jax.ShapeDtypeStruct((b, GDN_HEADS, t, HEAD_DIM), F32)
    return pl.pallas_call(
        functools.partial(_gdn_kernel, nb=b),
        grid=(nc_c + nc_x,),
        in_specs=[xs(fwd), xs(rev), vks(fwd), vks(rev), ks(fwd), ks(rev), gs(fwd), gs(rev), gts(fwd), gts(rev),
                  row, row, col, col],
        out_specs=(hs(fwd), hs(rev)),
        out_shape=(out, out),
        scratch_shapes=[pltpu.VMEM((nprob, HEAD_DIM, HEAD_DIM), F32)],
        compiler_params=_cparams(("arbitrary",)),
        name="gdn_scan",
    )(gqk, gqk, vk, vk, k_t, k_t, gates, gates, gates_t, gates_t, arow, drow, acol, dcol)


def _outproj_kernel(x_ref, mod_ref, ao_ref, mhf_ref, mhr_ref, mo_ref, mg_ref, ghf_ref, ghr_ref, gz_ref, gg_ref,
                    w_ref, n2_ref, rw_ref, x1_ref, hp_ref, aff_ref, *, d_model):
    d = d_model
    acc = jnp.zeros(x_ref.shape[1:], F32)
    for h in range(ATTN_HEADS):
        acc = acc + _dot(ao_ref[0, h], w_ref[h])
    for h in range(MLSTM_HEADS):
        hh = mhf_ref[0, h] + mhr_ref[0, h]
        hn = hh * lax.rsqrt(jnp.mean(hh * hh, axis=-1, keepdims=True) + EPS) * mg_ref[h:h + 1, :]
        acc = acc + _dot((_sigmoid(mo_ref[0, h]) * hn).astype(BF16), w_ref[ATTN_HEADS + h])
    for h in range(GDN_HEADS):
        oo = ghf_ref[0, h] + ghr_ref[0, h]
        on = oo * lax.rsqrt(jnp.mean(oo * oo, axis=-1, keepdims=True) + EPS) * gg_ref[...]
        acc = acc + _dot((on * _silu(gz_ref[0, h])).astype(BF16), w_ref[ATTN_HEADS + MLSTM_HEADS + h])
    x1 = x_ref[0] + mod_ref[:, 2 * d:3 * d] * acc
    x1_ref[0] = x1
    xn = x1 * lax.rsqrt(jnp.mean(x1 * x1, axis=-1, keepdims=True) + EPS) * n2_ref[...]
    h2 = xn * (1.0 + mod_ref[:, 4 * d:5 * d]) + mod_ref[:, 3 * d:4 * d]
    logits = _dot_nt(rw_ref[...], h2, precision=HIGHEST)
    e = jnp.exp(logits - jnp.max(logits, axis=0, keepdims=True))
    aff_ref[0] = e / jnp.sum(e, axis=0, keepdims=True)
    hp_ref[0] = h2


def _outproj(xa, modsel, ao, mhf, mhr, mo, mg, ghf, ghr, gz, gg, w_hm, n2, rw_t, tm, blk0):
    b, t, d = xa.shape
    nb = t // tm - blk0
    hm = lambda nh: pl.BlockSpec((1, nh, tm, HEAD_DIM), lambda bi, i: (bi, 0, i + blk0, 0))
    full = lambda a: pl.BlockSpec(a.shape, lambda bi, i: (0,) * a.ndim)
    return pl.pallas_call(
        functools.partial(_outproj_kernel, d_model=d),
        grid=(b, nb),
        in_specs=[
            pl.BlockSpec((1, tm, d), lambda bi, i: (bi, i + blk0, 0)),
            pl.BlockSpec((None, None, 1, N_MOD * d), lambda bi, i: (bi, jnp.minimum(i + blk0, 1), 0, 0)),
            hm(ATTN_HEADS), hm(4), hm(4), hm(4), full(mg), hm(4), hm(4), hm(4), full(gg),
            full(w_hm), full(n2), full(rw_t),
        ],
        out_specs=(pl.BlockSpec((1, tm, d), lambda bi, i: (bi, i + blk0, 0)),
                   pl.BlockSpec((1, tm, d), lambda bi, i: (bi, i + blk0, 0)),
                   pl.BlockSpec((1, N_EXPERTS, tm), lambda bi, i: (bi, 0, i + blk0))),
        out_shape=(jax.ShapeDtypeStruct((b, t, d), F32),
                   jax.ShapeDtypeStruct((b, t, d), F32),
                   jax.ShapeDtypeStruct((b, N_EXPERTS, t), F32)),
        compiler_params=_cparams(("parallel", "parallel")),
        name="outproj",
    )(xa, modsel, ao, mhf, mhr, mo, mg, ghf, ghr, gz, gg, w_hm, n2, rw_t)


def _route_kernel(aff_ref, idx_ref, val_ref, *, cap, rows):
    ne = N_EXPERTS
    a = aff_ref[0]

    def count(mask):
        return jnp.sum(jnp.sum(mask.astype(I32), axis=2, keepdims=True), axis=1, keepdims=True)

    tau_bits = jnp.zeros((ne, 1, 1), I32)
    for bit in range(30, -1, -1):
        cand = tau_bits | (1 << bit)
        keep = count(a >= lax.bitcast_convert_type(cand, F32)) >= cap
        tau_bits = jnp.where(keep, cand, tau_bits)
    tau = lax.bitcast_convert_type(tau_bits, F32)
    gt = a > tau
    eq = a == tau
    need = cap - count(gt)

    triu = (_iota((LANES, LANES), 0) <= _iota((LANES, LANES), 1)).astype(BF16)
    strict_lower = (_iota((rows, rows), 0) > _iota((rows, rows), 1)).astype(BF16)
    triu_r = (_iota((rows, rows), 0) <= _iota((rows, rows), 1)).astype(BF16)
    ones_r = jnp.ones((SUBLANES, LANES), BF16)

    def prefix(mask2d):
        m = mask2d.astype(BF16)
        within = _dot(m, triu)
        tot = jnp.broadcast_to(within[:, LANES - 1:LANES], (rows, LANES)).astype(BF16)
        return within, _dot(strict_lower, tot)

    lane_r = _iota((cap, rows), 1).astype(F32)
    lane_l = _iota((cap, LANES), 1).astype(F32)
    slot = _iota((cap, 1), 0).astype(F32)
    for e in range(ne):
        eq_e = eq[e]
        w_eq, before_eq = prefix(eq_e)
        rank_eq = w_eq - eq_e.astype(F32) + before_eq
        sel = gt[e] | (eq_e & (rank_eq < need[e].astype(F32)))
        rel, _ = prefix(sel)
        sel_b = sel.astype(BF16)
        row_tot = _dot_nt(ones_r, sel_b)
        row_incl = _dot(row_tot.astype(BF16), triu_r)
        row_excl = row_incl - row_tot
        kstar = jnp.sum((row_incl[0:1, :] <= slot).astype(F32), axis=-1, keepdims=True)
        onehot = (lane_r == kstar).astype(F32)
        base = jnp.sum(onehot * row_excl[0:1, :], axis=-1, keepdims=True)
        g_rel = _dot(onehot.astype(BF16), rel.astype(BF16))
        within = jnp.sum((g_rel <= slot - base).astype(F32), axis=-1, keepdims=True)
        g_aff = _dot(onehot, a[e], precision=HIGHEST)
        val_ref[0, e] = jnp.sum(jnp.where(lane_l == within, g_aff, 0.0), axis=-1, keepdims=True)
        idx_ref[0, e] = (kstar * LANES + within).astype(I32)


def _route(aff_tiles, cap):
    b, ne, rows, _ = aff_tiles.shape
    return pl.pallas_call(
        functools.partial(_route_kernel, cap=cap, rows=rows),
        grid=(b,),
        in_specs=[pl.BlockSpec((1, ne, rows, LANES), lambda bi: (bi, 0, 0, 0))],
        out_specs=(pl.BlockSpec((1, ne, cap, 1), lambda bi: (bi, 0, 0, 0)),
                   pl.BlockSpec((1, ne, cap, 1), lambda bi: (bi, 0, 0, 0))),
        out_shape=(jax.ShapeDtypeStruct((b, ne, cap, 1), I32), jax.ShapeDtypeStruct((b, ne, cap, 1), F32)),
        compiler_params=_cparams(("parallel",)),
        name="route",
    )(aff_tiles)


def _gather_kernel(idx_ref, h_ref, o_ref, rows_sc, *, cap, row_off):
    base = (pl.program_id(0) * pl.num_programs(2) + pl.program_id(2)) * cap

    def body(s, carry):
        t = idx_ref[base + s] + row_off
        rows_sc[pl.ds(s, 1), :] = h_ref[0, pl.ds(t, 1), :]
        return carry

    lax.fori_loop(0, cap, body, 0, unroll=8)
    o_ref[0, 0] = rows_sc[...].astype(BF16)


def _gather(idx_flat, h2, cap, rows_block, row_off):
    b, _, d = h2.shape
    dh = d // 2
    return pl.pallas_call(
        functools.partial(_gather_kernel, cap=cap, row_off=row_off),
        grid_spec=pltpu.PrefetchScalarGridSpec(
            num_scalar_prefetch=1,
            grid=(b, 2, N_EXPERTS),
            in_specs=[pl.BlockSpec((1, rows_block, dh), lambda bi, c, e, idx: (bi, 0, c))],
            out_specs=pl.BlockSpec((1, 1, cap, dh), lambda bi, c, e, idx: (e, bi, 0, c)),
            scratch_shapes=[pltpu.VMEM((cap, dh), F32)],
        ),
        out_shape=jax.ShapeDtypeStruct((N_EXPERTS, b, cap, d), BF16),
        compiler_params=_cparams(("arbitrary", "arbitrary", "arbitrary")),
        name="moe_gather",
    )(idx_flat, h2)


def _ffn_kernel(*refs, n_streams):
    xg_refs = refs[0:n_streams]
    val_refs = refs[n_streams:2 * n_streams]
    w1_ref, w3_ref, w2_ref = refs[2 * n_streams:2 * n_streams + 3]
    y_refs = refs[2 * n_streams + 3:]
    f = pl.program_id(1)
    w1 = w1_ref[0]
    w3 = w3_ref[0]
    w2 = w2_ref[0]
    for xg_ref, val_ref, y_ref in zip(xg_refs, val_refs, y_refs):
        xg = xg_ref[0].reshape(-1, w1.shape[0])
        y = _dot((_silu(_dot(xg, w1)) * _dot(xg, w3)).astype(BF16), w2)

        @pl.when(f == 0)
        def _():
            y_ref[0] = y

        @pl.when(f > 0)
        def _():
            y_ref[0] = y_ref[0] + y

        @pl.when(f == pl.num_programs(1) - 1)
        def _():
            y_ref[0] = y_ref[0] * val_ref[0]


def _ffn(xgs, vals, w1, w3, w2, tf):
    ne, d, ff = w1.shape
    n_streams = len(xgs)
    in_specs, out_specs, out_shapes = [], [], []
    for xg in xgs:
        in_specs.append(pl.BlockSpec((1,) + xg.shape[1:], lambda e, f: (e, 0, 0, 0)))
    for v in vals:
        in_specs.append(pl.BlockSpec((1,) + v.shape[1:], lambda e, f: (e, 0, 0)))
    in_specs += [pl.BlockSpec((1, d, tf), lambda e, f: (e, 0, f)),
                 pl.BlockSpec((1, d, tf), lambda e, f: (e, 0, f)),
                 pl.BlockSpec((1, tf, d), lambda e, f: (e, f, 0))]
    for xg in xgs:
        m = xg.shape[1] * xg.shape[2]
        out_specs.append(pl.BlockSpec((1, m, d), lambda e, f: (e, 0, 0)))
        out_shapes.append(jax.ShapeDtypeStruct((ne, m, d), F32))
    return pl.pallas_call(
        functools.partial(_ffn_kernel, n_streams=n_streams),
        grid=(ne, ff // tf),
        in_specs=in_specs,
        out_specs=tuple(out_specs),
        out_shape=tuple(out_shapes),
        compiler_params=_cparams(("parallel", "arbitrary")),
        name="moe_ffn",
    )(*xgs, *vals, w1, w3, w2)


def _combine_kernel(idx_ref, y_ref, o_ref, *, cap):
    e = pl.program_id(2)

    @pl.when(e == 0)
    def _():
        o_ref[...] = jnp.zeros(o_ref.shape, F32)

    base = (pl.program_id(0) * pl.num_programs(2) + e) * cap

    def body(s, carry):
        t = idx_ref[base + s]
        o_ref[0, pl.ds(t, 1), :] = o_ref[0, pl.ds(t, 1), :] + y_ref[0, pl.ds(s, 1), :]
        return carry

    lax.fori_loop(0, cap, body, 0, unroll=8)


def _combine(idx_flat, y, b, n_tok, cap):
    ne, _, d = y.shape
    dh = d // 2
    return pl.pallas_call(
        functools.partial(_combine_kernel, cap=cap),
        grid_spec=pltpu.PrefetchScalarGridSpec(
            num_scalar_prefetch=1,
            grid=(b, 2, ne),
            in_specs=[pl.BlockSpec((1, cap, dh), lambda bi, c, e, idx: (e, bi, c))],
            out_specs=pl.BlockSpec((1, n_tok, dh), lambda bi, c, e, idx: (bi, 0, c)),
        ),
        out_shape=jax.ShapeDtypeStruct((b, n_tok, d), F32),
        compiler_params=_cparams(("arbitrary", "arbitrary", "arbitrary")),
        name="moe_combine",
    )(idx_flat, y)


def _residual_kernel(x1_ref, mod_ref, mx_ref, mc_ref, o_ref, *, d_model, blk0):
    i = pl.program_id(1) + blk0
    moe = jnp.where(i == 0, mc_ref[0], mx_ref[0])
    o_ref[0] = x1_ref[0] + mod_ref[:, 5 * d_model:6 * d_model] * moe


def _residual(x1, modsel, moe_x, moe_c, tm, blk0):
    b, t, d = x1.shape
    nb = t // tm - blk0
    return pl.pallas_call(
        functools.partial(_residual_kernel, d_model=d, blk0=blk0),
        grid=(b, nb),
        in_specs=[pl.BlockSpec((1, tm, d), lambda bi, i: (bi, i + blk0, 0)),
                  pl.BlockSpec((None, None, 1, N_MOD * d), lambda bi, i: (bi, jnp.minimum(i + blk0, 1), 0, 0)),
                  pl.BlockSpec((1, tm, d), lambda bi, i: (bi, jnp.maximum(i + blk0 - 1, 0), 0)),
                  pl.BlockSpec((1, tm, d), lambda bi, i: (bi, 0, 0))],
        out_specs=pl.BlockSpec((1, tm, d), lambda bi, i: (bi, i, 0)),
        out_shape=jax.ShapeDtypeStruct((b, nb * tm, d), F32),
        compiler_params=_cparams(("parallel", "parallel")),
        name="moe_residual",
    )(x1, modsel, moe_x, moe_c)


def _rope_tables(ctx_len, seq):
    n = jnp.arange(seq)
    pos = jnp.stack([n // GRID_W, n % GRID_W], axis=-1).astype(F32)
    lane = jnp.arange(LANES) % HEAD_DIM
    axis = lane // 32
    n_freq = HEAD_DIM // 4
    inv = ROPE_THETA ** (-(lane % n_freq).astype(F32) / n_freq)
    ang = pos[:, axis] * inv
    sign = jnp.where((lane % 32) < 16, -1.0, 1.0)
    cos_t = jnp.concatenate([jnp.ones((ctx_len, LANES), F32), jnp.cos(ang)], axis=0)
    sin_t = jnp.concatenate([jnp.zeros((ctx_len, LANES), F32), jnp.sin(ang) * sign], axis=0)
    return cos_t, sin_t


def _reorder_w_in(w):
    d = w.shape[0]
    o = ATTN_W + 2 * KV_W
    attn = w[:, :o]
    ml = w[:, o:o + 4 * MLSTM_W]
    o += 4 * MLSTM_W
    mgate = w[:, o:o + 2 * N_DIR * MLSTM_HEADS]
    o += 2 * N_DIR * MLSTM_HEADS
    gd = w[:, o:o + 4 * GDN_W]
    o += 4 * GDN_W
    ggate = w[:, o:o + 2 * N_DIR * GDN_HEADS]
    pad = jnp.zeros((d, GATE_W - mgate.shape[1] - ggate.shape[1]), w.dtype)
    return jnp.concatenate([attn, ml, gd, mgate, ggate, pad], axis=1)


def _chunk_transpose(a):
    b, h, t, e = a.shape
    return jnp.swapaxes(a.reshape(b, h, t // CHUNK, CHUNK, e), -1, -2)


def _gate_row(vals, col0):
    flat = vals.reshape(-1).astype(F32)
    return jnp.zeros((1, GATE_W), F32).at[0, col0:col0 + flat.shape[0]].set(flat)


def _gate_col(first, second):
    flat = jnp.concatenate([first.reshape(-1), second.reshape(-1)]).astype(F32)
    return jnp.broadcast_to(flat[:, None], (flat.shape[0], CHUNK))


def kernel(x, c, ctx, c_ctx, mod_w, mod_b, norm1_g, w_in, q_norm_g, k_norm_g, mlstm_i_bias, mlstm_f_bias,
           mlstm_out_g, gdn_conv_w, gdn_a_log, gdn_dt_bias, gdn_out_g, w_out, norm2_g, router_w, w1, w3, w2):
    b, seq, d = x.shape
    ctx_len = ctx.shape[1]
    depth = mod_w.shape[0]
    tm = ctx_len
    t = ctx_len + seq
    ne = N_EXPERTS
    cap_x = CAPACITY_FACTOR * seq // ne
    cap_c = CAPACITY_FACTOR * ctx_len // ne
    rows_c = 2 * SUBLANES

    cvec = jnp.concatenate([c, c_ctx[None, :], jnp.zeros((SUBLANES - b - 1, d), F32)], axis=0)
    mod = _modulation(cvec, mod_w, mod_b)
    cos_t, sin_t = _rope_tables(ctx_len, seq)
    bd = jnp.kron(jnp.eye(LANES // HEAD_DIM, dtype=F32), jnp.full((HEAD_DIM, HEAD_DIM), 1.0 / HEAD_DIM, F32))
    xa = jnp.concatenate([ctx, x], axis=1)
    zeros8 = jnp.zeros((N_DIR, MLSTM_HEADS), F32)

    for l in range(depth):
        need_ctx = l < depth - 1
        blk0 = 0 if need_ctx else 1
        modsel = jnp.stack([jnp.broadcast_to(mod[l, b], (b, N_MOD * d)), mod[l, :b]], axis=1)[:, :, None, :]
        w_r = _reorder_w_in(w_in[l]).astype(BF16)
        qg = jnp.tile(q_norm_g[l], LANES // HEAD_DIM)[None, :]
        kg = jnp.tile(k_norm_g[l], LANES // HEAD_DIM)[None, :]
        q, k, v, mqkv, mo, gqkv, gz, gates = _inproj(xa, modsel, norm1_g[l][None, :], w_r, cos_t, sin_t, qg, kg, bd, tm)

        ao = _attention(q, k, jnp.swapaxes(v, 2, 3), ctx_len, tm, blk0)

        brow = _gate_row(mlstm_i_bias[l], GC_MI) + _gate_row(mlstm_f_bias[l], GC_MF)
        bcol = _gate_col(mlstm_i_bias[l], mlstm_f_bias[l])
        gates_t = _chunk_transpose(gates[:, None, :, :GATE_ROWS])[:, 0]
        mhf, mhr = _mlstm_scan(mqkv, _chunk_transpose(mqkv[:, 4:8]), gates, gates_t, brow, bcol, ctx_len)

        conv12 = gdn_conv_w[l].reshape(CONV_K, 3 * GDN_HEADS, HEAD_DIM).transpose(1, 0, 2)
        gp = _gdn_prep(gqkv, conv12, ctx_len)
        neg_a = -jnp.exp(gdn_a_log[l].astype(F32))
        vk = jnp.concatenate([gp[:, 8:12], gp[:, 4:8]], axis=-1)
        ghf, ghr = _gdn_scan(gp[:, 0:8], vk, _chunk_transpose(gp[:, 4:8]), gates, gates_t,
                             _gate_row(neg_a, GC_GA), _gate_row(gdn_dt_bias[l], GC_GA),
                             _gate_col(neg_a, zeros8), _gate_col(gdn_dt_bias[l], zeros8), ctx_len)

        w_hm = w_out[l].astype(BF16).reshape(ATTN_HEADS + MLSTM_HEADS + GDN_HEADS, HEAD_DIM, d)
        x1, hp, aff = _outproj(xa, modsel, ao, mhf, mhr, mo, mlstm_out_g[l], ghf, ghr, gz, gdn_out_g[l][None, :],
                               w_hm, norm2_g[l][None, :], router_w[l].T, tm, blk0)

        w1b, w3b, w2b = w1[l].astype(BF16), w3[l].astype(BF16), w2[l].astype(BF16)
        idx_x, val_x = _route(aff[:, :, ctx_len:].reshape(b, ne, seq // LANES, LANES), cap_x)
        idx_xf = idx_x.reshape(-1)
        xgs = [_gather(idx_xf, hp, cap_x, t, ctx_len)]
        vals = [val_x.transpose(1, 0, 2, 3).reshape(ne, b * cap_x, 1)]
        if need_ctx:
            aff_c = jnp.pad(aff[:, :, :ctx_len], ((0, 0), (0, 0), (0, rows_c * LANES - ctx_len)), constant_values=-1.0)
            idx_c, val_c = _route(aff_c.reshape(b, ne, rows_c, LANES), cap_c)
            idx_cf = idx_c.reshape(-1)
            xgs.append(_gather(idx_cf, hp, cap_c, ctx_len, 0))
            vals.append(val_c.transpose(1, 0, 2, 3).reshape(ne, b * cap_c, 1))
        ys = _ffn(xgs, vals, w1b, w3b, w2b, min(FFN_TF, w1b.shape[2]))
        moe_x = _combine(idx_xf, ys[0], b, seq, cap_x)
        moe_c = _combine(idx_cf, ys[1], b, ctx_len, cap_c) if need_ctx else moe_x
        xa = _residual(x1, modsel, moe_x, moe_c, tm, blk0)
    return xa
```

```python
import functools
import math

import jax
import jax.numpy as jnp
from jax import lax
from jax.experimental import pallas as pl
from jax.experimental.pallas import tpu as pltpu

F32 = jnp.float32
BF16 = jnp.bfloat16
I32 = jnp.int32
U32 = jnp.uint32
HIGHEST = lax.Precision.HIGHEST

HEAD_DIM = 64
ATTN_HEADS = 8
ATTN_KV_HEADS = 2
ATTN_REP = ATTN_HEADS // ATTN_KV_HEADS
MLSTM_HEADS = 4
GDN_HEADS = 4
N_DIR = 2
CHUNK = 64
CONV_K = 5
GRID_W = 64
ROPE_THETA = 10000.0
N_EXPERTS = 16
CAPACITY_FACTOR = 2
N_MOD = 6
EPS = 1e-6
ATTN_W = ATTN_HEADS * HEAD_DIM
KV_W = ATTN_KV_HEADS * HEAD_DIM
MLSTM_W = MLSTM_HEADS * HEAD_DIM
GDN_W = GDN_HEADS * HEAD_DIM
LANES = 128
SUBLANES = 8
GATE_W = LANES
GATE_ROWS = 32
GC_MI, GC_MF, GC_GA, GC_GB = 0, 8, 16, 24
VMEM_LIMIT = 56 * 1024 * 1024
ATTN_COLS = 256
ATTN_VPAD = 16
LOG2E = 1.4426950408889634
FFN_TF = 256
CONV_ROWS = 256
PAD_ROWS = 8


def _cparams(sem):
    return pltpu.CompilerParams(dimension_semantics=sem, vmem_limit_bytes=VMEM_LIMIT)


def _sigmoid(x):
    return 1.0 / (1.0 + jnp.exp(-x))


def _silu(x):
    return x * _sigmoid(x)


def _log_sigmoid(x):
    return jnp.minimum(x, 0.0) - jnp.log1p(jnp.exp(-jnp.abs(x)))


def _softplus(x):
    return jnp.maximum(x, 0.0) + jnp.log1p(jnp.exp(-jnp.abs(x)))


def _dot(a, b, precision=None):
    return jnp.dot(a, b, preferred_element_type=F32, precision=precision)


def _dot_nt(a, b, precision=None):
    return lax.dot_general(a, b, (((1,), (1,)), ((), ())), preferred_element_type=F32, precision=precision)


def _split_bf16(a):
    hi = a.astype(BF16)
    lo = (a - hi.astype(F32)).astype(BF16)
    return hi, lo


def _dot3(a, b):
    ah, al = _split_bf16(a)
    bh, bl = _split_bf16(b)
    return _dot(ah, bh) + (_dot(ah, bl) + _dot(al, bh))


def _iota(shape, dim):
    return lax.broadcasted_iota(I32, shape, dim)


def _eye_rows(rows, cols, first):
    return (_iota((rows, cols), 0) + first == _iota((rows, cols), 1)).astype(F32)


def _mod_kernel(c_ref, w_ref, b_ref, o_ref):
    s = _silu(c_ref[...])
    o_ref[0] = _dot(s, w_ref[0], precision=HIGHEST) + b_ref[0]


def _modulation(cvec, mod_w, mod_b):
    depth, d, n = mod_w.shape
    rows = cvec.shape[0]
    tn = d
    return pl.pallas_call(
        _mod_kernel,
        grid=(depth, n // tn),
        in_specs=[
            pl.BlockSpec((rows, d), lambda l, j: (0, 0)),
            pl.BlockSpec((1, d, tn), lambda l, j: (l, 0, j)),
            pl.BlockSpec((1, 1, tn), lambda l, j: (l, 0, j)),
        ],
        out_specs=pl.BlockSpec((1, rows, tn), lambda l, j: (l, 0, j)),
        out_shape=jax.ShapeDtypeStruct((depth, rows, n), F32),
        compiler_params=_cparams(("parallel", "parallel")),
        name="modulation",
    )(cvec, mod_w, mod_b.reshape(depth, 1, n))


def _inproj_kernel(x_ref, mod_ref, g_ref, w_ref, wt_ref, cos_ref, sin_ref, qg_ref, kg_ref, bd_ref,
                   q_ref, k_ref, vt_ref, ml_ref, mkt_ref, mo_ref, gd_ref, gz_ref, gate_ref, gatet_ref, *, d_model):
    x = x_ref[0]
    tm = x.shape[0]
    sh = mod_ref[:, 0:d_model]
    sc = mod_ref[:, d_model:2 * d_model]
    xn = x * lax.rsqrt(jnp.mean(x * x, axis=-1, keepdims=True) + EPS) * g_ref[...]
    h = (xn * (1.0 + sc) + sh).astype(BF16)
    p = _dot(h, w_ref[...])
    pt = _dot_nt(wt_ref[...], h)
    for g in range(ATTN_KV_HEADS):
        vt_ref[0, g, 0:HEAD_DIM, :] = pt[g * HEAD_DIM:(g + 1) * HEAD_DIM].astype(BF16)
        vt_ref[0, g, HEAD_DIM:, :] = (_iota((ATTN_VPAD, tm), 0) == 0).astype(BF16)
    for c in range(tm // CHUNK):
        cs = slice(c * CHUNK, (c + 1) * CHUNK)
        for j in range(MLSTM_HEADS):
            mkt_ref[0, j, c] = pt[KV_W + j * HEAD_DIM:KV_W + (j + 1) * HEAD_DIM, cs] * HEAD_DIM ** -0.5
        gatet_ref[0, c] = pt[KV_W + MLSTM_W:KV_W + MLSTM_W + GATE_ROWS, cs]

    cos = cos_ref[...]
    sin = sin_ref[...]
    first_half = (_iota(cos.shape, 1) % 32) < 16

    def norm_rope(xs, g, scale):
        ms = _dot(xs * xs, bd_ref[...], precision=HIGHEST)
        xn_ = xs * lax.rsqrt(ms + EPS) * g
        sw = jnp.where(first_half, pltpu.roll(xn_, LANES - 16, 1), pltpu.roll(xn_, 16, 1))
        return (xn_ * cos + sw * sin) * scale

    for j in range(ATTN_W // LANES):
        qs = norm_rope(p[:, j * LANES:(j + 1) * LANES], qg_ref[...], LOG2E * HEAD_DIM ** -0.5).astype(BF16)
        q_ref[0, 2 * j] = qs[:, 0:HEAD_DIM]
        q_ref[0, 2 * j + 1] = qs[:, HEAD_DIM:LANES]
    ks = norm_rope(p[:, ATTN_W:ATTN_W + KV_W], kg_ref[...], 1.0).astype(BF16)
    k_ref[0, 0] = ks[:, 0:HEAD_DIM]
    k_ref[0, 1] = ks[:, HEAD_DIM:LANES]

    off = ATTN_W + KV_W

    def head(j):
        return p[:, off + j * HEAD_DIM: off + (j + 1) * HEAD_DIM]

    for j in range(12):
        ml_ref[0, j] = head(j) * HEAD_DIM ** -0.5 if 4 <= j < 8 else head(j)
    for j in range(4):
        mo_ref[0, j] = head(12 + j)
    off += 4 * MLSTM_W
    for j in range(12):
        gd_ref[0, j] = head(j)
    for j in range(4):
        gz_ref[0, j] = head(12 + j)
    off += 4 * GDN_W
    gate_ref[0] = p[:, off:off + GATE_W]


def _inproj(xa, modsel, g1, w_r, w_t, cos_t, sin_t, qg, kg, bd, tm):
    b, t, d = xa.shape
    nb = t // tm
    nc = tm // CHUNK
    kern = functools.partial(_inproj_kernel, d_model=d)
    hm_shape = lambda nh, dt: jax.ShapeDtypeStruct((b, nh, t, HEAD_DIM), dt)
    out_shapes = (hm_shape(ATTN_HEADS, BF16), hm_shape(ATTN_KV_HEADS, BF16),
                  jax.ShapeDtypeStruct((b, ATTN_KV_HEADS, HEAD_DIM + ATTN_VPAD, t), BF16),
                  hm_shape(12, F32), jax.ShapeDtypeStruct((b, MLSTM_HEADS, t // CHUNK, HEAD_DIM, CHUNK), F32),
                  hm_shape(4, F32), hm_shape(12, F32), hm_shape(4, F32),
                  jax.ShapeDtypeStruct((b, t, GATE_W), F32),
                  jax.ShapeDtypeStruct((b, t // CHUNK, GATE_ROWS, CHUNK), F32))
    hm = lambda nh: pl.BlockSpec((1, nh, tm, HEAD_DIM), lambda bi, i: (bi, 0, i, 0))
    const = lambda a: pl.BlockSpec(a.shape, lambda bi, i: (0,) * a.ndim)
    return pl.pallas_call(
        kern,
        grid=(b, nb),
        in_specs=[
            pl.BlockSpec((1, tm, d), lambda bi, i: (bi, i, 0)),
            pl.BlockSpec((None, None, 1, N_MOD * d), lambda bi, i: (bi, jnp.minimum(i, 1), 0, 0)),
            const(g1), const(w_r), const(w_t),
            pl.BlockSpec((tm, LANES), lambda bi, i: (i, 0)),
            pl.BlockSpec((tm, LANES), lambda bi, i: (i, 0)),
            const(qg), const(kg), const(bd),
        ],
        out_specs=(hm(ATTN_HEADS), hm(ATTN_KV_HEADS),
                   pl.BlockSpec((1, ATTN_KV_HEADS, HEAD_DIM + ATTN_VPAD, tm), lambda bi, i: (bi, 0, 0, i)),
                   hm(12), pl.BlockSpec((1, MLSTM_HEADS, nc, HEAD_DIM, CHUNK), lambda bi, i: (bi, 0, i, 0, 0)),
                   hm(4), hm(12), hm(4),
                   pl.BlockSpec((1, tm, GATE_W), lambda bi, i: (bi, i, 0)),
                   pl.BlockSpec((1, nc, GATE_ROWS, CHUNK), lambda bi, i: (bi, i, 0, 0))),
        out_shape=out_shapes,
        compiler_params=_cparams(("parallel", "parallel")),
        name="inproj",
    )(xa, modsel, g1, w_r, w_t, cos_t, sin_t, qg, kg, bd)


def _attn_kernel(q_ref, k_ref, vt_ref, o_ref, sa_sc, sb_sc, xa_sc, xb_sc, m_sc, acc_sc, *, tq, tk, n_pairs, blk0):
    i = pl.program_id(2) + blk0
    q = q_ref[0].reshape(ATTN_REP * tq, HEAD_DIM)
    cols = [slice(c * ATTN_COLS, (c + 1) * ATTN_COLS) for c in range(ATTN_REP * tq // ATTN_COLS)]

    m_sc[...] = jnp.full(m_sc.shape, -jnp.inf, F32)
    acc_sc[...] = jnp.zeros(acc_sc.shape, F32)

    def score(s_ref, x_ref, tile):
        start = pl.multiple_of(tile * tk, tk)
        k = k_ref[0, 0, pl.ds(start, tk), :]
        for cs in cols:
            s = _dot_nt(k, q[cs])
            s_ref[:, cs] = s
            x_ref[:, cs] = jnp.max(s, axis=0, keepdims=True)

    def consume(s_ref, x_ref, tile):
        start = pl.multiple_of(tile * tk, tk)
        vt = vt_ref[0, 0, :, pl.ds(start, tk)]
        m_old = [m_sc[:, cs] for cs in cols]
        acc_old = [acc_sc[:, cs] for cs in cols]
        out = []
        for cs, mo, ao in zip(cols, m_old, acc_old):
            mn = jnp.maximum(mo, x_ref[:, cs])
            p = jnp.exp2(s_ref[:, cs] - mn)
            out.append((mn, jnp.exp2(mo - mn) * ao + _dot(vt, p.astype(BF16))))
        for cs, (mn, an) in zip(cols, out):
            m_sc[:, cs] = mn
            acc_sc[:, cs] = an

    score(sa_sc, xa_sc, 0)

    def body(j, carry):
        score(sb_sc, xb_sc, 2 * j + 1)
        consume(sa_sc, xa_sc, 2 * j)
        score(sa_sc, xa_sc, 2 * j + 2)
        consume(sb_sc, xb_sc, 2 * j + 1)
        return carry

    n = jnp.where(i == 0, 0, n_pairs)
    lax.fori_loop(0, n, body, 0)
    consume(sa_sc, xa_sc, 2 * n)
    o_t = (acc_sc[0:HEAD_DIM, :] / acc_sc[HEAD_DIM:HEAD_DIM + 1, :]).astype(BF16)
    eye = (_iota((tq, tq), 0) == _iota((tq, tq), 1)).astype(BF16)
    for r in range(ATTN_REP):
        o_ref[0, r] = _dot_nt(eye, o_t[:, r * tq:(r + 1) * tq]).astype(o_ref.dtype)


def _attention(q, k, vt, ctx_len, tq, blk0):
    b, _, t, _ = q.shape
    tk = ctx_len
    assert (t // tk) % 2 == 1
    nq = t // tq - blk0
    kern = functools.partial(_attn_kernel, tq=tq, tk=tk, n_pairs=(t // tk - 1) // 2, blk0=blk0)
    rows = ATTN_REP * tq
    return pl.pallas_call(
        kern,
        grid=(b, ATTN_KV_HEADS, nq),
        in_specs=[
            pl.BlockSpec((1, ATTN_REP, tq, HEAD_DIM), lambda bi, g, i: (bi, g, i + blk0, 0)),
            pl.BlockSpec((1, 1, t, HEAD_DIM), lambda bi, g, i: (bi, g, 0, 0)),
            pl.BlockSpec((1, 1, HEAD_DIM + ATTN_VPAD, t), lambda bi, g, i: (bi, g, 0, 0)),
        ],
        out_specs=pl.BlockSpec((1, ATTN_REP, tq, HEAD_DIM), lambda bi, g, i: (bi, g, i + blk0, 0)),
        out_shape=jax.ShapeDtypeStruct((b, ATTN_HEADS, t, HEAD_DIM), BF16),
        scratch_shapes=[pltpu.VMEM((tk, rows), F32), pltpu.VMEM((tk, rows), F32),
                        pltpu.VMEM((1, rows), F32), pltpu.VMEM((1, rows), F32),
                        pltpu.VMEM((1, rows), F32), pltpu.VMEM((HEAD_DIM + ATTN_VPAD, rows), F32)],
        compiler_params=_cparams(("parallel", "parallel", "arbitrary")),
        name="attention",
    )(q, k, vt)


def _chunk_maps(nc_c, nc_x):
    fwd = lambda j: j
    rev = lambda j: jnp.where(j < nc_c, nc_c - 1 - j, 2 * nc_c + nc_x - 1 - j)
    return fwd, rev


def _dir_masks(d):
    r = _iota((CHUNK, CHUNK), 0)
    c = _iota((CHUNK, CHUNK), 1)
    incl = (r >= c) if d == 0 else (r <= c)
    strict = (r > c) if d == 0 else (r < c)
    incl_t = (r <= c) if d == 0 else (r >= c)
    return incl, strict, incl.astype(F32), incl_t.astype(F32)


def _mlstm_kernel(xf_ref, xr_ref, ktf_ref, ktr_ref, gf_ref, gr_ref, gtf_ref, gtr_ref, brow_ref, bcol_ref,
                  hf_ref, hr_ref, c_sc, n_sc, m_sc, *, nb):
    @pl.when(pl.program_id(0) == 0)
    def _():
        c_sc[...] = jnp.zeros(c_sc.shape, F32)
        n_sc[...] = jnp.zeros(n_sc.shape, F32)
        m_sc[...] = jnp.zeros(m_sc.shape, F32)

    refs = ((xf_ref, ktf_ref, gf_ref, gtf_ref, hf_ref), (xr_ref, ktr_ref, gr_ref, gtr_ref, hr_ref))
    masks = [_dir_masks(d) for d in range(N_DIR)]
    gate = {}
    for d in range(N_DIR):
        _, _, g_ref, gt_ref, _ = refs[d]
        _, _, tri, tri_t = masks[d]
        for b in range(nb):
            cum = _dot(tri, _log_sigmoid(g_ref[b] + brow_ref[...]), precision=HIGHEST)
            g_t = gt_ref[b, 0, GC_MI:GC_MI + 16, :] + bcol_ref[...]
            lf_t = _log_sigmoid(g_t)
            gate[d, b] = (cum, g_t, lf_t, _dot(lf_t, tri_t, precision=HIGHEST))

    probs = [(d, b, h) for d in range(N_DIR) for b in range(nb) for h in range(MLSTM_HEADS)]
    st = []
    for d, b, h in probs:
        p = (d * nb + b) * MLSTM_HEADS + h
        x_ref, kt_ref = refs[d][0], refs[d][1]
        cum, g_t, lf_t, cum_t = gate[d, b]
        ri = d * MLSTM_HEADS + h
        rf = GC_MF + ri
        bcum_col = cum[:, rf:rf + 1]
        bcum_row = cum_t[rf - GC_MI:rf - GC_MI + 1, :]
        i_row = g_t[ri:ri + 1, :]
        b_last = jnp.sum(lf_t[rf - GC_MI:rf - GC_MI + 1, :], axis=-1, keepdims=True)
        m_old = m_sc[p, 0:1, 0:1]
        w_end = b_last - bcum_row + i_row
        m_new = jnp.maximum(b_last + m_old, jnp.max(w_end, axis=-1, keepdims=True))
        a_row = jnp.exp(w_end - m_new)
        dec = jnp.exp(b_last + m_old - m_new)
        dmat = jnp.where(masks[d][0], bcum_col - bcum_row + i_row, -jnp.inf)
        inter = bcum_col + m_old
        m_t = jnp.maximum(inter, jnp.max(dmat, axis=-1, keepdims=True))
        st.append(dict(p=p, q=x_ref[b, h], k=x_ref[b, MLSTM_HEADS + h], v=x_ref[b, 2 * MLSTM_HEADS + h],
                       k_t=kt_ref[b, h, 0], c_old=c_sc[p], n_old=n_sc[p], a_row=a_row, dec=dec, m_new=m_new,
                       m_t=m_t, w_in=jnp.exp(inter - m_t), dexp=jnp.exp(dmat - m_t), out=refs[d][4], b=b, h=h))
    for e in st:
        e["s"] = _dot(e["q"], e["k_t"]) * e["dexp"]
    for e in st:
        e["qc"] = _dot(e["q"], e["c_old"])
    for e in st:
        e["sv"] = _dot(e["s"], e["v"])
    for e in st:
        e["kv"] = _dot(e["k_t"] * e["a_row"], e["v"])
    for e in st:
        e["ak"] = _dot(jnp.broadcast_to(e["a_row"], (SUBLANES, CHUNK)), e["k"])
    for e in st:
        num = e["w_in"] * e["qc"] + e["sv"]
        den = (e["w_in"] * jnp.sum(e["q"] * e["n_old"][0:1, :], axis=-1, keepdims=True)
               + jnp.sum(e["s"], axis=-1, keepdims=True))
        e["out"][e["b"], e["h"]] = num / jnp.maximum(jnp.abs(den), jnp.exp(-e["m_t"]))
    for e in st:
        p = e["p"]
        c_sc[p] = e["dec"] * e["c_old"] + e["kv"]
        n_sc[p] = e["dec"] * e["n_old"] + e["ak"]
        m_sc[p] = jnp.broadcast_to(e["m_new"], (SUBLANES, LANES))


def _mlstm_scan(mqkv, k_t, gates, gates_t, brow, bcol, ctx_len):
    b, _, t, _ = mqkv.shape
    nc_c, nc_x = ctx_len // CHUNK, (t - ctx_len) // CHUNK
    fwd, rev = _chunk_maps(nc_c, nc_x)
    nprob = N_DIR * b * MLSTM_HEADS
    xs = lambda f: pl.BlockSpec((b, 12, CHUNK, HEAD_DIM), lambda j: (0, 0, f(j), 0))
    ks = lambda f: pl.BlockSpec((b, MLSTM_HEADS, 1, HEAD_DIM, CHUNK), lambda j: (0, 0, f(j), 0, 0))
    gs = lambda f: pl.BlockSpec((b, CHUNK, GATE_W), lambda j: (0, f(j), 0))
    gts = lambda f: pl.BlockSpec((b, 1, GATE_ROWS, CHUNK), lambda j: (0, f(j), 0, 0))
    hs = lambda f: pl.BlockSpec((b, MLSTM_HEADS, CHUNK, HEAD_DIM), lambda j: (0, 0, f(j), 0))
    out = jax.ShapeDtypeStruct((b, MLSTM_HEADS, t, HEAD_DIM), F32)
    return pl.pallas_call(
        functools.partial(_mlstm_kernel, nb=b),
        grid=(nc_c + nc_x,),
        in_specs=[xs(fwd), xs(rev), ks(fwd), ks(rev), gs(fwd), gs(rev), gts(fwd), gts(rev),
                  pl.BlockSpec((1, GATE_W), lambda j: (0, 0)),
                  pl.BlockSpec((16, CHUNK), lambda j: (0, 0))],
        out_specs=(hs(fwd), hs(rev)),
        out_shape=(out, out),
        scratch_shapes=[pltpu.VMEM((nprob, HEAD_DIM, HEAD_DIM), F32),
                        pltpu.VMEM((nprob, SUBLANES, HEAD_DIM), F32),
                        pltpu.VMEM((nprob, SUBLANES, LANES), F32)],
        compiler_params=_cparams(("arbitrary",)),
        name="mlstm_scan",
    )(mqkv, mqkv, k_t, k_t, gates, gates, gates_t, gates_t, brow, bcol)


def _gdn_prep_kernel(x_ref, w_ref, o_ref, pad_sc, *, ctx_len, t):
    part = pl.program_id(1) // GDN_HEADS
    zeros = jnp.zeros((PAD_ROWS, HEAD_DIM), F32)
    pad_sc[0:PAD_ROWS] = zeros
    pad_sc[PAD_ROWS:PAD_ROWS + ctx_len] = x_ref[0, 0, 0:ctx_len]
    pad_sc[PAD_ROWS + ctx_len:2 * PAD_ROWS + ctx_len] = zeros
    pad_sc[2 * PAD_ROWS + ctx_len:2 * PAD_ROWS + t] = x_ref[0, 0, ctx_len:t]
    pad_sc[2 * PAD_ROWS + t:3 * PAD_ROWS + t] = zeros
    w = w_ref[0]
    is_qk = part < 2
    scale = jnp.where(part == 0, HEAD_DIM ** -0.5, 1.0)
    for c in range(t // CONV_ROWS):
        r0 = c * CONV_ROWS
        base = r0 + (PAD_ROWS if r0 < ctx_len else 2 * PAD_ROWS) - CONV_K // 2
        y = w[0:1, :] * pad_sc[base:base + CONV_ROWS]
        for j in range(1, CONV_K):
            y = y + w[j:j + 1, :] * pad_sc[base + j:base + j + CONV_ROWS]
        y = _silu(y)
        yn = y * lax.rsqrt(jnp.sum(y * y, axis=-1, keepdims=True) + EPS) * scale
        o_ref[0, 0, r0:r0 + CONV_ROWS] = jnp.where(is_qk, yn, y)


def _gdn_prep(gqkv, conv_w12, ctx_len):
    b, np_, t, _ = gqkv.shape
    return pl.pallas_call(
        functools.partial(_gdn_prep_kernel, ctx_len=ctx_len, t=t),
        grid=(b, np_),
        in_specs=[pl.BlockSpec((1, 1, t, HEAD_DIM), lambda bi, p: (bi, p, 0, 0)),
                  pl.BlockSpec((1, CONV_K, HEAD_DIM), lambda bi, p: (p, 0, 0))],
        out_specs=pl.BlockSpec((1, 1, t, HEAD_DIM), lambda bi, p: (bi, p, 0, 0)),
        out_shape=jax.ShapeDtypeStruct(gqkv.shape, F32),
        scratch_shapes=[pltpu.VMEM((t + 3 * PAD_ROWS, HEAD_DIM), F32)],
        compiler_params=_cparams(("parallel", "parallel")),
        name="gdn_prep",
    )(gqkv, conv_w12)


def _gdn_kernel(xf_ref, xr_ref, vkf_ref, vkr_ref, ktf_ref, ktr_ref, gf_ref, gr_ref, gtf_ref, gtr_ref,
                arow_ref, drow_ref, acol_ref, dcol_ref, of_ref, or_ref, s_sc, *, nb):
    @pl.when(pl.program_id(0) == 0)
    def _():
        s_sc[...] = jnp.zeros(s_sc.shape, F32)

    refs = ((xf_ref, vkf_ref, ktf_ref, gf_ref, gtf_ref, of_ref), (xr_ref, vkr_ref, ktr_ref, gr_ref, gtr_ref, or_ref))
    masks = [_dir_masks(d) for d in range(N_DIR)]
    gate = {}
    for d in range(N_DIR):
        g_ref, gt_ref = refs[d][3], refs[d][4]
        _, _, tri, tri_t = masks[d]
        for b in range(nb):
            g = g_ref[b]
            gval = arow_ref[...] * _softplus(g + drow_ref[...])
            g_t = gt_ref[b, 0, GC_GA:GC_GA + 16, :]
            gval_t = acol_ref[...] * _softplus(g_t + dcol_ref[...])
            gate[d, b] = (_dot(tri, gval, precision=HIGHEST), _sigmoid(g), gval_t,
                          _dot(gval_t, tri_t, precision=HIGHEST))

    probs = [(d, b, h) for d in range(N_DIR) for b in range(nb) for h in range(GDN_HEADS)]
    second_half = _iota((CHUNK, 2 * HEAD_DIM), 1) >= HEAD_DIM
    st = []
    for d, b, h in probs:
        p = (d * nb + b) * GDN_HEADS + h
        x_ref, vk_ref, kt_ref = refs[d][0], refs[d][1], refs[d][2]
        incl, strict, _, _ = masks[d]
        gcum, beta_all, gval_t, gcum_t = gate[d, b]
        ra = d * GDN_HEADS + h
        g_col = gcum[:, GC_GA + ra:GC_GA + ra + 1]
        g_row = gcum_t[ra:ra + 1, :]
        beta = beta_all[:, GC_GB + ra:GC_GB + ra + 1]
        g_last = jnp.sum(gval_t[ra:ra + 1, :], axis=-1, keepdims=True)
        k_t = kt_ref[b, h, 0]
        eg = jnp.exp(g_col)
        st.append(dict(p=p, b=b, h=h, out=refs[d][5], strict=strict, k_t=k_t, eg_last=jnp.exp(g_last),
                       decay=jnp.where(incl, jnp.exp(jnp.where(incl, g_col - g_row, 0.0)), 0.0),
                       kb=x_ref[b, GDN_HEADS + h] * beta, qg=x_ref[b, h] * eg, q=x_ref[b, h],
                       ktg=k_t * jnp.exp(g_last - g_row), s_old=s_sc[p],
                       sol=vk_ref[b, h] * (beta * jnp.where(second_half, eg, 1.0))))
    for e in st:
        e["pw"] = -jnp.where(e["strict"], _dot3(e["kb"], e["k_t"]) * e["decay"], 0.0)
    n_fac = CHUNK.bit_length() - 1
    for it in range(n_fac):
        for e in st:
            e["sol"] = e["sol"] + _dot3(e["pw"], e["sol"])
        if it < n_fac - 1:
            for e in st:
                e["pw"] = _dot3(e["pw"], e["pw"])
    for e in st:
        e["ws"] = _dot(e["sol"][:, HEAD_DIM:], e["s_old"])
    for e in st:
        e["qs"] = _dot(e["qg"], e["s_old"])
    for e in st:
        e["qk"] = _dot(e["q"], e["k_t"]) * e["decay"]
    for e in st:
        e["v_new"] = e["sol"][:, :HEAD_DIM] - e["ws"]
    for e in st:
        e["out"][e["b"], e["h"]] = e["qs"] + _dot(e["qk"], e["v_new"])
    for e in st:
        s_sc[e["p"]] = e["eg_last"] * e["s_old"] + _dot(e["ktg"], e["v_new"])


def _gdn_scan(gqk, vk, k_t, gates, gates_t, arow, drow, acol, dcol, ctx_len):
    b, _, t, _ = gqk.shape
    nc_c, nc_x = ctx_len // CHUNK, (t - ctx_len) // CHUNK
    fwd, rev = _chunk_maps(nc_c, nc_x)
    nprob = N_DIR * b * GDN_HEADS
    xs = lambda f: pl.BlockSpec((b, 2 * GDN_HEADS, CHUNK, HEAD_DIM), lambda j: (0, 0, f(j), 0))
    vks = lambda f: pl.BlockSpec((b, GDN_HEADS, CHUNK, 2 * HEAD_DIM), lambda j: (0, 0, f(j), 0))
    ks = lambda f: pl.BlockSpec((b, GDN_HEADS, 1, HEAD_DIM, CHUNK), lambda j: (0, 0, f(j), 0, 0))
    gs = lambda f: pl.BlockSpec((b, CHUNK, GATE_W), lambda j: (0, f(j), 0))
    gts = lambda f: pl.BlockSpec((b, 1, GATE_ROWS, CHUNK), lambda j: (0, f(j), 0, 0))
    hs = lambda f: pl.BlockSpec((b, GDN_HEADS, CHUNK, HEAD_DIM), lambda j: (0, 0, f(j), 0))
    row = pl.BlockSpec((1, GATE_W), lambda j: (0, 0))
    col = pl.BlockSpec((16, CHUNK), lambda j: (0, 0))
    out = jax.ShapeDtypeStruct((b, GDN_HEADS, t, HEAD_DIM), F32)
    return pl.pallas_call(
        functools.partial(_gdn_kernel, nb=b),
        grid=(nc_c + nc_x,),
        in_specs=[xs(fwd), xs(rev), vks(fwd), vks(rev), ks(fwd), ks(rev), gs(fwd), gs(rev), gts(fwd), gts(rev),
                  row, row, col, col],
        out_specs=(hs(fwd), hs(rev)),
        out_shape=(out, out),
        scratch_shapes=[pltpu.VMEM((nprob, HEAD_DIM, HEAD_DIM), F32)],
        compiler_params=_cparams(("arbitrary",)),
        name="gdn_scan",
    )(gqk, gqk, vk, vk, k_t, k_t, gates, gates, gates_t, gates_t, arow, drow, acol, dcol)


def _outproj_kernel(x_ref, mod_ref, ao_ref, mhf_ref, mhr_ref, mo_ref, mg_ref, ghf_ref, ghr_ref, gz_ref, gg_ref,
                    w_ref, n2_ref, rw_ref, x1_ref, hp_ref, aff_ref, mix_sc, *, d_model):
    d = d_model

    def put(j, val):
        mix_sc[:, j * HEAD_DIM:(j + 1) * HEAD_DIM] = val.astype(BF16)

    for h in range(ATTN_HEADS):
        put(h, ao_ref[0, h])
    for h in range(MLSTM_HEADS):
        hh = mhf_ref[0, h] + mhr_ref[0, h]
        hn = hh * lax.rsqrt(jnp.mean(hh * hh, axis=-1, keepdims=True) + EPS) * mg_ref[h:h + 1, :]
        put(ATTN_HEADS + h, _sigmoid(mo_ref[0, h]) * hn)
    for h in range(GDN_HEADS):
        oo = ghf_ref[0, h] + ghr_ref[0, h]
        on = oo * lax.rsqrt(jnp.mean(oo * oo, axis=-1, keepdims=True) + EPS) * gg_ref[...]
        put(ATTN_HEADS + MLSTM_HEADS + h, on * _silu(gz_ref[0, h]))
    x1 = x_ref[0] + mod_ref[:, 2 * d:3 * d] * _dot(mix_sc[...], w_ref[...])
    x1_ref[0] = x1
    xn = x1 * lax.rsqrt(jnp.mean(x1 * x1, axis=-1, keepdims=True) + EPS) * n2_ref[...]
    h2 = xn * (1.0 + mod_ref[:, 4 * d:5 * d]) + mod_ref[:, 3 * d:4 * d]
    logits = _dot_nt(rw_ref[...], h2, precision=HIGHEST)
    e = jnp.exp(logits - jnp.max(logits, axis=0, keepdims=True))
    aff_ref[0] = e / jnp.sum(e, axis=0, keepdims=True)
    hp_ref[0] = h2


def _outproj(xa, modsel, ao, mhf, mhr, mo, mg, ghf, ghr, gz, gg, w_hm, n2, rw_t, tm, blk0):
    b, t, d = xa.shape
    nb = t // tm - blk0
    hm = lambda nh: pl.BlockSpec((1, nh, tm, HEAD_DIM), lambda bi, i: (bi, 0, i + blk0, 0))
    full = lambda a: pl.BlockSpec(a.shape, lambda bi, i: (0,) * a.ndim)
    return pl.pallas_call(
        functools.partial(_outproj_kernel, d_model=d),
        grid=(b, nb),
        in_specs=[
            pl.BlockSpec((1, tm, d), lambda bi, i: (bi, i + blk0, 0)),
            pl.BlockSpec((None, None, 1, N_MOD * d), lambda bi, i: (bi, jnp.minimum(i + blk0, 1), 0, 0)),
            hm(ATTN_HEADS), hm(4), hm(4), hm(4), full(mg), hm(4), hm(4), hm(4), full(gg),
            full(w_hm), full(n2), full(rw_t),
        ],
        out_specs=(pl.BlockSpec((1, tm, d), lambda bi, i: (bi, i + blk0, 0)),
                   pl.BlockSpec((1, tm, d), lambda bi, i: (bi, i + blk0, 0)),
                   pl.BlockSpec((1, N_EXPERTS, tm), lambda bi, i: (bi, 0, i + blk0))),
        out_shape=(jax.ShapeDtypeStruct((b, t, d), F32),
                   jax.ShapeDtypeStruct((b, t, d), F32),
                   jax.ShapeDtypeStruct((b, N_EXPERTS, t), F32)),
        scratch_shapes=[pltpu.VMEM((tm, w_hm.shape[0]), BF16)],
        compiler_params=_cparams(("parallel", "parallel")),
        name="outproj",
    )(xa, modsel, ao, mhf, mhr, mo, mg, ghf, ghr, gz, gg, w_hm, n2, rw_t)


def _route_kernel(aff_ref, idx_ref, val_ref, *, cap, rows):
    ne = N_EXPERTS
    a = aff_ref[0]

    def count(mask):
        return jnp.sum(jnp.sum(mask.astype(I32), axis=2, keepdims=True), axis=1, keepdims=True)

    tau_bits = jnp.zeros((ne, 1, 1), I32)
    for bit in range(30, -1, -1):
        cand = tau_bits | (1 << bit)
        keep = count(a >= lax.bitcast_convert_type(cand, F32)) >= cap
        tau_bits = jnp.where(keep, cand, tau_bits)
    tau = lax.bitcast_convert_type(tau_bits, F32)
    gt = a > tau
    eq = a == tau
    need = cap - count(gt)

    triu = (_iota((LANES, LANES), 0) <= _iota((LANES, LANES), 1)).astype(BF16)
    strict_lower = (_iota((rows, rows), 0) > _iota((rows, rows), 1)).astype(BF16)
    triu_r = (_iota((rows, rows), 0) <= _iota((rows, rows), 1)).astype(BF16)
    ones_r = jnp.ones((SUBLANES, LANES), BF16)

    def prefix(mask2d):
        m = mask2d.astype(BF16)
        within = _dot(m, triu)
        tot = jnp.broadcast_to(within[:, LANES - 1:LANES], (rows, LANES)).astype(BF16)
        return within, _dot(strict_lower, tot)

    lane_r = _iota((cap, rows), 1).astype(F32)
    lane_l = _iota((cap, LANES), 1).astype(F32)
    slot = _iota((cap, 1), 0).astype(F32)
    for e in range(ne):
        eq_e = eq[e]
        w_eq, before_eq = prefix(eq_e)
        rank_eq = w_eq - eq_e.astype(F32) + before_eq
        sel = gt[e] | (eq_e & (rank_eq < need[e].astype(F32)))
        rel, _ = prefix(sel)
        sel_b = sel.astype(BF16)
        row_tot = _dot_nt(ones_r, sel_b)
        row_incl = _dot(row_tot.astype(BF16), triu_r)
        row_excl = row_incl - row_tot
        kstar = jnp.sum((row_incl[0:1, :] <= slot).astype(F32), axis=-1, keepdims=True)
        onehot = (lane_r == kstar).astype(F32)
        base = jnp.sum(onehot * row_excl[0:1, :], axis=-1, keepdims=True)
        g_rel = _dot(onehot.astype(BF16), rel.astype(BF16))
        within = jnp.sum((g_rel <= slot - base).astype(F32), axis=-1, keepdims=True)
        g_aff = _dot(onehot, a[e], precision=HIGHEST)
        val_ref[0, e] = jnp.sum(jnp.where(lane_l == within, g_aff, 0.0), axis=-1, keepdims=True)
        idx_ref[0, e] = (kstar * LANES + within).astype(I32)


def _route(aff_tiles, cap):
    b, ne, rows, _ = aff_tiles.shape
    return pl.pallas_call(
        functools.partial(_route_kernel, cap=cap, rows=rows),
        grid=(b,),
        in_specs=[pl.BlockSpec((1, ne, rows, LANES), lambda bi: (bi, 0, 0, 0))],
        out_specs=(pl.BlockSpec((1, ne, cap, 1), lambda bi: (bi, 0, 0, 0)),
                   pl.BlockSpec((1, ne, cap, 1), lambda bi: (bi, 0, 0, 0))),
        out_shape=(jax.ShapeDtypeStruct((b, ne, cap, 1), I32), jax.ShapeDtypeStruct((b, ne, cap, 1), F32)),
        compiler_params=_cparams(("parallel",)),
        name="route",
    )(aff_tiles)


def _gather_kernel(idx_ref, h_ref, o_ref, rows_sc, *, cap, row_off):
    base = (pl.program_id(0) * pl.num_programs(2) + pl.program_id(2)) * cap

    def body(s, carry):
        t = idx_ref[base + s] + row_off
        rows_sc[pl.ds(s, 1), :] = h_ref[0, pl.ds(t, 1), :]
        return carry

    lax.fori_loop(0, cap, body, 0, unroll=8)
    o_ref[0, 0] = rows_sc[...].astype(BF16)


def _gather(idx_flat, h2, cap, rows_block, row_off):
    b, _, d = h2.shape
    dh = d // 2
    return pl.pallas_call(
        functools.partial(_gather_kernel, cap=cap, row_off=row_off),
        grid_spec=pltpu.PrefetchScalarGridSpec(
            num_scalar_prefetch=1,
            grid=(b, 2, N_EXPERTS),
            in_specs=[pl.BlockSpec((1, rows_block, dh), lambda bi, c, e, idx: (bi, 0, c))],
            out_specs=pl.BlockSpec((1, 1, cap, dh), lambda bi, c, e, idx: (e, bi, 0, c)),
            scratch_shapes=[pltpu.VMEM((cap, dh), F32)],
        ),
        out_shape=jax.ShapeDtypeStruct((N_EXPERTS, b, cap, d), BF16),
        compiler_params=_cparams(("arbitrary", "arbitrary", "arbitrary")),
        name="moe_gather",
    )(idx_flat, h2)


def _ffn_kernel(*refs, n_streams):
    xg_refs = refs[0:n_streams]
    val_refs = refs[n_streams:2 * n_streams]
    w1_ref, w3_ref, w2_ref = refs[2 * n_streams:2 * n_streams + 3]
    y_refs = refs[2 * n_streams + 3:]
    f = pl.program_id(1)
    w1 = w1_ref[0].astype(BF16)
    w3 = w3_ref[0].astype(BF16)
    w2 = w2_ref[0].astype(BF16)
    for xg_ref, val_ref, y_ref in zip(xg_refs, val_refs, y_refs):
        xg = xg_ref[0].reshape(-1, w1.shape[0])
        y = _dot((_silu(_dot(xg, w1)) * _dot(xg, w3)).astype(BF16), w2)

        @pl.when(f == 0)
        def _():
            y_ref[0] = y

        @pl.when(f > 0)
        def _():
            y_ref[0] = y_ref[0] + y

        @pl.when(f == pl.num_programs(1) - 1)
        def _():
            y_ref[0] = y_ref[0] * val_ref[0]


def _ffn(xgs, vals, w1, w3, w2, tf):
    ne, d, ff = w1.shape
    n_streams = len(xgs)
    in_specs, out_specs, out_shapes = [], [], []
    for xg in xgs:
        in_specs.append(pl.BlockSpec((1,) + xg.shape[1:], lambda e, f: (e, 0, 0, 0)))
    for v in vals:
        in_specs.append(pl.BlockSpec((1,) + v.shape[1:], lambda e, f: (e, 0, 0)))
    in_specs += [pl.BlockSpec((1, d, tf), lambda e, f: (e, 0, f)),
                 pl.BlockSpec((1, d, tf), lambda e, f: (e, 0, f)),
                 pl.BlockSpec((1, tf, d), lambda e, f: (e, f, 0))]
    for xg in xgs:
        m = xg.shape[1] * xg.shape[2]
        out_specs.append(pl.BlockSpec((1, m, d), lambda e, f: (e, 0, 0)))
        out_shapes.append(jax.ShapeDtypeStruct((ne, m, d), F32))
    return pl.pallas_call(
        functools.partial(_ffn_kernel, n_streams=n_streams),
        grid=(ne, ff // tf),
        in_specs=in_specs,
        out_specs=tuple(out_specs),
        out_shape=tuple(out_shapes),
        compiler_params=_cparams(("parallel", "arbitrary")),
        name="moe_ffn",
    )(*xgs, *vals, w1, w3, w2)


def _combine_kernel(idx_ref, y_ref, o_ref, *, cap):
    e = pl.program_id(2)

    @pl.when(e == 0)
    def _():
        o_ref[...] = jnp.zeros(o_ref.shape, F32)

    base = (pl.program_id(0) * pl.num_programs(2) + e) * cap

    def body(s, carry):
        t = idx_ref[base + s]
        o_ref[0, pl.ds(t, 1), :] = o_ref[0, pl.ds(t, 1), :] + y_ref[0, pl.ds(s, 1), :]
        return carry

    lax.fori_loop(0, cap, body, 0, unroll=8)


def _combine(idx_flat, y, b, n_tok, cap):
    ne, _, d = y.shape
    dh = d // 2
    return pl.pallas_call(
        functools.partial(_combine_kernel, cap=cap),
        grid_spec=pltpu.PrefetchScalarGridSpec(
            num_scalar_prefetch=1,
            grid=(b, 2, ne),
            in_specs=[pl.BlockSpec((1, cap, dh), lambda bi, c, e, idx: (e, bi, c))],
            out_specs=pl.BlockSpec((1, n_tok, dh), lambda bi, c, e, idx: (bi, 0, c)),
        ),
        out_shape=jax.ShapeDtypeStruct((b, n_tok, d), F32),
        compiler_params=_cparams(("arbitrary", "arbitrary", "arbitrary")),
        name="moe_combine",
    )(idx_flat, y)


def _residual_kernel(x1_ref, mod_ref, mx_ref, mc_ref, o_ref, *, d_model, blk0):
    i = pl.program_id(1) + blk0
    moe = jnp.where(i == 0, mc_ref[0], mx_ref[0])
    o_ref[0] = x1_ref[0] + mod_ref[:, 5 * d_model:6 * d_model] * moe


def _residual(x1, modsel, moe_x, moe_c, tm, blk0):
    b, t, d = x1.shape
    nb = t // tm - blk0
    return pl.pallas_call(
        functools.partial(_residual_kernel, d_model=d, blk0=blk0),
        grid=(b, nb),
        in_specs=[pl.BlockSpec((1, tm, d), lambda bi, i: (bi, i + blk0, 0)),
                  pl.BlockSpec((None, None, 1, N_MOD * d), lambda bi, i: (bi, jnp.minimum(i + blk0, 1), 0, 0)),
                  pl.BlockSpec((1, tm, d), lambda bi, i: (bi, jnp.maximum(i + blk0 - 1, 0), 0)),
                  pl.BlockSpec((1, tm, d), lambda bi, i: (bi, 0, 0))],
        out_specs=pl.BlockSpec((1, tm, d), lambda bi, i: (bi, i, 0)),
        out_shape=jax.ShapeDtypeStruct((b, nb * tm, d), F32),
        compiler_params=_cparams(("parallel", "parallel")),
        name="moe_residual",
    )(x1, modsel, moe_x, moe_c)


def _rope_tables(ctx_len, seq):
    n = jnp.arange(seq)
    pos = jnp.stack([n // GRID_W, n % GRID_W], axis=-1).astype(F32)
    lane = jnp.arange(LANES) % HEAD_DIM
    axis = lane // 32
    n_freq = HEAD_DIM // 4
    inv = ROPE_THETA ** (-(lane % n_freq).astype(F32) / n_freq)
    ang = pos[:, axis] * inv
    sign = jnp.where((lane % 32) < 16, -1.0, 1.0)
    cos_t = jnp.concatenate([jnp.ones((ctx_len, LANES), F32), jnp.cos(ang)], axis=0)
    sin_t = jnp.concatenate([jnp.zeros((ctx_len, LANES), F32), jnp.sin(ang) * sign], axis=0)
    return cos_t, sin_t


def _reorder_w_in(w):
    d = w.shape[0]
    qk = w[:, :ATTN_W + KV_W]
    v = w[:, ATTN_W + KV_W:ATTN_W + 2 * KV_W]
    o = ATTN_W + 2 * KV_W
    ml = w[:, o:o + 4 * MLSTM_W]
    o += 4 * MLSTM_W
    mgate = w[:, o:o + 2 * N_DIR * MLSTM_HEADS]
    o += 2 * N_DIR * MLSTM_HEADS
    gd = w[:, o:o + 4 * GDN_W]
    o += 4 * GDN_W
    ggate = w[:, o:o + 2 * N_DIR * GDN_HEADS]
    pad = jnp.zeros((d, GATE_W - mgate.shape[1] - ggate.shape[1]), w.dtype)
    w_r = jnp.concatenate([qk, ml, gd, mgate, ggate, pad], axis=1)
    w_t = jnp.concatenate([v, ml[:, MLSTM_W:2 * MLSTM_W], mgate, ggate], axis=1).T
    return w_r.astype(BF16), w_t.astype(BF16)


def _chunk_transpose(a):
    b, h, t, e = a.shape
    return jnp.swapaxes(a.reshape(b, h, t // CHUNK, CHUNK, e), -1, -2)


def _gate_row(vals, col0):
    flat = vals.reshape(-1).astype(F32)
    return jnp.zeros((1, GATE_W), F32).at[0, col0:col0 + flat.shape[0]].set(flat)


def _gate_col(first, second):
    flat = jnp.concatenate([first.reshape(-1), second.reshape(-1)]).astype(F32)
    return jnp.broadcast_to(flat[:, None], (flat.shape[0], CHUNK))


def kernel(x, c, ctx, c_ctx, mod_w, mod_b, norm1_g, w_in, q_norm_g, k_norm_g, mlstm_i_bias, mlstm_f_bias,
           mlstm_out_g, gdn_conv_w, gdn_a_log, gdn_dt_bias, gdn_out_g, w_out, norm2_g, router_w, w1, w3, w2):
    b, seq, d = x.shape
    ctx_len = ctx.shape[1]
    depth = mod_w.shape[0]
    tm = ctx_len
    t = ctx_len + seq
    ne = N_EXPERTS
    cap_x = CAPACITY_FACTOR * seq // ne
    cap_c = CAPACITY_FACTOR * ctx_len // ne
    rows_c = 2 * SUBLANES

    cvec = jnp.concatenate([c, c_ctx[None, :], jnp.zeros((SUBLANES - b - 1, d), F32)], axis=0)
    mod = _modulation(cvec, mod_w, mod_b)
    cos_t, sin_t = _rope_tables(ctx_len, seq)
    bd = jnp.kron(jnp.eye(LANES // HEAD_DIM, dtype=F32), jnp.full((HEAD_DIM, HEAD_DIM), 1.0 / HEAD_DIM, F32))
    xa = jnp.concatenate([ctx, x], axis=1)
    zeros8 = jnp.zeros((N_DIR, MLSTM_HEADS), F32)

    for l in range(depth):
        need_ctx = l < depth - 1
        blk0 = 0 if need_ctx else 1
        modsel = jnp.stack([jnp.broadcast_to(mod[l, b], (b, N_MOD * d)), mod[l, :b]], axis=1)[:, :, None, :]
        w_r, w_t = _reorder_w_in(w_in[l])
        qg = jnp.tile(q_norm_g[l], LANES // HEAD_DIM)[None, :]
        kg = jnp.tile(k_norm_g[l], LANES // HEAD_DIM)[None, :]
        q, k, vt, mqkv, mkt, mo, gqkv, gz, gates, gates_t = _inproj(
            xa, modsel, norm1_g[l][None, :], w_r, w_t, cos_t, sin_t, qg, kg, bd, tm)

        ao = _attention(q, k, vt, ctx_len, tm, blk0)

        brow = _gate_row(mlstm_i_bias[l], GC_MI) + _gate_row(mlstm_f_bias[l], GC_MF)
        bcol = _gate_col(mlstm_i_bias[l], mlstm_f_bias[l])
        mhf, mhr = _mlstm_scan(mqkv, mkt, gates, gates_t, brow, bcol, ctx_len)

        conv12 = gdn_conv_w[l].reshape(CONV_K, 3 * GDN_HEADS, HEAD_DIM).transpose(1, 0, 2)
        gp = _gdn_prep(gqkv, conv12, ctx_len)
        neg_a = -jnp.exp(gdn_a_log[l].astype(F32))
        vk = jnp.concatenate([gp[:, 8:12], gp[:, 4:8]], axis=-1)
        ghf, ghr = _gdn_scan(gp, vk, _chunk_transpose(gp[:, 4:8]), gates, gates_t,
                             _gate_row(neg_a, GC_GA), _gate_row(gdn_dt_bias[l], GC_GA),
                             _gate_col(neg_a, zeros8), _gate_col(gdn_dt_bias[l], zeros8), ctx_len)

        w_hm = w_out[l].astype(BF16)
        x1, hp, aff = _outproj(xa, modsel, ao, mhf, mhr, mo, mlstm_out_g[l], ghf, ghr, gz, gdn_out_g[l][None, :],
                               w_hm, norm2_g[l][None, :], router_w[l].T, tm, blk0)

        idx_x, val_x = _route(aff[:, :, ctx_len:].reshape(b, ne, seq // LANES, LANES), cap_x)
        idx_xf = idx_x.reshape(-1)
        xgs = [_gather(idx_xf, hp, cap_x, t, ctx_len)]
        vals = [val_x.transpose(1, 0, 2, 3).reshape(ne, b * cap_x, 1)]
        if need_ctx:
            aff_c = jnp.pad(aff[:, :, :ctx_len], ((0, 0), (0, 0), (0, rows_c * LANES - ctx_len)), constant_values=-1.0)
            idx_c, val_c = _route(aff_c.reshape(b, ne, rows_c, LANES), cap_c)
            idx_cf = idx_c.reshape(-1)
            xgs.append(_gather(idx_cf, hp, cap_c, ctx_len, 0))
            vals.append(val_c.transpose(1, 0, 2, 3).reshape(ne, b * cap_c, 1))
        ys = _ffn(xgs, vals, w1[l], w3[l], w2[l], min(FFN_TF, w1.shape[3]))
        moe_x = _combine(idx_xf, ys[0], b, seq, cap_x)
        moe_c = _combine(idx_cf, ys[1], b, ctx_len, cap_c) if need_ctx else moe_x
        xa = _residual(x1, modsel, moe_x, moe_c, tm, blk0)
    return xa
```

```python
import functools
import math

import jax
import jax.numpy as jnp
from jax import lax
from jax.experimental import pallas as pl
from jax.experimental.pallas import tpu as pltpu

F32 = jnp.float32
BF16 = jnp.bfloat16
I32 = jnp.int32
U32 = jnp.uint32
HIGHEST = lax.Precision.HIGHEST

HEAD_DIM = 64
ATTN_HEADS = 8
ATTN_KV_HEADS = 2
ATTN_REP = ATTN_HEADS // ATTN_KV_HEADS
MLSTM_HEADS = 4
GDN_HEADS = 4
N_DIR = 2
CHUNK = 64
CONV_K = 5
GRID_W = 64
ROPE_THETA = 10000.0
N_EXPERTS = 16
CAPACITY_FACTOR = 2
N_MOD = 6
EPS = 1e-6
ATTN_W = ATTN_HEADS * HEAD_DIM
KV_W = ATTN_KV_HEADS * HEAD_DIM
MLSTM_W = MLSTM_HEADS * HEAD_DIM
GDN_W = GDN_HEADS * HEAD_DIM
LANES = 128
SUBLANES = 8
GATE_W = LANES
GATE_ROWS = 32
GC_MI, GC_MF, GC_GA, GC_GB = 0, 8, 16, 24
VMEM_LIMIT = 56 * 1024 * 1024
ATTN_COLS = 256
ATTN_VPAD = 16
LOG2E = 1.4426950408889634
FFN_TF = 256
CONV_ROWS = 256
PAD_ROWS = 8


def _cparams(sem):
    return pltpu.CompilerParams(dimension_semantics=sem, vmem_limit_bytes=VMEM_LIMIT)


def _sigmoid(x):
    return 1.0 / (1.0 + jnp.exp(-x))


def _silu(x):
    return x * _sigmoid(x)


def _log_sigmoid(x):
    return jnp.minimum(x, 0.0) - jnp.log1p(jnp.exp(-jnp.abs(x)))


def _softplus(x):
    return jnp.maximum(x, 0.0) + jnp.log1p(jnp.exp(-jnp.abs(x)))


def _dot(a, b, precision=None):
    return jnp.dot(a, b, preferred_element_type=F32, precision=precision)


def _dot_nt(a, b, precision=None):
    return lax.dot_general(a, b, (((1,), (1,)), ((), ())), preferred_element_type=F32, precision=precision)


def _split_bf16(a):
    hi = a.astype(BF16)
    lo = (a - hi.astype(F32)).astype(BF16)
    return hi, lo


def _split_bf16_3(a):
    hi = a.astype(BF16)
    r = a - hi.astype(F32)
    mid = r.astype(BF16)
    return hi, mid, (r - mid.astype(F32)).astype(BF16)


def _dot3(a, b):
    ah, al = _split_bf16(a)
    bh, bl = _split_bf16(b)
    return _dot(ah, bh) + (_dot(ah, bl) + _dot(al, bh))


def _iota(shape, dim):
    return lax.broadcasted_iota(I32, shape, dim)


def _eye_rows(rows, cols, first):
    return (_iota((rows, cols), 0) + first == _iota((rows, cols), 1)).astype(F32)


def _mod_kernel(c_ref, w_ref, b_ref, o_ref):
    s = _silu(c_ref[...])
    o_ref[0] = _dot(s, w_ref[0], precision=HIGHEST) + b_ref[0]


def _modulation(cvec, mod_w, mod_b):
    depth, d, n = mod_w.shape
    rows = cvec.shape[0]
    tn = d
    return pl.pallas_call(
        _mod_kernel,
        grid=(depth, n // tn),
        in_specs=[
            pl.BlockSpec((rows, d), lambda l, j: (0, 0)),
            pl.BlockSpec((1, d, tn), lambda l, j: (l, 0, j)),
            pl.BlockSpec((1, 1, tn), lambda l, j: (l, 0, j)),
        ],
        out_specs=pl.BlockSpec((1, rows, tn), lambda l, j: (l, 0, j)),
        out_shape=jax.ShapeDtypeStruct((depth, rows, n), F32),
        compiler_params=_cparams(("parallel", "parallel")),
        name="modulation",
    )(cvec, mod_w, mod_b.reshape(depth, 1, n))


def _inproj_kernel(x_ref, mod_ref, g_ref, w_ref, wt_ref, cos_ref, sin_ref, qg_ref, kg_ref, bd_ref,
                   q_ref, k_ref, vt_ref, ml_ref, mkt_ref, mo_ref, gd_ref, gz_ref, gate_ref, gatet_ref, *, d_model):
    x = x_ref[0]
    tm = x.shape[0]
    sh = mod_ref[:, 0:d_model]
    sc = mod_ref[:, d_model:2 * d_model]
    xn = x * lax.rsqrt(jnp.mean(x * x, axis=-1, keepdims=True) + EPS) * g_ref[...]
    h = (xn * (1.0 + sc) + sh).astype(BF16)
    p = _dot(h, w_ref[...])
    pt = _dot_nt(wt_ref[...], h)
    for g in range(ATTN_KV_HEADS):
        vt_ref[0, g, 0:HEAD_DIM, :] = pt[g * HEAD_DIM:(g + 1) * HEAD_DIM].astype(BF16)
        vt_ref[0, g, HEAD_DIM:, :] = (_iota((ATTN_VPAD, tm), 0) == 0).astype(BF16)
    for c in range(tm // CHUNK):
        cs = slice(c * CHUNK, (c + 1) * CHUNK)
        for j in range(MLSTM_HEADS):
            mkt_ref[0, j, c] = pt[KV_W + j * HEAD_DIM:KV_W + (j + 1) * HEAD_DIM, cs] * HEAD_DIM ** -0.5
        gatet_ref[0, c] = pt[KV_W + MLSTM_W:KV_W + MLSTM_W + GATE_ROWS, cs]

    cos = cos_ref[...]
    sin = sin_ref[...]
    first_half = (_iota(cos.shape, 1) % 32) < 16

    def norm_rope(xs, g, scale):
        ms = _dot(xs * xs, bd_ref[...], precision=HIGHEST)
        xn_ = xs * lax.rsqrt(ms + EPS) * g
        sw = jnp.where(first_half, pltpu.roll(xn_, LANES - 16, 1), pltpu.roll(xn_, 16, 1))
        return (xn_ * cos + sw * sin) * scale

    for j in range(ATTN_W // LANES):
        qs = norm_rope(p[:, j * LANES:(j + 1) * LANES], qg_ref[...], LOG2E * HEAD_DIM ** -0.5).astype(BF16)
        q_ref[0, 2 * j] = qs[:, 0:HEAD_DIM]
        q_ref[0, 2 * j + 1] = qs[:, HEAD_DIM:LANES]
    ks = norm_rope(p[:, ATTN_W:ATTN_W + KV_W], kg_ref[...], 1.0).astype(BF16)
    k_ref[0, 0] = ks[:, 0:HEAD_DIM]
    k_ref[0, 1] = ks[:, HEAD_DIM:LANES]

    off = ATTN_W + KV_W

    def head(j):
        return p[:, off + j * HEAD_DIM: off + (j + 1) * HEAD_DIM]

    for j in range(12):
        ml_ref[0, j] = head(j) * HEAD_DIM ** -0.5 if 4 <= j < 8 else head(j)
    for j in range(4):
        mo_ref[0, j] = head(12 + j)
    off += 4 * MLSTM_W
    for j in range(12):
        gd_ref[0, j] = head(j)
    for j in range(4):
        gz_ref[0, j] = head(12 + j)
    off += 4 * GDN_W
    gate_ref[0] = p[:, off:off + GATE_W]


def _inproj(xa, modsel, g1, w_r, w_t, cos_t, sin_t, qg, kg, bd, tm):
    b, t, d = xa.shape
    nb = t // tm
    nc = tm // CHUNK
    kern = functools.partial(_inproj_kernel, d_model=d)
    hm_shape = lambda nh, dt: jax.ShapeDtypeStruct((b, nh, t, HEAD_DIM), dt)
    out_shapes = (hm_shape(ATTN_HEADS, BF16), hm_shape(ATTN_KV_HEADS, BF16),
                  jax.ShapeDtypeStruct((b, ATTN_KV_HEADS, HEAD_DIM + ATTN_VPAD, t), BF16),
                  hm_shape(12, F32), jax.ShapeDtypeStruct((b, MLSTM_HEADS, t // CHUNK, HEAD_DIM, CHUNK), F32),
                  hm_shape(4, F32), hm_shape(12, F32), hm_shape(4, F32),
                  jax.ShapeDtypeStruct((b, t, GATE_W), F32),
                  jax.ShapeDtypeStruct((b, t // CHUNK, GATE_ROWS, CHUNK), F32))
    hm = lambda nh: pl.BlockSpec((1, nh, tm, HEAD_DIM), lambda bi, i: (bi, 0, i, 0))
    const = lambda a: pl.BlockSpec(a.shape, lambda bi, i: (0,) * a.ndim)
    return pl.pallas_call(
        kern,
        grid=(b, nb),
        in_specs=[
            pl.BlockSpec((1, tm, d), lambda bi, i: (bi, i, 0)),
            pl.BlockSpec((None, None, 1, N_MOD * d), lambda bi, i: (bi, jnp.minimum(i, 1), 0, 0)),
            const(g1), const(w_r), const(w_t),
            pl.BlockSpec((tm, LANES), lambda bi, i: (i, 0)),
            pl.BlockSpec((tm, LANES), lambda bi, i: (i, 0)),
            const(qg), const(kg), const(bd),
        ],
        out_specs=(hm(ATTN_HEADS), hm(ATTN_KV_HEADS),
                   pl.BlockSpec((1, ATTN_KV_HEADS, HEAD_DIM + ATTN_VPAD, tm), lambda bi, i: (bi, 0, 0, i)),
                   hm(12), pl.BlockSpec((1, MLSTM_HEADS, nc, HEAD_DIM, CHUNK), lambda bi, i: (bi, 0, i, 0, 0)),
                   hm(4), hm(12), hm(4),
                   pl.BlockSpec((1, tm, GATE_W), lambda bi, i: (bi, i, 0)),
                   pl.BlockSpec((1, nc, GATE_ROWS, CHUNK), lambda bi, i: (bi, i, 0, 0))),
        out_shape=out_shapes,
        compiler_params=_cparams(("parallel", "parallel")),
        name="inproj",
    )(xa, modsel, g1, w_r, w_t, cos_t, sin_t, qg, kg, bd)


def _attn_kernel(q_ref, k_ref, vt_ref, o_ref, sa_sc, sb_sc, xa_sc, xb_sc, m_sc, acc_sc, *, tq, tk, n_pairs, blk0):
    i = pl.program_id(2) + blk0
    q = q_ref[0].reshape(ATTN_REP * tq, HEAD_DIM)
    cols = [slice(c * ATTN_COLS, (c + 1) * ATTN_COLS) for c in range(ATTN_REP * tq // ATTN_COLS)]

    m_sc[...] = jnp.full(m_sc.shape, -jnp.inf, F32)
    acc_sc[...] = jnp.zeros(acc_sc.shape, F32)

    def score(s_ref, x_ref, tile):
        start = pl.multiple_of(tile * tk, tk)
        k = k_ref[0, 0, pl.ds(start, tk), :]
        for cs in cols:
            s = _dot_nt(k, q[cs])
            s_ref[:, cs] = s
            x_ref[:, cs] = jnp.max(s, axis=0, keepdims=True)

    def consume(s_ref, x_ref, tile, half):
        start = pl.multiple_of(tile * tk, tk)
        vt = vt_ref[0, 0, :, pl.ds(start, tk)]
        for cs in cols:
            mo = m_sc[half, :, cs]
            mn = jnp.maximum(mo, x_ref[:, cs])
            p = jnp.exp2(s_ref[:, cs] - mn)
            acc_sc[half, :, cs] = jnp.exp2(mo - mn) * acc_sc[half, :, cs] + _dot(vt, p.astype(BF16))
            m_sc[half, :, cs] = mn

    score(sa_sc, xa_sc, 0)

    def body(j, carry):
        score(sb_sc, xb_sc, 2 * j + 1)
        consume(sa_sc, xa_sc, 2 * j, 0)
        score(sa_sc, xa_sc, 2 * j + 2)
        consume(sb_sc, xb_sc, 2 * j + 1, 1)
        return carry

    n = jnp.where(i == 0, 0, n_pairs)
    lax.fori_loop(0, n, body, 0)
    consume(sa_sc, xa_sc, 2 * n, 0)
    m = jnp.maximum(m_sc[0], m_sc[1])
    acc = jnp.exp2(m_sc[0] - m) * acc_sc[0] + jnp.exp2(m_sc[1] - m) * acc_sc[1]
    o_t = (acc[0:HEAD_DIM, :] / acc[HEAD_DIM:HEAD_DIM + 1, :]).astype(BF16)
    eye = (_iota((tq, tq), 0) == _iota((tq, tq), 1)).astype(BF16)
    for r in range(ATTN_REP):
        o_ref[0, r] = _dot_nt(eye, o_t[:, r * tq:(r + 1) * tq]).astype(o_ref.dtype)


def _attention(q, k, vt, ctx_len, tq, blk0):
    b, _, t, _ = q.shape
    tk = ctx_len
    assert (t // tk) % 2 == 1
    nq = t // tq - blk0
    kern = functools.partial(_attn_kernel, tq=tq, tk=tk, n_pairs=(t // tk - 1) // 2, blk0=blk0)
    rows = ATTN_REP * tq
    return pl.pallas_call(
        kern,
        grid=(b, ATTN_KV_HEADS, nq),
        in_specs=[
            pl.BlockSpec((1, ATTN_REP, tq, HEAD_DIM), lambda bi, g, i: (bi, g, i + blk0, 0)),
            pl.BlockSpec((1, 1, t, HEAD_DIM), lambda bi, g, i: (bi, g, 0, 0)),
            pl.BlockSpec((1, 1, HEAD_DIM + ATTN_VPAD, t), lambda bi, g, i: (bi, g, 0, 0)),
        ],
        out_specs=pl.BlockSpec((1, ATTN_REP, tq, HEAD_DIM), lambda bi, g, i: (bi, g, i + blk0, 0)),
        out_shape=jax.ShapeDtypeStruct((b, ATTN_HEADS, t, HEAD_DIM), BF16),
        scratch_shapes=[pltpu.VMEM((tk, rows), F32), pltpu.VMEM((tk, rows), F32),
                        pltpu.VMEM((1, rows), F32), pltpu.VMEM((1, rows), F32),
                        pltpu.VMEM((2, 1, rows), F32), pltpu.VMEM((2, HEAD_DIM + ATTN_VPAD, rows), F32)],
        compiler_params=_cparams(("parallel", "parallel", "arbitrary")),
        name="attention",
    )(q, k, vt)


def _chunk_maps(nc_c, nc_x):
    fwd = lambda j: j
    rev = lambda j: jnp.where(j < nc_c, nc_c - 1 - j, 2 * nc_c + nc_x - 1 - j)
    return fwd, rev


def _dir_masks(d):
    r = _iota((CHUNK, CHUNK), 0)
    c = _iota((CHUNK, CHUNK), 1)
    incl = (r >= c) if d == 0 else (r <= c)
    strict = (r > c) if d == 0 else (r < c)
    incl_t = (r <= c) if d == 0 else (r >= c)
    return incl, strict, incl.astype(F32), incl_t.astype(F32)


def _mlstm_kernel(xf_ref, xr_ref, ktf_ref, ktr_ref, gf_ref, gr_ref, gtf_ref, gtr_ref, brow_ref, bcol_ref,
                  hf_ref, hr_ref, c_sc, n_sc, m_sc, *, nb):
    @pl.when(pl.program_id(0) == 0)
    def _():
        c_sc[...] = jnp.zeros(c_sc.shape, F32)
        n_sc[...] = jnp.zeros(n_sc.shape, F32)
        m_sc[...] = jnp.zeros(m_sc.shape, F32)

    refs = ((xf_ref, ktf_ref, gf_ref, gtf_ref, hf_ref), (xr_ref, ktr_ref, gr_ref, gtr_ref, hr_ref))
    masks = [_dir_masks(d) for d in range(N_DIR)]
    gate = {}
    for d in range(N_DIR):
        _, _, g_ref, gt_ref, _ = refs[d]
        _, _, tri, tri_t = masks[d]
        sel = (_iota((GATE_W, MLSTM_HEADS * LANES), 0)
               == GC_MF + d * MLSTM_HEADS + _iota((GATE_W, MLSTM_HEADS * LANES), 1) // LANES).astype(BF16)
        for b in range(nb):
            cum = _dot(tri, _log_sigmoid(g_ref[b] + brow_ref[...]), precision=HIGHEST)
            g_t = gt_ref[b, 0, GC_MI:GC_MI + 16, :] + bcol_ref[...]
            lf_t = _log_sigmoid(g_t)
            cum_rep = sum(_dot(piece, sel) for piece in _split_bf16_3(cum))
            gate[d, b] = (cum_rep, g_t, lf_t, _dot(lf_t, tri_t, precision=HIGHEST))

    probs = [(d, b, h) for d in range(N_DIR) for b in range(nb) for h in range(MLSTM_HEADS)]
    st = []
    for d, b, h in probs:
        p = (d * nb + b) * MLSTM_HEADS + h
        x_ref, kt_ref = refs[d][0], refs[d][1]
        cum_rep, g_t, lf_t, cum_t = gate[d, b]
        ri = d * MLSTM_HEADS + h
        rf = GC_MF + ri
        bcum_col = cum_rep[:, h * LANES:h * LANES + CHUNK]
        bcum_row = cum_t[rf - GC_MI:rf - GC_MI + 1, :]
        i_row = g_t[ri:ri + 1, :]
        b_last = jnp.sum(lf_t[rf - GC_MI:rf - GC_MI + 1, :], axis=-1, keepdims=True)
        m_old = m_sc[p, 0:1, 0:1]
        w_end = b_last - bcum_row + i_row
        m_new = jnp.maximum(b_last + m_old, jnp.max(w_end, axis=-1, keepdims=True))
        a_row = jnp.exp(w_end - m_new)
        dec = jnp.exp(b_last + m_old - m_new)
        dmat = jnp.where(masks[d][0], bcum_col - bcum_row + i_row, -jnp.inf)
        inter = bcum_col + m_sc[p, 0:1, 0:CHUNK]
        m_t = jnp.maximum(inter, jnp.max(dmat, axis=-1, keepdims=True))
        st.append(dict(p=p, q=x_ref[b, h], k=x_ref[b, MLSTM_HEADS + h], v=x_ref[b, 2 * MLSTM_HEADS + h],
                       k_t=kt_ref[b, h, 0], c_old=c_sc[p], n_old=n_sc[p], a_row=a_row, dec=dec, m_new=m_new,
                       m_t=m_t, w_in=jnp.exp(inter - m_t), dexp=jnp.exp(dmat - m_t), out=refs[d][4], b=b, h=h))
    for e in st:
        e["s"] = _dot(e["q"], e["k_t"]) * e["dexp"]
    for e in st:
        e["qc"] = _dot(e["q"], e["c_old"])
    for e in st:
        e["sv"] = _dot(e["s"], e["v"])
    for e in st:
        e["kv"] = _dot(e["k_t"] * e["a_row"], e["v"])
    for e in st:
        e["ak"] = _dot(jnp.broadcast_to(e["a_row"], (SUBLANES, CHUNK)), e["k"])
    for e in st:
        num = e["w_in"] * e["qc"] + e["sv"]
        den = (e["w_in"] * jnp.sum(e["q"] * e["n_old"][0:1, :], axis=-1, keepdims=True)
               + jnp.sum(e["s"], axis=-1, keepdims=True))
        e["out"][e["b"], e["h"]] = num / jnp.maximum(jnp.abs(den), jnp.exp(-e["m_t"]))
    for e in st:
        p = e["p"]
        c_sc[p] = e["dec"] * e["c_old"] + e["kv"]
        n_sc[p] = e["dec"] * e["n_old"] + e["ak"]
        m_sc[p] = jnp.broadcast_to(e["m_new"], (SUBLANES, LANES))


def _mlstm_scan(mqkv, k_t, gates, gates_t, brow, bcol, ctx_len):
    b, _, t, _ = mqkv.shape
    nc_c, nc_x = ctx_len // CHUNK, (t - ctx_len) // CHUNK
    fwd, rev = _chunk_maps(nc_c, nc_x)
    nprob = N_DIR * b * MLSTM_HEADS
    xs = lambda f: pl.BlockSpec((b, 12, CHUNK, HEAD_DIM), lambda j: (0, 0, f(j), 0))
    ks = lambda f: pl.BlockSpec((b, MLSTM_HEADS, 1, HEAD_DIM, CHUNK), lambda j: (0, 0, f(j), 0, 0))
    gs = lambda f: pl.BlockSpec((b, CHUNK, GATE_W), lambda j: (0, f(j), 0))
    gts = lambda f: pl.BlockSpec((b, 1, GATE_ROWS, CHUNK), lambda j: (0, f(j), 0, 0))
    hs = lambda f: pl.BlockSpec((b, MLSTM_HEADS, CHUNK, HEAD_DIM), lambda j: (0, 0, f(j), 0))
    out = jax.ShapeDtypeStruct((b, MLSTM_HEADS, t, HEAD_DIM), F32)
    return pl.pallas_call(
        functools.partial(_mlstm_kernel, nb=b),
        grid=(nc_c + nc_x,),
        in_specs=[xs(fwd), xs(rev), ks(fwd), ks(rev), gs(fwd), gs(rev), gts(fwd), gts(rev),
                  pl.BlockSpec((1, GATE_W), lambda j: (0, 0)),
                  pl.BlockSpec((16, CHUNK), lambda j: (0, 0))],
        out_specs=(hs(fwd), hs(rev)),
        out_shape=(out, out),
        scratch_shapes=[pltpu.VMEM((nprob, HEAD_DIM, HEAD_DIM), F32),
                        pltpu.VMEM((nprob, SUBLANES, HEAD_DIM), F32),
                        pltpu.VMEM((nprob, SUBLANES, LANES), F32)],
        compiler_params=_cparams(("arbitrary",)),
        name="mlstm_scan",
    )(mqkv, mqkv, k_t, k_t, gates, gates, gates_t, gates_t, brow, bcol)


def _gdn_prep_kernel(x_ref, w_ref, o_ref, pad_sc, *, ctx_len, t):
    part = pl.program_id(1) // GDN_HEADS
    zeros = jnp.zeros((PAD_ROWS, HEAD_DIM), F32)
    pad_sc[0:PAD_ROWS] = zeros
    pad_sc[PAD_ROWS:PAD_ROWS + ctx_len] = x_ref[0, 0, 0:ctx_len]
    pad_sc[PAD_ROWS + ctx_len:2 * PAD_ROWS + ctx_len] = zeros
    pad_sc[2 * PAD_ROWS + ctx_len:2 * PAD_ROWS + t] = x_ref[0, 0, ctx_len:t]
    pad_sc[2 * PAD_ROWS + t:3 * PAD_ROWS + t] = zeros
    w = w_ref[0]
    is_qk = part < 2
    scale = jnp.where(part == 0, HEAD_DIM ** -0.5, 1.0)
    for c in range(t // CONV_ROWS):
        r0 = c * CONV_ROWS
        base = r0 + (PAD_ROWS if r0 < ctx_len else 2 * PAD_ROWS) - CONV_K // 2
        y = w[0:1, :] * pad_sc[base:base + CONV_ROWS]
        for j in range(1, CONV_K):
            y = y + w[j:j + 1, :] * pad_sc[base + j:base + j + CONV_ROWS]
        y = _silu(y)
        yn = y * lax.rsqrt(jnp.sum(y * y, axis=-1, keepdims=True) + EPS) * scale
        o_ref[0, 0, r0:r0 + CONV_ROWS] = jnp.where(is_qk, yn, y)


def _gdn_prep(gqkv, conv_w12, ctx_len):
    b, np_, t, _ = gqkv.shape
    return pl.pallas_call(
        functools.partial(_gdn_prep_kernel, ctx_len=ctx_len, t=t),
        grid=(b, np_),
        in_specs=[pl.BlockSpec((1, 1, t, HEAD_DIM), lambda bi, p: (bi, p, 0, 0)),
                  pl.BlockSpec((1, CONV_K, HEAD_DIM), lambda bi, p: (p, 0, 0))],
        out_specs=pl.BlockSpec((1, 1, t, HEAD_DIM), lambda bi, p: (bi, p, 0, 0)),
        out_shape=jax.ShapeDtypeStruct(gqkv.shape, F32),
        scratch_shapes=[pltpu.VMEM((t + 3 * PAD_ROWS, HEAD_DIM), F32)],
        compiler_params=_cparams(("parallel", "parallel")),
        name="gdn_prep",
    )(gqkv, conv_w12)


def _gdn_kernel(xf_ref, xr_ref, gf_ref, gr_ref, gtf_ref, gtr_ref, arow_ref, drow_ref, acol_ref, dcol_ref,
                of_ref, or_ref, s_sc, rhs_sc, *, nb):
    @pl.when(pl.program_id(0) == 0)
    def _():
        s_sc[...] = jnp.zeros(s_sc.shape, F32)

    refs = ((xf_ref, gf_ref, gtf_ref, of_ref), (xr_ref, gr_ref, gtr_ref, or_ref))
    masks = [_dir_masks(d) for d in range(N_DIR)]
    gate = {}
    for d in range(N_DIR):
        g_ref, gt_ref = refs[d][1], refs[d][2]
        _, _, tri, tri_t = masks[d]
        tile = _iota((GATE_W, 2 * GDN_HEADS * LANES), 1) // LANES
        src = jnp.where(tile < GDN_HEADS, GC_GA + d * GDN_HEADS + tile, GC_GB + d * GDN_HEADS + tile - GDN_HEADS)
        sel = (_iota((GATE_W, 2 * GDN_HEADS * LANES), 0) == src).astype(BF16)
        is_decay = _iota((CHUNK, GATE_W), 1) < GC_GB
        for b in range(nb):
            g = g_ref[b]
            gval = arow_ref[...] * _softplus(g + drow_ref[...])
            g_t = gt_ref[b, 0, GC_GA:GC_GA + 16, :]
            gval_t = acol_ref[...] * _softplus(g_t + dcol_ref[...])
            cols = jnp.where(is_decay, _dot(tri, gval, precision=HIGHEST), _sigmoid(g))
            rep = sum(_dot(piece, sel) for piece in _split_bf16_3(cols))
            gate[d, b] = (rep, gval_t, _dot(gval_t, tri_t, precision=HIGHEST))

    probs = [(d, b, h) for d in range(N_DIR) for b in range(nb) for h in range(GDN_HEADS)]
    eye = _eye_rows(HEAD_DIM, HEAD_DIM, 0).astype(BF16)
    st = []
    for d, b, h in probs:
        x_ref = refs[d][0]
        st.append(dict(p=(d * nb + b) * GDN_HEADS + h, d=d, b=b, h=h, q=x_ref[b, h], k=x_ref[b, GDN_HEADS + h],
                       v=x_ref[b, 2 * GDN_HEADS + h]))
    for e in st:
        e["k_t"] = sum(_dot_nt(eye, piece) for piece in _split_bf16(e["k"]))
    for e in st:
        d, b, h, p = e["d"], e["b"], e["h"], e["p"]
        incl, strict, _, _ = masks[d]
        rep, gval_t, gcum_t = gate[d, b]
        ra = d * GDN_HEADS + h
        g_col = rep[:, h * LANES:h * LANES + CHUNK]
        g_row = gcum_t[ra:ra + 1, :]
        beta = rep[:, (GDN_HEADS + h) * LANES:(GDN_HEADS + h) * LANES + CHUNK]
        g_last = jnp.sum(gval_t[ra:ra + 1, :], axis=-1, keepdims=True)
        eg = jnp.exp(g_col)
        kb = e["k"] * beta
        rhs_sc[p, :, 0:HEAD_DIM] = e["v"] * beta
        rhs_sc[p, :, HEAD_DIM:] = kb * eg
        e.update(out=refs[d][3], strict=strict, eg_last=jnp.exp(g_last), kb=kb, qg=e["q"] * eg,
                 decay=jnp.where(incl, jnp.exp(jnp.where(incl, g_col - g_row, 0.0)), 0.0),
                 ktg=e["k_t"] * jnp.exp(g_last - g_row), s_old=s_sc[p])
    for e in st:
        e["sol"] = rhs_sc[e["p"]]
    for e in st:
        e["pw"] = -jnp.where(e["strict"], _dot3(e["kb"], e["k_t"]) * e["decay"], 0.0)
    n_fac = CHUNK.bit_length() - 1
    for it in range(n_fac):
        for e in st:
            e["sol"] = e["sol"] + _dot3(e["pw"], e["sol"])
        if it < n_fac - 1:
            for e in st:
                e["pw"] = _dot3(e["pw"], e["pw"])
    for e in st:
        e["ws"] = _dot(e["sol"][:, HEAD_DIM:], e["s_old"])
    for e in st:
        e["qs"] = _dot(e["qg"], e["s_old"])
    for e in st:
        e["qk"] = _dot(e["q"], e["k_t"]) * e["decay"]
    for e in st:
        e["v_new"] = e["sol"][:, :HEAD_DIM] - e["ws"]
    for e in st:
        e["out"][e["b"], e["h"]] = e["qs"] + _dot(e["qk"], e["v_new"])
    for e in st:
        s_sc[e["p"]] = e["eg_last"] * e["s_old"] + _dot(e["ktg"], e["v_new"])


def _gdn_scan(gp, gates, gates_t, arow, drow, acol, dcol, ctx_len):
    b, _, t, _ = gp.shape
    nc_c, nc_x = ctx_len // CHUNK, (t - ctx_len) // CHUNK
    fwd, rev = _chunk_maps(nc_c, nc_x)
    nprob = N_DIR * b * GDN_HEADS
    xs = lambda f: pl.BlockSpec((b, 3 * GDN_HEADS, CHUNK, HEAD_DIM), lambda j: (0, 0, f(j), 0))
    gs = lambda f: pl.BlockSpec((b, CHUNK, GATE_W), lambda j: (0, f(j), 0))
    gts = lambda f: pl.BlockSpec((b, 1, GATE_ROWS, CHUNK), lambda j: (0, f(j), 0, 0))
    hs = lambda f: pl.BlockSpec((b, GDN_HEADS, CHUNK, HEAD_DIM), lambda j: (0, 0, f(j), 0))
    row = pl.BlockSpec((1, GATE_W), lambda j: (0, 0))
    col = pl.BlockSpec((16, CHUNK), lambda j: (0, 0))
    out = jax.ShapeDtypeStruct((b, GDN_HEADS, t, HEAD_DIM), F32)
    return pl.pallas_call(
        functools.partial(_gdn_kernel, nb=b),
        grid=(nc_c + nc_x,),
        in_specs=[xs(fwd), xs(rev), gs(fwd), gs(rev), gts(fwd), gts(rev), row, row, col, col],
        out_specs=(hs(fwd), hs(rev)),
        out_shape=(out, out),
        scratch_shapes=[pltpu.VMEM((nprob, HEAD_DIM, HEAD_DIM), F32),
                        pltpu.VMEM((nprob, CHUNK, 2 * HEAD_DIM), F32)],
        compiler_params=_cparams(("arbitrary",)),
        name="gdn_scan",
    )(gp, gp, gates, gates, gates_t, gates_t, arow, drow, acol, dcol)


def _outproj_kernel(x_ref, mod_ref, ao_ref, mhf_ref, mhr_ref, mo_ref, mg_ref, ghf_ref, ghr_ref, gz_ref, gg_ref,
                    w_ref, n2_ref, rw_ref, x1_ref, hp_ref, aff_ref, mix_sc, *, d_model):
    d = d_model

    def put(j, val):
        mix_sc[:, j * HEAD_DIM:(j + 1) * HEAD_DIM] = val.astype(BF16)

    for h in range(ATTN_HEADS):
        put(h, ao_ref[0, h])
    for h in range(MLSTM_HEADS):
        hh = mhf_ref[0, h] + mhr_ref[0, h]
        hn = hh * lax.rsqrt(jnp.mean(hh * hh, axis=-1, keepdims=True) + EPS) * mg_ref[h:h + 1, :]
        put(ATTN_HEADS + h, _sigmoid(mo_ref[0, h]) * hn)
    for h in range(GDN_HEADS):
        oo = ghf_ref[0, h] + ghr_ref[0, h]
        on = oo * lax.rsqrt(jnp.mean(oo * oo, axis=-1, keepdims=True) + EPS) * gg_ref[...]
        put(ATTN_HEADS + MLSTM_HEADS + h, on * _silu(gz_ref[0, h]))
    x1 = x_ref[0] + mod_ref[:, 2 * d:3 * d] * _dot(mix_sc[...], w_ref[...])
    x1_ref[0] = x1
    xn = x1 * lax.rsqrt(jnp.mean(x1 * x1, axis=-1, keepdims=True) + EPS) * n2_ref[...]
    h2 = xn * (1.0 + mod_ref[:, 4 * d:5 * d]) + mod_ref[:, 3 * d:4 * d]
    logits = _dot_nt(rw_ref[...], h2, precision=HIGHEST)
    e = jnp.exp(logits - jnp.max(logits, axis=0, keepdims=True))
    aff_ref[0] = e / jnp.sum(e, axis=0, keepdims=True)
    hp_ref[0] = h2


def _outproj(xa, modsel, ao, mhf, mhr, mo, mg, ghf, ghr, gz, gg, w_hm, n2, rw_t, tm, blk0):
    b, t, d = xa.shape
    nb = t // tm - blk0
    hm = lambda nh: pl.BlockSpec((1, nh, tm, HEAD_DIM), lambda bi, i: (bi, 0, i + blk0, 0))
    full = lambda a: pl.BlockSpec(a.shape, lambda bi, i: (0,) * a.ndim)
    return pl.pallas_call(
        functools.partial(_outproj_kernel, d_model=d),
        grid=(b, nb),
        in_specs=[
            pl.BlockSpec((1, tm, d), lambda bi, i: (bi, i + blk0, 0)),
            pl.BlockSpec((None, None, 1, N_MOD * d), lambda bi, i: (bi, jnp.minimum(i + blk0, 1), 0, 0)),
            hm(ATTN_HEADS), hm(4), hm(4), hm(4), full(mg), hm(4), hm(4), hm(4), full(gg),
            full(w_hm), full(n2), full(rw_t),
        ],
        out_specs=(pl.BlockSpec((1, tm, d), lambda bi, i: (bi, i + blk0, 0)),
                   pl.BlockSpec((1, tm, d), lambda bi, i: (bi, i + blk0, 0)),
                   pl.BlockSpec((1, N_EXPERTS, tm), lambda bi, i: (bi, 0, i + blk0))),
        out_shape=(jax.ShapeDtypeStruct((b, t, d), F32),
                   jax.ShapeDtypeStruct((b, t, d), F32),
                   jax.ShapeDtypeStruct((b, N_EXPERTS, t), F32)),
        scratch_shapes=[pltpu.VMEM((tm, w_hm.shape[0]), BF16)],
        compiler_params=_cparams(("parallel", "parallel")),
        name="outproj",
    )(xa, modsel, ao, mhf, mhr, mo, mg, ghf, ghr, gz, gg, w_hm, n2, rw_t)


def _route_kernel(aff_ref, idx_ref, val_ref, *, cap, rows):
    ne = N_EXPERTS
    a = aff_ref[0]

    def count(mask):
        return jnp.sum(jnp.sum(mask.astype(I32), axis=2, keepdims=True), axis=1, keepdims=True)

    tau_bits = jnp.zeros((ne, 1, 1), I32)
    for bit in range(30, -1, -1):
        cand = tau_bits | (1 << bit)
        keep = count(a >= lax.bitcast_convert_type(cand, F32)) >= cap
        tau_bits = jnp.where(keep, cand, tau_bits)
    tau = lax.bitcast_convert_type(tau_bits, F32)
    gt = a > tau
    eq = a == tau
    need = cap - count(gt)

    triu = (_iota((LANES, LANES), 0) <= _iota((LANES, LANES), 1)).astype(BF16)
    strict_lower = (_iota((rows, rows), 0) > _iota((rows, rows), 1)).astype(BF16)
    triu_r = (_iota((rows, rows), 0) <= _iota((rows, rows), 1)).astype(BF16)
    ones_r = jnp.ones((SUBLANES, LANES), BF16)

    def prefix(mask2d):
        m = mask2d.astype(BF16)
        within = _dot(m, triu)
        tot = jnp.broadcast_to(within[:, LANES - 1:LANES], (rows, LANES)).astype(BF16)
        return within, _dot(strict_lower, tot)

    lane_r = _iota((cap, rows), 1).astype(F32)
    lane_l = _iota((cap, LANES), 1).astype(F32)
    slot = _iota((cap, 1), 0).astype(F32)
    for e in range(ne):
        eq_e = eq[e]
        w_eq, before_eq = prefix(eq_e)
        rank_eq = w_eq - eq_e.astype(F32) + before_eq
        sel = gt[e] | (eq_e & (rank_eq < need[e].astype(F32)))
        rel, _ = prefix(sel)
        sel_b = sel.astype(BF16)
        row_tot = _dot_nt(ones_r, sel_b)
        row_incl = _dot(row_tot.astype(BF16), triu_r)
        row_excl = row_incl - row_tot
        kstar = jnp.sum((row_incl[0:1, :] <= slot).astype(F32), axis=-1, keepdims=True)
        onehot = (lane_r == kstar).astype(F32)
        base = jnp.sum(onehot * row_excl[0:1, :], axis=-1, keepdims=True)
        g_rel = _dot(onehot.astype(BF16), rel.astype(BF16))
        within = jnp.sum((g_rel <= slot - base).astype(F32), axis=-1, keepdims=True)
        g_aff = _dot(onehot, a[e], precision=HIGHEST)
        val_ref[0, e] = jnp.sum(jnp.where(lane_l == within, g_aff, 0.0), axis=-1, keepdims=True)
        idx_ref[0, e] = (kstar * LANES + within).astype(I32)


def _route(aff_tiles, cap):
    b, ne, rows, _ = aff_tiles.shape
    return pl.pallas_call(
        functools.partial(_route_kernel, cap=cap, rows=rows),
        grid=(b,),
        in_specs=[pl.BlockSpec((1, ne, rows, LANES), lambda bi: (bi, 0, 0, 0))],
        out_specs=(pl.BlockSpec((1, ne, cap, 1), lambda bi: (bi, 0, 0, 0)),
                   pl.BlockSpec((1, ne, cap, 1), lambda bi: (bi, 0, 0, 0))),
        out_shape=(jax.ShapeDtypeStruct((b, ne, cap, 1), I32), jax.ShapeDtypeStruct((b, ne, cap, 1), F32)),
        compiler_params=_cparams(("parallel",)),
        name="route",
    )(aff_tiles)


def _gather_kernel(idx_ref, h_ref, o_ref, rows_sc, *, cap, row_off):
    base = (pl.program_id(0) * pl.num_programs(2) + pl.program_id(2)) * cap

    def body(s, carry):
        t = idx_ref[base + s] + row_off
        rows_sc[pl.ds(s, 1), :] = h_ref[0, pl.ds(t, 1), :]
        return carry

    lax.fori_loop(0, cap, body, 0, unroll=8)
    o_ref[0, 0] = rows_sc[...].astype(BF16)


def _gather(idx_flat, h2, cap, rows_block, row_off):
    b, _, d = h2.shape
    dh = d // 2
    return pl.pallas_call(
        functools.partial(_gather_kernel, cap=cap, row_off=row_off),
        grid_spec=pltpu.PrefetchScalarGridSpec(
            num_scalar_prefetch=1,
            grid=(b, 2, N_EXPERTS),
            in_specs=[pl.BlockSpec((1, rows_block, dh), lambda bi, c, e, idx: (bi, 0, c))],
            out_specs=pl.BlockSpec((1, 1, cap, dh), lambda bi, c, e, idx: (e, bi, 0, c)),
            scratch_shapes=[pltpu.VMEM((cap, dh), F32)],
        ),
        out_shape=jax.ShapeDtypeStruct((N_EXPERTS, b, cap, d), BF16),
        compiler_params=_cparams(("arbitrary", "arbitrary", "arbitrary")),
        name="moe_gather",
    )(idx_flat, h2)


def _ffn_kernel(*refs, n_streams):
    xg_refs = refs[0:n_streams]
    val_refs = refs[n_streams:2 * n_streams]
    w1_ref, w3_ref, w2_ref = refs[2 * n_streams:2 * n_streams + 3]
    y_refs = refs[2 * n_streams + 3:]
    f = pl.program_id(1)
    w1 = w1_ref[0].astype(BF16)
    w3 = w3_ref[0].astype(BF16)
    w2 = w2_ref[0].astype(BF16)
    for xg_ref, val_ref, y_ref in zip(xg_refs, val_refs, y_refs):
        xg = xg_ref[0].reshape(-1, w1.shape[0])
        y = _dot((_silu(_dot(xg, w1)) * _dot(xg, w3)).astype(BF16), w2)

        @pl.when(f == 0)
        def _():
            y_ref[0] = y

        @pl.when(f > 0)
        def _():
            y_ref[0] = y_ref[0] + y

        @pl.when(f == pl.num_programs(1) - 1)
        def _():
            y_ref[0] = y_ref[0] * val_ref[0]


def _ffn(xgs, vals, w1, w3, w2, tf):
    ne, d, ff = w1.shape
    n_streams = len(xgs)
    in_specs, out_specs, out_shapes = [], [], []
    for xg in xgs:
        in_specs.append(pl.BlockSpec((1,) + xg.shape[1:], lambda e, f: (e, 0, 0, 0)))
    for v in vals:
        in_specs.append(pl.BlockSpec((1,) + v.shape[1:], lambda e, f: (e, 0, 0)))
    in_specs += [pl.BlockSpec((1, d, tf), lambda e, f: (e, 0, f)),
                 pl.BlockSpec((1, d, tf), lambda e, f: (e, 0, f)),
                 pl.BlockSpec((1, tf, d), lambda e, f: (e, f, 0))]
    for xg in xgs:
        m = xg.shape[1] * xg.shape[2]
        out_specs.append(pl.BlockSpec((1, m, d), lambda e, f: (e, 0, 0)))
        out_shapes.append(jax.ShapeDtypeStruct((ne, m, d), F32))
    return pl.pallas_call(
        functools.partial(_ffn_kernel, n_streams=n_streams),
        grid=(ne, ff // tf),
        in_specs=in_specs,
        out_specs=tuple(out_specs),
        out_shape=tuple(out_shapes),
        compiler_params=_cparams(("parallel", "arbitrary")),
        name="moe_ffn",
    )(*xgs, *vals, w1, w3, w2)


def _combine_kernel(idx_ref, y_ref, o_ref, *, cap):
    e = pl.program_id(2)

    @pl.when(e == 0)
    def _():
        o_ref[...] = jnp.zeros(o_ref.shape, F32)

    base = (pl.program_id(0) * pl.num_programs(2) + e) * cap

    def body(s, carry):
        t = idx_ref[base + s]
        o_ref[0, pl.ds(t, 1), :] = o_ref[0, pl.ds(t, 1), :] + y_ref[0, pl.ds(s, 1), :]
        return carry

    lax.fori_loop(0, cap, body, 0, unroll=8)


def _combine(idx_flat, y, b, n_tok, cap):
    ne, _, d = y.shape
    dh = d // 2
    return pl.pallas_call(
        functools.partial(_combine_kernel, cap=cap),
        grid_spec=pltpu.PrefetchScalarGridSpec(
            num_scalar_prefetch=1,
            grid=(b, 2, ne),
            in_specs=[pl.BlockSpec((1, cap, dh), lambda bi, c, e, idx: (e, bi, c))],
            out_specs=pl.BlockSpec((1, n_tok, dh), lambda bi, c, e, idx: (bi, 0, c)),
        ),
        out_shape=jax.ShapeDtypeStruct((b, n_tok, d), F32),
        compiler_params=_cparams(("arbitrary", "arbitrary", "arbitrary")),
        name="moe_combine",
    )(idx_flat, y)


def _residual_kernel(x1_ref, mod_ref, mx_ref, mc_ref, o_ref, *, d_model, blk0):
    i = pl.program_id(1) + blk0
    moe = jnp.where(i == 0, mc_ref[0], mx_ref[0])
    o_ref[0] = x1_ref[0] + mod_ref[:, 5 * d_model:6 * d_model] * moe


def _residual(x1, modsel, moe_x, moe_c, tm, blk0):
    b, t, d = x1.shape
    nb = t // tm - blk0
    return pl.pallas_call(
        functools.partial(_residual_kernel, d_model=d, blk0=blk0),
        grid=(b, nb),
        in_specs=[pl.BlockSpec((1, tm, d), lambda bi, i: (bi, i + blk0, 0)),
                  pl.BlockSpec((None, None, 1, N_MOD * d), lambda bi, i: (bi, jnp.minimum(i + blk0, 1), 0, 0)),
                  pl.BlockSpec((1, tm, d), lambda bi, i: (bi, jnp.maximum(i + blk0 - 1, 0), 0)),
                  pl.BlockSpec((1, tm, d), lambda bi, i: (bi, 0, 0))],
        out_specs=pl.BlockSpec((1, tm, d), lambda bi, i: (bi, i, 0)),
        out_shape=jax.ShapeDtypeStruct((b, nb * tm, d), F32),
        compiler_params=_cparams(("parallel", "parallel")),
        name="moe_residual",
    )(x1, modsel, moe_x, moe_c)


def _rope_tables(ctx_len, seq):
    n = jnp.arange(seq)
    pos = jnp.stack([n // GRID_W, n % GRID_W], axis=-1).astype(F32)
    lane = jnp.arange(LANES) % HEAD_DIM
    axis = lane // 32
    n_freq = HEAD_DIM // 4
    inv = ROPE_THETA ** (-(lane % n_freq).astype(F32) / n_freq)
    ang = pos[:, axis] * inv
    sign = jnp.where((lane % 32) < 16, -1.0, 1.0)
    cos_t = jnp.concatenate([jnp.ones((ctx_len, LANES), F32), jnp.cos(ang)], axis=0)
    sin_t = jnp.concatenate([jnp.zeros((ctx_len, LANES), F32), jnp.sin(ang) * sign], axis=0)
    return cos_t, sin_t


def _reorder_w_in(w):
    d = w.shape[0]
    qk = w[:, :ATTN_W + KV_W]
    v = w[:, ATTN_W + KV_W:ATTN_W + 2 * KV_W]
    o = ATTN_W + 2 * KV_W
    ml = w[:, o:o + 4 * MLSTM_W]
    o += 4 * MLSTM_W
    mgate = w[:, o:o + 2 * N_DIR * MLSTM_HEADS]
    o += 2 * N_DIR * MLSTM_HEADS
    gd = w[:, o:o + 4 * GDN_W]
    o += 4 * GDN_W
    ggate = w[:, o:o + 2 * N_DIR * GDN_HEADS]
    pad = jnp.zeros((d, GATE_W - mgate.shape[1] - ggate.shape[1]), w.dtype)
    w_r = jnp.concatenate([qk, ml, gd, mgate, ggate, pad], axis=1)
    w_t = jnp.concatenate([v, ml[:, MLSTM_W:2 * MLSTM_W], mgate, ggate], axis=1).T
    return w_r.astype(BF16), w_t.astype(BF16)


def _gate_row(vals, col0):
    flat = vals.reshape(-1).astype(F32)
    return jnp.zeros((1, GATE_W), F32).at[0, col0:col0 + flat.shape[0]].set(flat)


def _gate_col(first, second):
    flat = jnp.concatenate([first.reshape(-1), second.reshape(-1)]).astype(F32)
    return jnp.broadcast_to(flat[:, None], (flat.shape[0], CHUNK))


def kernel(x, c, ctx, c_ctx, mod_w, mod_b, norm1_g, w_in, q_norm_g, k_norm_g, mlstm_i_bias, mlstm_f_bias,
           mlstm_out_g, gdn_conv_w, gdn_a_log, gdn_dt_bias, gdn_out_g, w_out, norm2_g, router_w, w1, w3, w2):
    b, seq, d = x.shape
    ctx_len = ctx.shape[1]
    depth = mod_w.shape[0]
    tm = ctx_len
    t = ctx_len + seq
    ne = N_EXPERTS
    cap_x = CAPACITY_FACTOR * seq // ne
    cap_c = CAPACITY_FACTOR * ctx_len // ne
    rows_c = 2 * SUBLANES

    cvec = jnp.concatenate([c, c_ctx[None, :], jnp.zeros((SUBLANES - b - 1, d), F32)], axis=0)
    mod = _modulation(cvec, mod_w, mod_b)
    cos_t, sin_t = _rope_tables(ctx_len, seq)
    bd = jnp.kron(jnp.eye(LANES // HEAD_DIM, dtype=F32), jnp.full((HEAD_DIM, HEAD_DIM), 1.0 / HEAD_DIM, F32))
    xa = jnp.concatenate([ctx, x], axis=1)
    zeros8 = jnp.zeros((N_DIR, MLSTM_HEADS), F32)

    for l in range(depth):
        need_ctx = l < depth - 1
        blk0 = 0 if need_ctx else 1
        modsel = jnp.stack([jnp.broadcast_to(mod[l, b], (b, N_MOD * d)), mod[l, :b]], axis=1)[:, :, None, :]
        w_r, w_t = _reorder_w_in(w_in[l])
        qg = jnp.tile(q_norm_g[l], LANES // HEAD_DIM)[None, :]
        kg = jnp.tile(k_norm_g[l], LANES // HEAD_DIM)[None, :]
        q, k, vt, mqkv, mkt, mo, gqkv, gz, gates, gates_t = _inproj(
            xa, modsel, norm1_g[l][None, :], w_r, w_t, cos_t, sin_t, qg, kg, bd, tm)

        ao = _attention(q, k, vt, ctx_len, tm, blk0)

        brow = _gate_row(mlstm_i_bias[l], GC_MI) + _gate_row(mlstm_f_bias[l], GC_MF)
        bcol = _gate_col(mlstm_i_bias[l], mlstm_f_bias[l])
        mhf, mhr = _mlstm_scan(mqkv, mkt, gates, gates_t, brow, bcol, ctx_len)

        conv12 = gdn_conv_w[l].reshape(CONV_K, 3 * GDN_HEADS, HEAD_DIM).transpose(1, 0, 2)
        gp = _gdn_prep(gqkv, conv12, ctx_len)
        neg_a = -jnp.exp(gdn_a_log[l].astype(F32))
        ghf, ghr = _gdn_scan(gp, gates, gates_t, _gate_row(neg_a, GC_GA), _gate_row(gdn_dt_bias[l], GC_GA),
                             _gate_col(neg_a, zeros8), _gate_col(gdn_dt_bias[l], zeros8), ctx_len)

        w_hm = w_out[l].astype(BF16)
        x1, hp, aff = _outproj(xa, modsel, ao, mhf, mhr, mo, mlstm_out_g[l], ghf, ghr, gz, gdn_out_g[l][None, :],
                               w_hm, norm2_g[l][None, :], router_w[l].T, tm, blk0)

        idx_x, val_x = _route(aff[:, :, ctx_len:].reshape(b, ne, seq // LANES, LANES), cap_x)
        idx_xf = idx_x.reshape(-1)
        xgs = [_gather(idx_xf, hp, cap_x, t, ctx_len)]
        vals = [val_x.transpose(1, 0, 2, 3).reshape(ne, b * cap_x, 1)]
        if need_ctx:
            aff_c = jnp.pad(aff[:, :, :ctx_len], ((0, 0), (0, 0), (0, rows_c * LANES - ctx_len)), constant_values=-1.0)
            idx_c, val_c = _route(aff_c.reshape(b, ne, rows_c, LANES), cap_c)
            idx_cf = idx_c.reshape(-1)
            xgs.append(_gather(idx_cf, hp, cap_c, ctx_len, 0))
            vals.append(val_c.transpose(1, 0, 2, 3).reshape(ne, b * cap_c, 1))
        ys = _ffn(xgs, vals, w1[l], w3[l], w2[l], min(FFN_TF, w1.shape[3]))
        moe_x = _combine(idx_xf, ys[0], b, seq, cap_x)
        moe_c = _combine(idx_cf, ys[1], b, ctx_len, cap_c) if need_ctx else moe_x
        xa = _residual(x1, modsel, moe_x, moe_c, tm, blk0)
    return xa
```

```python
import functools
import math

import jax
import jax.numpy as jnp
from jax import lax
from jax.experimental import pallas as pl
from jax.experimental.pallas import tpu as pltpu

F32 = jnp.float32
BF16 = jnp.bfloat16
I32 = jnp.int32
U32 = jnp.uint32
HIGHEST = lax.Precision.HIGHEST

HEAD_DIM = 64
ATTN_HEADS = 8
ATTN_KV_HEADS = 2
ATTN_REP = ATTN_HEADS // ATTN_KV_HEADS
MLSTM_HEADS = 4
GDN_HEADS = 4
N_DIR = 2
CHUNK = 64
CONV_K = 5
GRID_W = 64
ROPE_THETA = 10000.0
N_EXPERTS = 16
CAPACITY_FACTOR = 2
N_MOD = 6
EPS = 1e-6
ATTN_W = ATTN_HEADS * HEAD_DIM
KV_W = ATTN_KV_HEADS * HEAD_DIM
MLSTM_W = MLSTM_HEADS * HEAD_DIM
GDN_W = GDN_HEADS * HEAD_DIM
LANES = 128
SUBLANES = 8
GATE_W = LANES
GATE_ROWS = 32
GC_MI, GC_MF, GC_GA, GC_GB = 0, 8, 16, 24
VMEM_LIMIT = 56 * 1024 * 1024
ATTN_COLS = 256
ATTN_VPAD = 16
LOG2E = 1.4426950408889634
FFN_TF = 256
CONV_ROWS = 256
PAD_ROWS = 8
COMBINE_GROUP = 8


def _cparams(sem):
    return pltpu.CompilerParams(dimension_semantics=sem, vmem_limit_bytes=VMEM_LIMIT)


def _sigmoid(x):
    return 1.0 / (1.0 + jnp.exp(-x))


def _silu(x):
    return x * _sigmoid(x)


def _log_sigmoid(x):
    return jnp.minimum(x, 0.0) - jnp.log1p(jnp.exp(-jnp.abs(x)))


def _softplus(x):
    return jnp.maximum(x, 0.0) + jnp.log1p(jnp.exp(-jnp.abs(x)))


def _dot(a, b, precision=None):
    return jnp.dot(a, b, preferred_element_type=F32, precision=precision)


def _dot_nt(a, b, precision=None):
    return lax.dot_general(a, b, (((1,), (1,)), ((), ())), preferred_element_type=F32, precision=precision)


def _split_bf16(a):
    hi = a.astype(BF16)
    lo = (a - hi.astype(F32)).astype(BF16)
    return hi, lo


def _split_bf16_3(a):
    hi = a.astype(BF16)
    r = a - hi.astype(F32)
    mid = r.astype(BF16)
    return hi, mid, (r - mid.astype(F32)).astype(BF16)


def _dot3(a, b):
    ah, al = _split_bf16(a)
    bh, bl = _split_bf16(b)
    return _dot(ah, bh) + (_dot(ah, bl) + _dot(al, bh))


def _iota(shape, dim):
    return lax.broadcasted_iota(I32, shape, dim)


def _eye_rows(rows, cols, first):
    return (_iota((rows, cols), 0) + first == _iota((rows, cols), 1)).astype(F32)


def _mod_kernel(c_ref, w_ref, b_ref, o_ref):
    s = _silu(c_ref[...])
    o_ref[0] = _dot(s, w_ref[0], precision=HIGHEST) + b_ref[0]


def _modulation(cvec, mod_w, mod_b):
    depth, d, n = mod_w.shape
    rows = cvec.shape[0]
    tn = d
    return pl.pallas_call(
        _mod_kernel,
        grid=(depth, n // tn),
        in_specs=[
            pl.BlockSpec((rows, d), lambda l, j: (0, 0)),
            pl.BlockSpec((1, d, tn), lambda l, j: (l, 0, j)),
            pl.BlockSpec((1, 1, tn), lambda l, j: (l, 0, j)),
        ],
        out_specs=pl.BlockSpec((1, rows, tn), lambda l, j: (l, 0, j)),
        out_shape=jax.ShapeDtypeStruct((depth, rows, n), F32),
        compiler_params=_cparams(("parallel", "parallel")),
        name="modulation",
    )(cvec, mod_w, mod_b.reshape(depth, 1, n))


def _inproj_kernel(x_ref, mod_ref, g_ref, w_ref, wt_ref, cos_ref, sin_ref, qg_ref, kg_ref, bd_ref,
                   q_ref, k_ref, vt_ref, ml_ref, mkt_ref, mo_ref, gd_ref, gz_ref, gate_ref, gatet_ref, *, d_model):
    x = x_ref[0]
    tm = x.shape[0]
    sh = mod_ref[:, 0:d_model]
    sc = mod_ref[:, d_model:2 * d_model]
    xn = x * lax.rsqrt(jnp.mean(x * x, axis=-1, keepdims=True) + EPS) * g_ref[...]
    h = (xn * (1.0 + sc) + sh).astype(BF16)
    p = _dot(h, w_ref[...])
    pt = _dot_nt(wt_ref[...], h)
    for g in range(ATTN_KV_HEADS):
        vt_ref[0, g, 0:HEAD_DIM, :] = pt[g * HEAD_DIM:(g + 1) * HEAD_DIM].astype(BF16)
        vt_ref[0, g, HEAD_DIM:, :] = (_iota((ATTN_VPAD, tm), 0) == 0).astype(BF16)
    for c in range(tm // CHUNK):
        cs = slice(c * CHUNK, (c + 1) * CHUNK)
        for j in range(MLSTM_HEADS):
            mkt_ref[0, j, c] = pt[KV_W + j * HEAD_DIM:KV_W + (j + 1) * HEAD_DIM, cs] * HEAD_DIM ** -0.5
        gatet_ref[0, c] = pt[KV_W + MLSTM_W:KV_W + MLSTM_W + GATE_ROWS, cs]

    cos = cos_ref[...]
    sin = sin_ref[...]
    first_half = (_iota(cos.shape, 1) % 32) < 16

    def norm_rope(xs, g, scale):
        ms = _dot(xs * xs, bd_ref[...], precision=HIGHEST)
        xn_ = xs * lax.rsqrt(ms + EPS) * g
        sw = jnp.where(first_half, pltpu.roll(xn_, LANES - 16, 1), pltpu.roll(xn_, 16, 1))
        return (xn_ * cos + sw * sin) * scale

    for j in range(ATTN_W // LANES):
        qs = norm_rope(p[:, j * LANES:(j + 1) * LANES], qg_ref[...], LOG2E * HEAD_DIM ** -0.5).astype(BF16)
        q_ref[0, 2 * j] = qs[:, 0:HEAD_DIM]
        q_ref[0, 2 * j + 1] = qs[:, HEAD_DIM:LANES]
    ks = norm_rope(p[:, ATTN_W:ATTN_W + KV_W], kg_ref[...], 1.0).astype(BF16)
    k_ref[0, 0] = ks[:, 0:HEAD_DIM]
    k_ref[0, 1] = ks[:, HEAD_DIM:LANES]

    off = ATTN_W + KV_W

    def head(j):
        return p[:, off + j * HEAD_DIM: off + (j + 1) * HEAD_DIM]

    for j in range(12):
        ml_ref[0, j] = head(j) * HEAD_DIM ** -0.5 if 4 <= j < 8 else head(j)
    for j in range(4):
        mo_ref[0, j] = head(12 + j)
    off += 4 * MLSTM_W
    for j in range(12):
        gd_ref[0, j] = head(j)
    for j in range(4):
        gz_ref[0, j] = head(12 + j)
    off += 4 * GDN_W
    gate_ref[0] = p[:, off:off + GATE_W]


def _inproj(xa, modsel, g1, w_r, w_t, cos_t, sin_t, qg, kg, bd, tm):
    b, t, d = xa.shape
    nb = t // tm
    nc = tm // CHUNK
    kern = functools.partial(_inproj_kernel, d_model=d)
    hm_shape = lambda nh, dt: jax.ShapeDtypeStruct((b, nh, t, HEAD_DIM), dt)
    out_shapes = (hm_shape(ATTN_HEADS, BF16), hm_shape(ATTN_KV_HEADS, BF16),
                  jax.ShapeDtypeStruct((b, ATTN_KV_HEADS, HEAD_DIM + ATTN_VPAD, t), BF16),
                  hm_shape(12, F32), jax.ShapeDtypeStruct((b, MLSTM_HEADS, t // CHUNK, HEAD_DIM, CHUNK), F32),
                  hm_shape(4, F32), hm_shape(12, F32), hm_shape(4, F32),
                  jax.ShapeDtypeStruct((b, t, GATE_W), F32),
                  jax.ShapeDtypeStruct((b, t // CHUNK, GATE_ROWS, CHUNK), F32))
    hm = lambda nh: pl.BlockSpec((1, nh, tm, HEAD_DIM), lambda bi, i: (bi, 0, i, 0))
    const = lambda a: pl.BlockSpec(a.shape, lambda bi, i: (0,) * a.ndim)
    return pl.pallas_call(
        kern,
        grid=(b, nb),
        in_specs=[
            pl.BlockSpec((1, tm, d), lambda bi, i: (bi, i, 0)),
            pl.BlockSpec((None, None, 1, N_MOD * d), lambda bi, i: (bi, jnp.minimum(i, 1), 0, 0)),
            const(g1), const(w_r), const(w_t),
            pl.BlockSpec((tm, LANES), lambda bi, i: (i, 0)),
            pl.BlockSpec((tm, LANES), lambda bi, i: (i, 0)),
            const(qg), const(kg), const(bd),
        ],
        out_specs=(hm(ATTN_HEADS), hm(ATTN_KV_HEADS),
                   pl.BlockSpec((1, ATTN_KV_HEADS, HEAD_DIM + ATTN_VPAD, tm), lambda bi, i: (bi, 0, 0, i)),
                   hm(12), pl.BlockSpec((1, MLSTM_HEADS, nc, HEAD_DIM, CHUNK), lambda bi, i: (bi, 0, i, 0, 0)),
                   hm(4), hm(12), hm(4),
                   pl.BlockSpec((1, tm, GATE_W), lambda bi, i: (bi, i, 0)),
                   pl.BlockSpec((1, nc, GATE_ROWS, CHUNK), lambda bi, i: (bi, i, 0, 0))),
        out_shape=out_shapes,
        compiler_params=_cparams(("parallel", "parallel")),
        name="inproj",
    )(xa, modsel, g1, w_r, w_t, cos_t, sin_t, qg, kg, bd)


def _attn_kernel(q_ref, k_ref, vt_ref, o_ref, sa_sc, sb_sc, xa_sc, xb_sc, m_sc, acc_sc, *, tq, tk, n_pairs, blk0):
    i = pl.program_id(2) + blk0
    q = q_ref[0].reshape(ATTN_REP * tq, HEAD_DIM)
    cols = [slice(c * ATTN_COLS, (c + 1) * ATTN_COLS) for c in range(ATTN_REP * tq // ATTN_COLS)]

    m_sc[...] = jnp.full(m_sc.shape, -jnp.inf, F32)
    acc_sc[...] = jnp.zeros(acc_sc.shape, F32)

    def score(s_ref, x_ref, tile):
        start = pl.multiple_of(tile * tk, tk)
        k = k_ref[0, 0, pl.ds(start, tk), :]
        for cs in cols:
            s = _dot_nt(k, q[cs])
            s_ref[:, cs] = s
            x_ref[:, cs] = jnp.max(s, axis=0, keepdims=True)

    def consume(s_ref, x_ref, tile, half):
        start = pl.multiple_of(tile * tk, tk)
        vt = vt_ref[0, 0, :, pl.ds(start, tk)]
        for cs in cols:
            mo = m_sc[half, :, cs]
            mn = jnp.maximum(mo, x_ref[:, cs])
            p = jnp.exp2(s_ref[:, cs] - mn)
            acc_sc[half, :, cs] = jnp.exp2(mo - mn) * acc_sc[half, :, cs] + _dot(vt, p.astype(BF16))
            m_sc[half, :, cs] = mn

    score(sa_sc, xa_sc, 0)

    def body(j, carry):
        score(sb_sc, xb_sc, 2 * j + 1)
        consume(sa_sc, xa_sc, 2 * j, 0)
        score(sa_sc, xa_sc, 2 * j + 2)
        consume(sb_sc, xb_sc, 2 * j + 1, 1)
        return carry

    n = jnp.where(i == 0, 0, n_pairs)
    lax.fori_loop(0, n, body, 0)
    consume(sa_sc, xa_sc, 2 * n, 0)
    m = jnp.maximum(m_sc[0], m_sc[1])
    acc = jnp.exp2(m_sc[0] - m) * acc_sc[0] + jnp.exp2(m_sc[1] - m) * acc_sc[1]
    o_t = (acc[0:HEAD_DIM, :] / acc[HEAD_DIM:HEAD_DIM + 1, :]).astype(BF16)
    eye = (_iota((tq, tq), 0) == _iota((tq, tq), 1)).astype(BF16)
    for r in range(ATTN_REP):
        o_ref[0, r] = _dot_nt(eye, o_t[:, r * tq:(r + 1) * tq]).astype(o_ref.dtype)


def _attention(q, k, vt, ctx_len, tq, blk0):
    b, _, t, _ = q.shape
    tk = ctx_len
    assert (t // tk) % 2 == 1
    nq = t // tq - blk0
    kern = functools.partial(_attn_kernel, tq=tq, tk=tk, n_pairs=(t // tk - 1) // 2, blk0=blk0)
    rows = ATTN_REP * tq
    return pl.pallas_call(
        kern,
        grid=(b, ATTN_KV_HEADS, nq),
        in_specs=[
            pl.BlockSpec((1, ATTN_REP, tq, HEAD_DIM), lambda bi, g, i: (bi, g, i + blk0, 0)),
            pl.BlockSpec((1, 1, t, HEAD_DIM), lambda bi, g, i: (bi, g, 0, 0)),
            pl.BlockSpec((1, 1, HEAD_DIM + ATTN_VPAD, t), lambda bi, g, i: (bi, g, 0, 0)),
        ],
        out_specs=pl.BlockSpec((1, ATTN_REP, tq, HEAD_DIM), lambda bi, g, i: (bi, g, i + blk0, 0)),
        out_shape=jax.ShapeDtypeStruct((b, ATTN_HEADS, t, HEAD_DIM), BF16),
        scratch_shapes=[pltpu.VMEM((tk, rows), F32), pltpu.VMEM((tk, rows), F32),
                        pltpu.VMEM((1, rows), F32), pltpu.VMEM((1, rows), F32),
                        pltpu.VMEM((2, 1, rows), F32), pltpu.VMEM((2, HEAD_DIM + ATTN_VPAD, rows), F32)],
        compiler_params=_cparams(("parallel", "parallel", "arbitrary")),
        name="attention",
    )(q, k, vt)


def _chunk_maps(nc_c, nc_x):
    fwd = lambda j: j
    rev = lambda j: jnp.where(j < nc_c, nc_c - 1 - j, 2 * nc_c + nc_x - 1 - j)
    return fwd, rev


def _dir_masks(d):
    r = _iota((CHUNK, CHUNK), 0)
    c = _iota((CHUNK, CHUNK), 1)
    incl = (r >= c) if d == 0 else (r <= c)
    strict = (r > c) if d == 0 else (r < c)
    incl_t = (r <= c) if d == 0 else (r >= c)
    return incl, strict, incl.astype(F32), incl_t.astype(F32)


def _mlstm_kernel(xf_ref, xr_ref, ktf_ref, ktr_ref, gf_ref, gr_ref, gtf_ref, gtr_ref, brow_ref, bcol_ref,
                  hf_ref, hr_ref, c_sc, n_sc, m_sc, *, nb):
    @pl.when(pl.program_id(0) == 0)
    def _():
        c_sc[...] = jnp.zeros(c_sc.shape, F32)
        n_sc[...] = jnp.zeros(n_sc.shape, F32)
        m_sc[...] = jnp.zeros(m_sc.shape, F32)

    refs = ((xf_ref, ktf_ref, gf_ref, gtf_ref, hf_ref), (xr_ref, ktr_ref, gr_ref, gtr_ref, hr_ref))
    masks = [_dir_masks(d) for d in range(N_DIR)]
    gate = {}
    for d in range(N_DIR):
        _, _, g_ref, gt_ref, _ = refs[d]
        _, _, tri, tri_t = masks[d]
        sel = (_iota((GATE_W, MLSTM_HEADS * LANES), 0)
               == GC_MF + d * MLSTM_HEADS + _iota((GATE_W, MLSTM_HEADS * LANES), 1) // LANES).astype(BF16)
        for b in range(nb):
            cum = _dot(tri, _log_sigmoid(g_ref[b] + brow_ref[...]), precision=HIGHEST)
            g_t = gt_ref[b, 0, GC_MI:GC_MI + 16, :] + bcol_ref[...]
            lf_t = _log_sigmoid(g_t)
            cum_rep = sum(_dot(piece, sel) for piece in _split_bf16_3(cum))
            gate[d, b] = (cum_rep, g_t, lf_t, _dot(lf_t, tri_t, precision=HIGHEST))

    probs = [(d, b, h) for d in range(N_DIR) for b in range(nb) for h in range(MLSTM_HEADS)]
    st = []
    for d, b, h in probs:
        p = (d * nb + b) * MLSTM_HEADS + h
        x_ref, kt_ref = refs[d][0], refs[d][1]
        cum_rep, g_t, lf_t, cum_t = gate[d, b]
        ri = d * MLSTM_HEADS + h
        rf = GC_MF + ri
        bcum_col = cum_rep[:, h * LANES:h * LANES + CHUNK]
        bcum_row = cum_t[rf - GC_MI:rf - GC_MI + 1, :]
        i_row = g_t[ri:ri + 1, :]
        b_last = jnp.sum(lf_t[rf - GC_MI:rf - GC_MI + 1, :], axis=-1, keepdims=True)
        m_old = m_sc[p, 0:1, 0:1]
        w_end = b_last - bcum_row + i_row
        m_new = jnp.maximum(b_last + m_old, jnp.max(w_end, axis=-1, keepdims=True))
        a_row = jnp.exp(w_end - m_new)
        dec = jnp.exp(b_last + m_old - m_new)
        dmat = jnp.where(masks[d][0], bcum_col - bcum_row + i_row, -jnp.inf)
        inter = bcum_col + m_sc[p, 0:1, 0:CHUNK]
        m_t = jnp.maximum(inter, jnp.max(dmat, axis=-1, keepdims=True))
        st.append(dict(p=p, q=x_ref[b, h], k=x_ref[b, MLSTM_HEADS + h], v=x_ref[b, 2 * MLSTM_HEADS + h],
                       k_t=kt_ref[b, h, 0], c_old=c_sc[p], n_old=n_sc[p], a_row=a_row, dec=dec, m_new=m_new,
                       m_t=m_t, w_in=jnp.exp(inter - m_t), dexp=jnp.exp(dmat - m_t), out=refs[d][4], b=b, h=h))
    for e in st:
        e["s"] = _dot(e["q"], e["k_t"]) * e["dexp"]
    for e in st:
        e["qc"] = _dot(e["q"], e["c_old"])
    for e in st:
        e["sv"] = _dot(e["s"], e["v"])
    for e in st:
        e["kv"] = _dot(e["k_t"] * e["a_row"], e["v"])
    for e in st:
        e["ak"] = _dot(jnp.broadcast_to(e["a_row"], (SUBLANES, CHUNK)), e["k"])
    for e in st:
        num = e["w_in"] * e["qc"] + e["sv"]
        den = (e["w_in"] * jnp.sum(e["q"] * e["n_old"][0:1, :], axis=-1, keepdims=True)
               + jnp.sum(e["s"], axis=-1, keepdims=True))
        e["out"][e["b"], e["h"]] = num / jnp.maximum(jnp.abs(den), jnp.exp(-e["m_t"]))
    for e in st:
        p = e["p"]
        c_sc[p] = e["dec"] * e["c_old"] + e["kv"]
        n_sc[p] = e["dec"] * e["n_old"] + e["ak"]
        m_sc[p] = jnp.broadcast_to(e["m_new"], (SUBLANES, LANES))


def _mlstm_scan(mqkv, k_t, gates, gates_t, brow, bcol, ctx_len):
    b, _, t, _ = mqkv.shape
    nc_c, nc_x = ctx_len // CHUNK, (t - ctx_len) // CHUNK
    fwd, rev = _chunk_maps(nc_c, nc_x)
    nprob = N_DIR * b * MLSTM_HEADS
    xs = lambda f: pl.BlockSpec((b, 12, CHUNK, HEAD_DIM), lambda j: (0, 0, f(j), 0))
    ks = lambda f: pl.BlockSpec((b, MLSTM_HEADS, 1, HEAD_DIM, CHUNK), lambda j: (0, 0, f(j), 0, 0))
    gs = lambda f: pl.BlockSpec((b, CHUNK, GATE_W), lambda j: (0, f(j), 0))
    gts = lambda f: pl.BlockSpec((b, 1, GATE_ROWS, CHUNK), lambda j: (0, f(j), 0, 0))
    hs = lambda f: pl.BlockSpec((b, MLSTM_HEADS, CHUNK, HEAD_DIM), lambda j: (0, 0, f(j), 0))
    out = jax.ShapeDtypeStruct((b, MLSTM_HEADS, t, HEAD_DIM), F32)
    return pl.pallas_call(
        functools.partial(_mlstm_kernel, nb=b),
        grid=(nc_c + nc_x,),
        in_specs=[xs(fwd), xs(rev), ks(fwd), ks(rev), gs(fwd), gs(rev), gts(fwd), gts(rev),
                  pl.BlockSpec((1, GATE_W), lambda j: (0, 0)),
                  pl.BlockSpec((16, CHUNK), lambda j: (0, 0))],
        out_specs=(hs(fwd), hs(rev)),
        out_shape=(out, out),
        scratch_shapes=[pltpu.VMEM((nprob, HEAD_DIM, HEAD_DIM), F32),
                        pltpu.VMEM((nprob, SUBLANES, HEAD_DIM), F32),
                        pltpu.VMEM((nprob, SUBLANES, LANES), F32)],
        compiler_params=_cparams(("arbitrary",)),
        name="mlstm_scan",
    )(mqkv, mqkv, k_t, k_t, gates, gates, gates_t, gates_t, brow, bcol)


def _gdn_prep_kernel(x_ref, w_ref, o_ref, pad_sc, *, ctx_len, t):
    part = pl.program_id(1) // GDN_HEADS
    zeros = jnp.zeros((PAD_ROWS, HEAD_DIM), F32)
    pad_sc[0:PAD_ROWS] = zeros
    pad_sc[PAD_ROWS:PAD_ROWS + ctx_len] = x_ref[0, 0, 0:ctx_len]
    pad_sc[PAD_ROWS + ctx_len:2 * PAD_ROWS + ctx_len] = zeros
    pad_sc[2 * PAD_ROWS + ctx_len:2 * PAD_ROWS + t] = x_ref[0, 0, ctx_len:t]
    pad_sc[2 * PAD_ROWS + t:3 * PAD_ROWS + t] = zeros
    w = w_ref[0]
    is_qk = part < 2
    scale = jnp.where(part == 0, HEAD_DIM ** -0.5, 1.0)
    for c in range(t // CONV_ROWS):
        r0 = c * CONV_ROWS
        base = r0 + (PAD_ROWS if r0 < ctx_len else 2 * PAD_ROWS) - CONV_K // 2
        y = w[0:1, :] * pad_sc[base:base + CONV_ROWS]
        for j in range(1, CONV_K):
            y = y + w[j:j + 1, :] * pad_sc[base + j:base + j + CONV_ROWS]
        y = _silu(y)
        yn = y * lax.rsqrt(jnp.sum(y * y, axis=-1, keepdims=True) + EPS) * scale
        o_ref[0, 0, r0:r0 + CONV_ROWS] = jnp.where(is_qk, yn, y)


def _gdn_prep(gqkv, conv_w12, ctx_len):
    b, np_, t, _ = gqkv.shape
    return pl.pallas_call(
        functools.partial(_gdn_prep_kernel, ctx_len=ctx_len, t=t),
        grid=(b, np_),
        in_specs=[pl.BlockSpec((1, 1, t, HEAD_DIM), lambda bi, p: (bi, p, 0, 0)),
                  pl.BlockSpec((1, CONV_K, HEAD_DIM), lambda bi, p: (p, 0, 0))],
        out_specs=pl.BlockSpec((1, 1, t, HEAD_DIM), lambda bi, p: (bi, p, 0, 0)),
        out_shape=jax.ShapeDtypeStruct(gqkv.shape, F32),
        scratch_shapes=[pltpu.VMEM((t + 3 * PAD_ROWS, HEAD_DIM), F32)],
        compiler_params=_cparams(("parallel", "parallel")),
        name="gdn_prep",
    )(gqkv, conv_w12)


def _gdn_kernel(xf_ref, xr_ref, gf_ref, gr_ref, gtf_ref, gtr_ref, arow_ref, drow_ref, acol_ref, dcol_ref,
                of_ref, or_ref, s_sc, rhs_sc, *, nb):
    @pl.when(pl.program_id(0) == 0)
    def _():
        s_sc[...] = jnp.zeros(s_sc.shape, F32)

    refs = ((xf_ref, gf_ref, gtf_ref, of_ref), (xr_ref, gr_ref, gtr_ref, or_ref))
    masks = [_dir_masks(d) for d in range(N_DIR)]
    gate = {}
    for d in range(N_DIR):
        g_ref, gt_ref = refs[d][1], refs[d][2]
        _, _, tri, tri_t = masks[d]
        for b in range(nb):
            g = g_ref[b]
            gval = arow_ref[...] * _softplus(g + drow_ref[...])
            g_t = gt_ref[b, 0, GC_GA:GC_GA + 16, :]
            gval_t = acol_ref[...] * _softplus(g_t + dcol_ref[...])
            gate[d, b] = (_dot(tri, gval, precision=HIGHEST), _sigmoid(g), gval_t,
                          _dot(gval_t, tri_t, precision=HIGHEST))

    probs = [(d, b, h) for d in range(N_DIR) for b in range(nb) for h in range(GDN_HEADS)]
    eye = _eye_rows(HEAD_DIM, HEAD_DIM, 0).astype(BF16)
    st = []
    for d, b, h in probs:
        x_ref = refs[d][0]
        st.append(dict(p=(d * nb + b) * GDN_HEADS + h, d=d, b=b, h=h, q=x_ref[b, h], k=x_ref[b, GDN_HEADS + h],
                       v=x_ref[b, 2 * GDN_HEADS + h]))
    for e in st:
        e["k_t"] = sum(_dot_nt(eye, piece) for piece in _split_bf16(e["k"]))
    for e in st:
        d, b, h, p = e["d"], e["b"], e["h"], e["p"]
        incl, strict, _, _ = masks[d]
        gcum, beta_all, gval_t, gcum_t = gate[d, b]
        ra = d * GDN_HEADS + h
        g_col = gcum[:, GC_GA + ra:GC_GA + ra + 1]
        g_row = gcum_t[ra:ra + 1, :]
        beta = beta_all[:, GC_GB + ra:GC_GB + ra + 1]
        g_last = jnp.sum(gval_t[ra:ra + 1, :], axis=-1, keepdims=True)
        eg = jnp.exp(g_col)
        kb = e["k"] * beta
        rhs_sc[p, :, 0:HEAD_DIM] = e["v"] * beta
        rhs_sc[p, :, HEAD_DIM:] = kb * eg
        e.update(out=refs[d][3], strict=strict, eg_last=jnp.exp(g_last), kb=kb, qg=e["q"] * eg,
                 decay=jnp.where(incl, jnp.exp(jnp.where(incl, g_col - g_row, 0.0)), 0.0),
                 ktg=e["k_t"] * jnp.exp(g_last - g_row), s_old=s_sc[p])
    for e in st:
        e["sol"] = rhs_sc[e["p"]]
    for e in st:
        e["pw"] = -jnp.where(e["strict"], _dot3(e["kb"], e["k_t"]) * e["decay"], 0.0)
    n_fac = CHUNK.bit_length() - 1
    for it in range(n_fac):
        for e in st:
            e["sol"] = e["sol"] + _dot3(e["pw"], e["sol"])
        if it < n_fac - 1:
            for e in st:
                e["pw"] = _dot3(e["pw"], e["pw"])
    for e in st:
        e["ws"] = _dot(e["sol"][:, HEAD_DIM:], e["s_old"])
    for e in st:
        e["qs"] = _dot(e["qg"], e["s_old"])
    for e in st:
        e["qk"] = _dot(e["q"], e["k_t"]) * e["decay"]
    for e in st:
        e["v_new"] = e["sol"][:, :HEAD_DIM] - e["ws"]
    for e in st:
        e["out"][e["b"], e["h"]] = e["qs"] + _dot(e["qk"], e["v_new"])
    for e in st:
        s_sc[e["p"]] = e["eg_last"] * e["s_old"] + _dot(e["ktg"], e["v_new"])


def _gdn_scan(gp, gates, gates_t, arow, drow, acol, dcol, ctx_len):
    b, _, t, _ = gp.shape
    nc_c, nc_x = ctx_len // CHUNK, (t - ctx_len) // CHUNK
    fwd, rev = _chunk_maps(nc_c, nc_x)
    nprob = N_DIR * b * GDN_HEADS
    xs = lambda f: pl.BlockSpec((b, 3 * GDN_HEADS, CHUNK, HEAD_DIM), lambda j: (0, 0, f(j), 0))
    gs = lambda f: pl.BlockSpec((b, CHUNK, GATE_W), lambda j: (0, f(j), 0))
    gts = lambda f: pl.BlockSpec((b, 1, GATE_ROWS, CHUNK), lambda j: (0, f(j), 0, 0))
    hs = lambda f: pl.BlockSpec((b, GDN_HEADS, CHUNK, HEAD_DIM), lambda j: (0, 0, f(j), 0))
    row = pl.BlockSpec((1, GATE_W), lambda j: (0, 0))
    col = pl.BlockSpec((16, CHUNK), lambda j: (0, 0))
    out = jax.ShapeDtypeStruct((b, GDN_HEADS, t, HEAD_DIM), F32)
    return pl.pallas_call(
        functools.partial(_gdn_kernel, nb=b),
        grid=(nc_c + nc_x,),
        in_specs=[xs(fwd), xs(rev), gs(fwd), gs(rev), gts(fwd), gts(rev), row, row, col, col],
        out_specs=(hs(fwd), hs(rev)),
        out_shape=(out, out),
        scratch_shapes=[pltpu.VMEM((nprob, HEAD_DIM, HEAD_DIM), F32),
                        pltpu.VMEM((nprob, CHUNK, 2 * HEAD_DIM), F32)],
        compiler_params=_cparams(("arbitrary",)),
        name="gdn_scan",
    )(gp, gp, gates, gates, gates_t, gates_t, arow, drow, acol, dcol)


def _outproj_kernel(x_ref, mod_ref, ao_ref, mhf_ref, mhr_ref, mo_ref, mg_ref, ghf_ref, ghr_ref, gz_ref, gg_ref,
                    w_ref, n2_ref, rw_ref, x1_ref, hp_ref, aff_ref, mix_sc, *, d_model):
    d = d_model

    def put(j, val):
        mix_sc[:, j * HEAD_DIM:(j + 1) * HEAD_DIM] = val.astype(BF16)

    for h in range(ATTN_HEADS):
        put(h, ao_ref[0, h])
    for h in range(MLSTM_HEADS):
        hh = mhf_ref[0, h] + mhr_ref[0, h]
        hn = hh * lax.rsqrt(jnp.mean(hh * hh, axis=-1, keepdims=True) + EPS) * mg_ref[h:h + 1, :]
        put(ATTN_HEADS + h, _sigmoid(mo_ref[0, h]) * hn)
    for h in range(GDN_HEADS):
        oo = ghf_ref[0, h] + ghr_ref[0, h]
        on = oo * lax.rsqrt(jnp.mean(oo * oo, axis=-1, keepdims=True) + EPS) * gg_ref[...]
        put(ATTN_HEADS + MLSTM_HEADS + h, on * _silu(gz_ref[0, h]))
    x1 = x_ref[0] + mod_ref[:, 2 * d:3 * d] * _dot(mix_sc[...], w_ref[...])
    x1_ref[0] = x1
    xn = x1 * lax.rsqrt(jnp.mean(x1 * x1, axis=-1, keepdims=True) + EPS) * n2_ref[...]
    h2 = xn * (1.0 + mod_ref[:, 4 * d:5 * d]) + mod_ref[:, 3 * d:4 * d]
    logits = _dot_nt(rw_ref[...], h2, precision=HIGHEST)
    e = jnp.exp(logits - jnp.max(logits, axis=0, keepdims=True))
    aff_ref[0] = e / jnp.sum(e, axis=0, keepdims=True)
    hp_ref[0] = h2


def _outproj(xa, modsel, ao, mhf, mhr, mo, mg, ghf, ghr, gz, gg, w_hm, n2, rw_t, tm, blk0):
    b, t, d = xa.shape
    nb = t // tm - blk0
    hm = lambda nh: pl.BlockSpec((1, nh, tm, HEAD_DIM), lambda bi, i: (bi, 0, i + blk0, 0))
    full = lambda a: pl.BlockSpec(a.shape, lambda bi, i: (0,) * a.ndim)
    return pl.pallas_call(
        functools.partial(_outproj_kernel, d_model=d),
        grid=(b, nb),
        in_specs=[
            pl.BlockSpec((1, tm, d), lambda bi, i: (bi, i + blk0, 0)),
            pl.BlockSpec((None, None, 1, N_MOD * d), lambda bi, i: (bi, jnp.minimum(i + blk0, 1), 0, 0)),
            hm(ATTN_HEADS), hm(4), hm(4), hm(4), full(mg), hm(4), hm(4), hm(4), full(gg),
            full(w_hm), full(n2), full(rw_t),
        ],
        out_specs=(pl.BlockSpec((1, tm, d), lambda bi, i: (bi, i + blk0, 0)),
                   pl.BlockSpec((1, tm, d), lambda bi, i: (bi, i + blk0, 0)),
                   pl.BlockSpec((1, N_EXPERTS, tm), lambda bi, i: (bi, 0, i + blk0))),
        out_shape=(jax.ShapeDtypeStruct((b, t, d), F32),
                   jax.ShapeDtypeStruct((b, t, d), F32),
                   jax.ShapeDtypeStruct((b, N_EXPERTS, t), F32)),
        scratch_shapes=[pltpu.VMEM((tm, w_hm.shape[0]), BF16)],
        compiler_params=_cparams(("parallel", "parallel")),
        name="outproj",
    )(xa, modsel, ao, mhf, mhr, mo, mg, ghf, ghr, gz, gg, w_hm, n2, rw_t)


def _route_kernel(aff_ref, idx_ref, val_ref, *, cap, rows):
    ne = N_EXPERTS
    a = aff_ref[0]

    def count(mask):
        return jnp.sum(jnp.sum(mask.astype(I32), axis=2, keepdims=True), axis=1, keepdims=True)

    tau_bits = jnp.zeros((ne, 1, 1), I32)
    for bit in range(30, -1, -1):
        cand = tau_bits | (1 << bit)
        keep = count(a >= lax.bitcast_convert_type(cand, F32)) >= cap
        tau_bits = jnp.where(keep, cand, tau_bits)
    tau = lax.bitcast_convert_type(tau_bits, F32)
    gt = a > tau
    eq = a == tau
    need = cap - count(gt)

    triu = (_iota((LANES, LANES), 0) <= _iota((LANES, LANES), 1)).astype(BF16)
    strict_lower = (_iota((rows, rows), 0) > _iota((rows, rows), 1)).astype(BF16)
    triu_r = (_iota((rows, rows), 0) <= _iota((rows, rows), 1)).astype(BF16)
    ones_r = jnp.ones((SUBLANES, LANES), BF16)

    def prefix(mask2d):
        m = mask2d.astype(BF16)
        within = _dot(m, triu)
        tot = jnp.broadcast_to(within[:, LANES - 1:LANES], (rows, LANES)).astype(BF16)
        return within, _dot(strict_lower, tot)

    lane_r = _iota((cap, rows), 1).astype(F32)
    lane_l = _iota((cap, LANES), 1).astype(F32)
    slot = _iota((cap, 1), 0).astype(F32)
    for e in range(ne):
        eq_e = eq[e]
        w_eq, before_eq = prefix(eq_e)
        rank_eq = w_eq - eq_e.astype(F32) + before_eq
        sel = gt[e] | (eq_e & (rank_eq < need[e].astype(F32)))
        rel, _ = prefix(sel)
        sel_b = sel.astype(BF16)
        row_tot = _dot_nt(ones_r, sel_b)
        row_incl = _dot(row_tot.astype(BF16), triu_r)
        row_excl = row_incl - row_tot
        kstar = jnp.sum((row_incl[0:1, :] <= slot).astype(F32), axis=-1, keepdims=True)
        onehot = (lane_r == kstar).astype(F32)
        base = jnp.sum(onehot * row_excl[0:1, :], axis=-1, keepdims=True)
        g_rel = _dot(onehot.astype(BF16), rel.astype(BF16))
        within = jnp.sum((g_rel <= slot - base).astype(F32), axis=-1, keepdims=True)
        g_aff = _dot(onehot, a[e], precision=HIGHEST)
        val_ref[0, e] = jnp.sum(jnp.where(lane_l == within, g_aff, 0.0), axis=-1, keepdims=True)
        idx_ref[0, e] = (kstar * LANES + within).astype(I32)


def _route(aff_tiles, cap):
    b, ne, rows, _ = aff_tiles.shape
    return pl.pallas_call(
        functools.partial(_route_kernel, cap=cap, rows=rows),
        grid=(b,),
        in_specs=[pl.BlockSpec((1, ne, rows, LANES), lambda bi: (bi, 0, 0, 0))],
        out_specs=(pl.BlockSpec((1, ne, cap, 1), lambda bi: (bi, 0, 0, 0)),
                   pl.BlockSpec((1, ne, cap, 1), lambda bi: (bi, 0, 0, 0))),
        out_shape=(jax.ShapeDtypeStruct((b, ne, cap, 1), I32), jax.ShapeDtypeStruct((b, ne, cap, 1), F32)),
        compiler_params=_cparams(("parallel",)),
        name="route",
    )(aff_tiles)


def _gather_kernel(idx_ref, h_ref, o_ref, rows_sc, *, cap, row_off):
    base = (pl.program_id(0) * pl.num_programs(2) + pl.program_id(2)) * cap

    def body(s, carry):
        t = idx_ref[base + s] + row_off
        rows_sc[pl.ds(s, 1), :] = h_ref[0, pl.ds(t, 1), :]
        return carry

    lax.fori_loop(0, cap, body, 0, unroll=8)
    o_ref[0, 0] = rows_sc[...].astype(BF16)


def _gather(idx_flat, h2, cap, rows_block, row_off):
    b, _, d = h2.shape
    dh = d // 2
    return pl.pallas_call(
        functools.partial(_gather_kernel, cap=cap, row_off=row_off),
        grid_spec=pltpu.PrefetchScalarGridSpec(
            num_scalar_prefetch=1,
            grid=(b, 2, N_EXPERTS),
            in_specs=[pl.BlockSpec((1, rows_block, dh), lambda bi, c, e, idx: (bi, 0, c))],
            out_specs=pl.BlockSpec((1, 1, cap, dh), lambda bi, c, e, idx: (e, bi, 0, c)),
            scratch_shapes=[pltpu.VMEM((cap, dh), F32)],
        ),
        out_shape=jax.ShapeDtypeStruct((N_EXPERTS, b, cap, d), BF16),
        compiler_params=_cparams(("arbitrary", "arbitrary", "arbitrary")),
        name="moe_gather",
    )(idx_flat, h2)


def _ffn_kernel(*refs, n_streams):
    xg_refs = refs[0:n_streams]
    val_refs = refs[n_streams:2 * n_streams]
    w1_ref, w3_ref, w2_ref = refs[2 * n_streams:2 * n_streams + 3]
    y_refs = refs[2 * n_streams + 3:]
    f = pl.program_id(1)
    w1 = w1_ref[0].astype(BF16)
    w3 = w3_ref[0].astype(BF16)
    w2 = w2_ref[0].astype(BF16)
    for xg_ref, val_ref, y_ref in zip(xg_refs, val_refs, y_refs):
        xg = xg_ref[0].reshape(-1, w1.shape[0])
        y = _dot((_silu(_dot(xg, w1)) * _dot(xg, w3)).astype(BF16), w2)

        @pl.when(f == 0)
        def _():
            y_ref[0] = y

        @pl.when(f > 0)
        def _():
            y_ref[0] = y_ref[0] + y

        @pl.when(f == pl.num_programs(1) - 1)
        def _():
            y_ref[0] = y_ref[0] * val_ref[0]


def _ffn(xgs, vals, w1, w3, w2, layer, tf):
    _, ne, d, ff = w1.shape
    n_streams = len(xgs)
    in_specs, out_specs, out_shapes = [], [], []
    for xg in xgs:
        in_specs.append(pl.BlockSpec((1,) + xg.shape[1:], lambda e, f: (e, 0, 0, 0)))
    for v in vals:
        in_specs.append(pl.BlockSpec((1,) + v.shape[1:], lambda e, f: (e, 0, 0)))
    in_specs += [pl.BlockSpec((None, 1, d, tf), lambda e, f: (layer, e, 0, f)),
                 pl.BlockSpec((None, 1, d, tf), lambda e, f: (layer, e, 0, f)),
                 pl.BlockSpec((None, 1, tf, d), lambda e, f: (layer, e, f, 0))]
    for xg in xgs:
        m = xg.shape[1] * xg.shape[2]
        out_specs.append(pl.BlockSpec((1, m, d), lambda e, f: (e, 0, 0)))
        out_shapes.append(jax.ShapeDtypeStruct((ne, m, d), F32))
    return pl.pallas_call(
        functools.partial(_ffn_kernel, n_streams=n_streams),
        grid=(ne, ff // tf),
        in_specs=in_specs,
        out_specs=tuple(out_specs),
        out_shape=tuple(out_shapes),
        compiler_params=_cparams(("parallel", "arbitrary")),
        name="moe_ffn",
    )(*xgs, *vals, w1, w3, w2)


def _combine_kernel(idx_ref, y_ref, o_ref, *, cap):
    e = pl.program_id(2)

    @pl.when(e == 0)
    def _():
        o_ref[...] = jnp.zeros(o_ref.shape, F32)

    base = (pl.program_id(0) * pl.num_programs(2) + e) * cap

    def body(g, carry):
        s0 = pl.multiple_of(g * COMBINE_GROUP, COMBINE_GROUP)
        ts = [idx_ref[base + s0 + r] for r in range(COMBINE_GROUP)]
        ys = y_ref[0, pl.ds(s0, COMBINE_GROUP), :]
        rows = [o_ref[0, pl.ds(t, 1), :] for t in ts]
        for r, t in enumerate(ts):
            o_ref[0, pl.ds(t, 1), :] = rows[r] + ys[r:r + 1, :]
        return carry

    lax.fori_loop(0, cap // COMBINE_GROUP, body, 0)


def _combine(idx_flat, y, b, n_tok, cap):
    ne, _, d = y.shape
    dh = d // 2
    return pl.pallas_call(
        functools.partial(_combine_kernel, cap=cap),
        grid_spec=pltpu.PrefetchScalarGridSpec(
            num_scalar_prefetch=1,
            grid=(b, 2, ne),
            in_specs=[pl.BlockSpec((1, cap, dh), lambda bi, c, e, idx: (e, bi, c))],
            out_specs=pl.BlockSpec((1, n_tok, dh), lambda bi, c, e, idx: (bi, 0, c)),
        ),
        out_shape=jax.ShapeDtypeStruct((b, n_tok, d), F32),
        compiler_params=_cparams(("arbitrary", "arbitrary", "arbitrary")),
        name="moe_combine",
    )(idx_flat, y)


def _residual_kernel(x1_ref, mod_ref, mx_ref, mc_ref, o_ref, *, d_model, blk0):
    i = pl.program_id(1) + blk0
    moe = jnp.where(i == 0, mc_ref[0], mx_ref[0])
    o_ref[0] = x1_ref[0] + mod_ref[:, 5 * d_model:6 * d_model] * moe


def _residual(x1, modsel, moe_x, moe_c, tm, blk0):
    b, t, d = x1.shape
    nb = t // tm - blk0
    return pl.pallas_call(
        functools.partial(_residual_kernel, d_model=d, blk0=blk0),
        grid=(b, nb),
        in_specs=[pl.BlockSpec((1, tm, d), lambda bi, i: (bi, i + blk0, 0)),
                  pl.BlockSpec((None, None, 1, N_MOD * d), lambda bi, i: (bi, jnp.minimum(i + blk0, 1), 0, 0)),
                  pl.BlockSpec((1, tm, d), lambda bi, i: (bi, jnp.maximum(i + blk0 - 1, 0), 0)),
                  pl.BlockSpec((1, tm, d), lambda bi, i: (bi, 0, 0))],
        out_specs=pl.BlockSpec((1, tm, d), lambda bi, i: (bi, i, 0)),
        out_shape=jax.ShapeDtypeStruct((b, nb * tm, d), F32),
        compiler_params=_cparams(("parallel", "parallel")),
        name="moe_residual",
    )(x1, modsel, moe_x, moe_c)


def _rope_tables(ctx_len, seq):
    n = jnp.arange(seq)
    pos = jnp.stack([n // GRID_W, n % GRID_W], axis=-1).astype(F32)
    lane = jnp.arange(LANES) % HEAD_DIM
    axis = lane // 32
    n_freq = HEAD_DIM // 4
    inv = ROPE_THETA ** (-(lane % n_freq).astype(F32) / n_freq)
    ang = pos[:, axis] * inv
    sign = jnp.where((lane % 32) < 16, -1.0, 1.0)
    cos_t = jnp.concatenate([jnp.ones((ctx_len, LANES), F32), jnp.cos(ang)], axis=0)
    sin_t = jnp.concatenate([jnp.zeros((ctx_len, LANES), F32), jnp.sin(ang) * sign], axis=0)
    return cos_t, sin_t


def _reorder_w_in(w):
    d = w.shape[0]
    qk = w[:, :ATTN_W + KV_W]
    v = w[:, ATTN_W + KV_W:ATTN_W + 2 * KV_W]
    o = ATTN_W + 2 * KV_W
    ml = w[:, o:o + 4 * MLSTM_W]
    o += 4 * MLSTM_W
    mgate = w[:, o:o + 2 * N_DIR * MLSTM_HEADS]
    o += 2 * N_DIR * MLSTM_HEADS
    gd = w[:, o:o + 4 * GDN_W]
    o += 4 * GDN_W
    ggate = w[:, o:o + 2 * N_DIR * GDN_HEADS]
    pad = jnp.zeros((d, GATE_W - mgate.shape[1] - ggate.shape[1]), w.dtype)
    w_r = jnp.concatenate([qk, ml, gd, mgate, ggate, pad], axis=1)
    w_t = jnp.concatenate([v, ml[:, MLSTM_W:2 * MLSTM_W], mgate, ggate], axis=1).T
    return w_r.astype(BF16), w_t.astype(BF16)


def _gate_row(vals, col0):
    flat = vals.reshape(-1).astype(F32)
    return jnp.zeros((1, GATE_W), F32).at[0, col0:col0 + flat.shape[0]].set(flat)


def _gate_col(first, second):
    flat = jnp.concatenate([first.reshape(-1), second.reshape(-1)]).astype(F32)
    return jnp.broadcast_to(flat[:, None], (flat.shape[0], CHUNK))


def kernel(x, c, ctx, c_ctx, mod_w, mod_b, norm1_g, w_in, q_norm_g, k_norm_g, mlstm_i_bias, mlstm_f_bias,
           mlstm_out_g, gdn_conv_w, gdn_a_log, gdn_dt_bias, gdn_out_g, w_out, norm2_g, router_w, w1, w3, w2):
    b, seq, d = x.shape
    ctx_len = ctx.shape[1]
    depth = mod_w.shape[0]
    tm = ctx_len
    t = ctx_len + seq
    ne = N_EXPERTS
    cap_x = CAPACITY_FACTOR * seq // ne
    cap_c = CAPACITY_FACTOR * ctx_len // ne
    rows_c = 2 * SUBLANES

    cvec = jnp.concatenate([c, c_ctx[None, :], jnp.zeros((SUBLANES - b - 1, d), F32)], axis=0)
    mod = _modulation(cvec, mod_w, mod_b)
    cos_t, sin_t = _rope_tables(ctx_len, seq)
    bd = jnp.kron(jnp.eye(LANES // HEAD_DIM, dtype=F32), jnp.full((HEAD_DIM, HEAD_DIM), 1.0 / HEAD_DIM, F32))
    xa = jnp.concatenate([ctx, x], axis=1)
    zeros8 = jnp.zeros((N_DIR, MLSTM_HEADS), F32)

    for l in range(depth):
        need_ctx = l < depth - 1
        blk0 = 0 if need_ctx else 1
        modsel = jnp.stack([jnp.broadcast_to(mod[l, b], (b, N_MOD * d)), mod[l, :b]], axis=1)[:, :, None, :]
        w_r, w_t = _reorder_w_in(w_in[l])
        qg = jnp.tile(q_norm_g[l], LANES // HEAD_DIM)[None, :]
        kg = jnp.tile(k_norm_g[l], LANES // HEAD_DIM)[None, :]
        q, k, vt, mqkv, mkt, mo, gqkv, gz, gates, gates_t = _inproj(
            xa, modsel, norm1_g[l][None, :], w_r, w_t, cos_t, sin_t, qg, kg, bd, tm)

        ao = _attention(q, k, vt, ctx_len, tm, blk0)

        brow = _gate_row(mlstm_i_bias[l], GC_MI) + _gate_row(mlstm_f_bias[l], GC_MF)
        bcol = _gate_col(mlstm_i_bias[l], mlstm_f_bias[l])
        mhf, mhr = _mlstm_scan(mqkv, mkt, gates, gates_t, brow, bcol, ctx_len)

        conv12 = gdn_conv_w[l].reshape(CONV_K, 3 * GDN_HEADS, HEAD_DIM).transpose(1, 0, 2)
        gp = _gdn_prep(gqkv, conv12, ctx_len)
        neg_a = -jnp.exp(gdn_a_log[l].astype(F32))
        ghf, ghr = _gdn_scan(gp, gates, gates_t, _gate_row(neg_a, GC_GA), _gate_row(gdn_dt_bias[l], GC_GA),
                             _gate_col(neg_a, zeros8), _gate_col(gdn_dt_bias[l], zeros8), ctx_len)

        w_hm = w_out[l].astype(BF16)
        x1, hp, aff = _outproj(xa, modsel, ao, mhf, mhr, mo, mlstm_out_g[l], ghf, ghr, gz, gdn_out_g[l][None, :],
                               w_hm, norm2_g[l][None, :], router_w[l].T, tm, blk0)

        idx_x, val_x = _route(aff[:, :, ctx_len:].reshape(b, ne, seq // LANES, LANES), cap_x)
        idx_xf = idx_x.reshape(-1)
        xgs = [_gather(idx_xf, hp, cap_x, t, ctx_len)]
        vals = [val_x.transpose(1, 0, 2, 3).reshape(ne, b * cap_x, 1)]
        if need_ctx:
            aff_c = jnp.pad(aff[:, :, :ctx_len], ((0, 0), (0, 0), (0, rows_c * LANES - ctx_len)), constant_values=-1.0)
            idx_c, val_c = _route(aff_c.reshape(b, ne, rows_c, LANES), cap_c)
            idx_cf = idx_c.reshape(-1)
            xgs.append(_gather(idx_cf, hp, cap_c, ctx_len, 0))
            vals.append(val_c.transpose(1, 0, 2, 3).reshape(ne, b * cap_c, 1))
        ys = _ffn(xgs, vals, w1, w3, w2, l, min(FFN_TF, w1.shape[3]))
        moe_x = _combine(idx_xf, ys[0], b, seq, cap_x)
        moe_c = _combine(idx_cf, ys[1], b, ctx_len, cap_c) if need_ctx else moe_x
        xa = _residual(x1, modsel, moe_x, moe_c, tm, blk0)
    return xa
```

```python
import functools
import math

import jax
import jax.numpy as jnp
from jax import lax
from jax.experimental import pallas as pl
from jax.experimental.pallas import tpu as pltpu

F32 = jnp.float32
BF16 = jnp.bfloat16
I32 = jnp.int32
U32 = jnp.uint32
HIGHEST = lax.Precision.HIGHEST

HEAD_DIM = 64
ATTN_HEADS = 8
ATTN_KV_HEADS = 2
ATTN_REP = ATTN_HEADS // ATTN_KV_HEADS
MLSTM_HEADS = 4
GDN_HEADS = 4
N_DIR = 2
CHUNK = 64
CONV_K = 5
GRID_W = 64
ROPE_THETA = 10000.0
N_EXPERTS = 16
CAPACITY_FACTOR = 2
N_MOD = 6
EPS = 1e-6
ATTN_W = ATTN_HEADS * HEAD_DIM
KV_W = ATTN_KV_HEADS * HEAD_DIM
MLSTM_W = MLSTM_HEADS * HEAD_DIM
GDN_W = GDN_HEADS * HEAD_DIM
LANES = 128
SUBLANES = 8
GATE_W = LANES
GATE_ROWS = 32
GC_MI, GC_MF, GC_GA, GC_GB = 0, 8, 16, 24
VMEM_LIMIT = 56 * 1024 * 1024
ATTN_COLS = 512
ATTN_VPAD = 16
LOG2E = 1.4426950408889634
FFN_TF = 256
CONV_ROWS = 256
PAD_ROWS = 8
COMBINE_GROUP = 8


def _cparams(sem):
    return pltpu.CompilerParams(dimension_semantics=sem, vmem_limit_bytes=VMEM_LIMIT)


def _sigmoid(x):
    return 1.0 / (1.0 + jnp.exp(-x))


def _silu(x):
    return x * _sigmoid(x)


def _log_sigmoid(x):
    return jnp.minimum(x, 0.0) - jnp.log1p(jnp.exp(-jnp.abs(x)))


def _softplus(x):
    return jnp.maximum(x, 0.0) + jnp.log1p(jnp.exp(-jnp.abs(x)))


def _dot(a, b, precision=None):
    return jnp.dot(a, b, preferred_element_type=F32, precision=precision)


def _dot_nt(a, b, precision=None):
    return lax.dot_general(a, b, (((1,), (1,)), ((), ())), preferred_element_type=F32, precision=precision)


def _split_bf16(a):
    hi = a.astype(BF16)
    lo = (a - hi.astype(F32)).astype(BF16)
    return hi, lo


def _split_bf16_3(a):
    hi = a.astype(BF16)
    r = a - hi.astype(F32)
    mid = r.astype(BF16)
    return hi, mid, (r - mid.astype(F32)).astype(BF16)


def _dot3(a, b):
    ah, al = _split_bf16(a)
    bh, bl = _split_bf16(b)
    return _dot(ah, bh) + (_dot(ah, bl) + _dot(al, bh))


def _iota(shape, dim):
    return lax.broadcasted_iota(I32, shape, dim)


def _eye_rows(rows, cols, first):
    return (_iota((rows, cols), 0) + first == _iota((rows, cols), 1)).astype(F32)


def _mod_kernel(c_ref, w_ref, b_ref, o_ref):
    s = _silu(c_ref[...])
    o_ref[0] = _dot(s, w_ref[0], precision=HIGHEST) + b_ref[0]


def _modulation(cvec, mod_w, mod_b):
    depth, d, n = mod_w.shape
    rows = cvec.shape[0]
    tn = d
    return pl.pallas_call(
        _mod_kernel,
        grid=(depth, n // tn),
        in_specs=[
            pl.BlockSpec((rows, d), lambda l, j: (0, 0)),
            pl.BlockSpec((1, d, tn), lambda l, j: (l, 0, j)),
            pl.BlockSpec((1, 1, tn), lambda l, j: (l, 0, j)),
        ],
        out_specs=pl.BlockSpec((1, rows, tn), lambda l, j: (l, 0, j)),
        out_shape=jax.ShapeDtypeStruct((depth, rows, n), F32),
        compiler_params=_cparams(("parallel", "parallel")),
        name="modulation",
    )(cvec, mod_w, mod_b.reshape(depth, 1, n))


def _inproj_kernel(x_ref, mod_ref, g_ref, w_ref, wt_ref, cos_ref, sin_ref, qg_ref, kg_ref, bd_ref,
                   q_ref, k_ref, vt_ref, ml_ref, mkt_ref, mo_ref, gd_ref, gz_ref, gate_ref, gatet_ref, *, d_model):
    x = x_ref[0]
    tm = x.shape[0]
    sh = mod_ref[:, 0:d_model]
    sc = mod_ref[:, d_model:2 * d_model]
    xn = x * lax.rsqrt(jnp.mean(x * x, axis=-1, keepdims=True) + EPS) * g_ref[...]
    h = (xn * (1.0 + sc) + sh).astype(BF16)
    p = _dot(h, w_ref[...])
    pt = _dot_nt(wt_ref[...], h)
    for g in range(ATTN_KV_HEADS):
        vt_ref[0, g, 0:HEAD_DIM, :] = pt[g * HEAD_DIM:(g + 1) * HEAD_DIM].astype(BF16)
        vt_ref[0, g, HEAD_DIM:, :] = (_iota((ATTN_VPAD, tm), 0) == 0).astype(BF16)
    for c in range(tm // CHUNK):
        cs = slice(c * CHUNK, (c + 1) * CHUNK)
        for j in range(MLSTM_HEADS):
            mkt_ref[0, j, c] = pt[KV_W + j * HEAD_DIM:KV_W + (j + 1) * HEAD_DIM, cs] * HEAD_DIM ** -0.5
        gatet_ref[0, c] = pt[KV_W + MLSTM_W:KV_W + MLSTM_W + GATE_ROWS, cs]

    cos = cos_ref[...]
    sin = sin_ref[...]
    first_half = (_iota(cos.shape, 1) % 32) < 16

    def norm_rope(xs, g, scale):
        ms = sum(_dot(piece, bd_ref[...]) for piece in _split_bf16(xs * xs))
        xn_ = xs * lax.rsqrt(ms + EPS) * g
        sw = jnp.where(first_half, pltpu.roll(xn_, LANES - 16, 1), pltpu.roll(xn_, 16, 1))
        return (xn_ * cos + sw * sin) * scale

    for j in range(ATTN_W // LANES):
        qs = norm_rope(p[:, j * LANES:(j + 1) * LANES], qg_ref[...], LOG2E * HEAD_DIM ** -0.5).astype(BF16)
        q_ref[0, 2 * j] = qs[:, 0:HEAD_DIM]
        q_ref[0, 2 * j + 1] = qs[:, HEAD_DIM:LANES]
    ks = norm_rope(p[:, ATTN_W:ATTN_W + KV_W], kg_ref[...], 1.0).astype(BF16)
    k_ref[0, 0] = ks[:, 0:HEAD_DIM]
    k_ref[0, 1] = ks[:, HEAD_DIM:LANES]

    off = ATTN_W + KV_W

    def head(j):
        return p[:, off + j * HEAD_DIM: off + (j + 1) * HEAD_DIM]

    for j in range(12):
        ml_ref[0, j] = head(j) * HEAD_DIM ** -0.5 if 4 <= j < 8 else head(j)
    for j in range(4):
        mo_ref[0, j] = head(12 + j)
    off += 4 * MLSTM_W
    for j in range(12):
        gd_ref[0, j] = head(j)
    for j in range(4):
        gz_ref[0, j] = head(12 + j)
    off += 4 * GDN_W
    gate_ref[0] = p[:, off:off + GATE_W]


def _inproj(xa, modsel, g1, w_r, w_t, cos_t, sin_t, qg, kg, bd, tm):
    b, t, d = xa.shape
    nb = t // tm
    nc = tm // CHUNK
    kern = functools.partial(_inproj_kernel, d_model=d)
    hm_shape = lambda nh, dt: jax.ShapeDtypeStruct((b, nh, t, HEAD_DIM), dt)
    out_shapes = (hm_shape(ATTN_HEADS, BF16), hm_shape(ATTN_KV_HEADS, BF16),
                  jax.ShapeDtypeStruct((b, ATTN_KV_HEADS, HEAD_DIM + ATTN_VPAD, t), BF16),
                  hm_shape(12, F32), jax.ShapeDtypeStruct((b, MLSTM_HEADS, t // CHUNK, HEAD_DIM, CHUNK), F32),
                  hm_shape(4, F32), hm_shape(12, F32), hm_shape(4, F32),
                  jax.ShapeDtypeStruct((b, t, GATE_W), F32),
                  jax.ShapeDtypeStruct((b, t // CHUNK, GATE_ROWS, CHUNK), F32))
    hm = lambda nh: pl.BlockSpec((1, nh, tm, HEAD_DIM), lambda bi, i: (bi, 0, i, 0))
    const = lambda a: pl.BlockSpec(a.shape, lambda bi, i: (0,) * a.ndim)
    return pl.pallas_call(
        kern,
        grid=(b, nb),
        in_specs=[
            pl.BlockSpec((1, tm, d), lambda bi, i: (bi, i, 0)),
            pl.BlockSpec((None, None, 1, N_MOD * d), lambda bi, i: (bi, jnp.minimum(i, 1), 0, 0)),
            const(g1), const(w_r), const(w_t),
            pl.BlockSpec((tm, LANES), lambda bi, i: (i, 0)),
            pl.BlockSpec((tm, LANES), lambda bi, i: (i, 0)),
            const(qg), const(kg), const(bd),
        ],
        out_specs=(hm(ATTN_HEADS), hm(ATTN_KV_HEADS),
                   pl.BlockSpec((1, ATTN_KV_HEADS, HEAD_DIM + ATTN_VPAD, tm), lambda bi, i: (bi, 0, 0, i)),
                   hm(12), pl.BlockSpec((1, MLSTM_HEADS, nc, HEAD_DIM, CHUNK), lambda bi, i: (bi, 0, i, 0, 0)),
                   hm(4), hm(12), hm(4),
                   pl.BlockSpec((1, tm, GATE_W), lambda bi, i: (bi, i, 0)),
                   pl.BlockSpec((1, nc, GATE_ROWS, CHUNK), lambda bi, i: (bi, i, 0, 0))),
        out_shape=out_shapes,
        compiler_params=_cparams(("parallel", "parallel")),
        name="inproj",
    )(xa, modsel, g1, w_r, w_t, cos_t, sin_t, qg, kg, bd)


def _attn_kernel(q_ref, k_ref, vt_ref, o_ref, sa_sc, sb_sc, xa_sc, xb_sc, m_sc, acc_sc, *, tq, tk, n_pairs, blk0):
    i = pl.program_id(2) + blk0
    q = q_ref[0].reshape(ATTN_REP * tq, HEAD_DIM)
    cols = [slice(c * ATTN_COLS, (c + 1) * ATTN_COLS) for c in range(ATTN_REP * tq // ATTN_COLS)]

    m_sc[...] = jnp.full(m_sc.shape, -jnp.inf, F32)
    acc_sc[...] = jnp.zeros(acc_sc.shape, F32)

    def score(s_ref, x_ref, tile):
        start = pl.multiple_of(tile * tk, tk)
        k = k_ref[0, 0, pl.ds(start, tk), :]
        for cs in cols:
            s = _dot_nt(k, q[cs])
            s_ref[:, cs] = s
            x_ref[:, cs] = jnp.max(s, axis=0, keepdims=True)

    def consume(s_ref, x_ref, tile, half):
        start = pl.multiple_of(tile * tk, tk)
        vt = vt_ref[0, 0, :, pl.ds(start, tk)]
        for cs in cols:
            mo = m_sc[half, :, cs]
            mn = jnp.maximum(mo, x_ref[:, cs])
            p = jnp.exp2(s_ref[:, cs] - mn)
            acc_sc[half, :, cs] = jnp.exp2(mo - mn) * acc_sc[half, :, cs] + _dot(vt, p.astype(BF16))
            m_sc[half, :, cs] = mn

    score(sa_sc, xa_sc, 0)

    def body(j, carry):
        score(sb_sc, xb_sc, 2 * j + 1)
        consume(sa_sc, xa_sc, 2 * j, 0)
        score(sa_sc, xa_sc, 2 * j + 2)
        consume(sb_sc, xb_sc, 2 * j + 1, 1)
        return carry

    n = jnp.where(i == 0, 0, n_pairs)
    lax.fori_loop(0, n, body, 0)
    consume(sa_sc, xa_sc, 2 * n, 0)
    m = jnp.maximum(m_sc[0], m_sc[1])
    acc = jnp.exp2(m_sc[0] - m) * acc_sc[0] + jnp.exp2(m_sc[1] - m) * acc_sc[1]
    o_t = (acc[0:HEAD_DIM, :] / acc[HEAD_DIM:HEAD_DIM + 1, :]).astype(BF16)
    eye = (_iota((tq, tq), 0) == _iota((tq, tq), 1)).astype(BF16)
    for r in range(ATTN_REP):
        o_ref[0, r] = _dot_nt(eye, o_t[:, r * tq:(r + 1) * tq]).astype(o_ref.dtype)


def _attention(q, k, vt, ctx_len, tq, blk0):
    b, _, t, _ = q.shape
    tk = ctx_len
    assert (t // tk) % 2 == 1
    nq = t // tq - blk0
    kern = functools.partial(_attn_kernel, tq=tq, tk=tk, n_pairs=(t // tk - 1) // 2, blk0=blk0)
    rows = ATTN_REP * tq
    return pl.pallas_call(
        kern,
        grid=(b, ATTN_KV_HEADS, nq),
        in_specs=[
            pl.BlockSpec((1, ATTN_REP, tq, HEAD_DIM), lambda bi, g, i: (bi, g, i + blk0, 0)),
            pl.BlockSpec((1, 1, t, HEAD_DIM), lambda bi, g, i: (bi, g, 0, 0)),
            pl.BlockSpec((1, 1, HEAD_DIM + ATTN_VPAD, t), lambda bi, g, i: (bi, g, 0, 0)),
        ],
        out_specs=pl.BlockSpec((1, ATTN_REP, tq, HEAD_DIM), lambda bi, g, i: (bi, g, i + blk0, 0)),
        out_shape=jax.ShapeDtypeStruct((b, ATTN_HEADS, t, HEAD_DIM), BF16),
        scratch_shapes=[pltpu.VMEM((tk, rows), F32), pltpu.VMEM((tk, rows), F32),
                        pltpu.VMEM((1, rows), F32), pltpu.VMEM((1, rows), F32),
                        pltpu.VMEM((2, 1, rows), F32), pltpu.VMEM((2, HEAD_DIM + ATTN_VPAD, rows), F32)],
        compiler_params=_cparams(("parallel", "parallel", "arbitrary")),
        name="attention",
    )(q, k, vt)


def _chunk_maps(nc_c, nc_x):
    fwd = lambda j: j
    rev = lambda j: jnp.where(j < nc_c, nc_c - 1 - j, 2 * nc_c + nc_x - 1 - j)
    return fwd, rev


def _dir_masks(d):
    r = _iota((CHUNK, CHUNK), 0)
    c = _iota((CHUNK, CHUNK), 1)
    incl = (r >= c) if d == 0 else (r <= c)
    strict = (r > c) if d == 0 else (r < c)
    incl_t = (r <= c) if d == 0 else (r >= c)
    return incl, strict, incl.astype(F32), incl_t.astype(F32)


def _mlstm_kernel(xf_ref, xr_ref, ktf_ref, ktr_ref, gf_ref, gr_ref, gtf_ref, gtr_ref, brow_ref, bcol_ref,
                  hf_ref, hr_ref, c_sc, n_sc, m_sc, *, nb):
    @pl.when(pl.program_id(0) == 0)
    def _():
        c_sc[...] = jnp.zeros(c_sc.shape, F32)
        n_sc[...] = jnp.zeros(n_sc.shape, F32)
        m_sc[...] = jnp.zeros(m_sc.shape, F32)

    refs = ((xf_ref, ktf_ref, gf_ref, gtf_ref, hf_ref), (xr_ref, ktr_ref, gr_ref, gtr_ref, hr_ref))
    masks = [_dir_masks(d) for d in range(N_DIR)]
    gate = {}
    for d in range(N_DIR):
        _, _, g_ref, gt_ref, _ = refs[d]
        _, _, tri, tri_t = masks[d]
        sel = (_iota((GATE_W, MLSTM_HEADS * LANES), 0)
               == GC_MF + d * MLSTM_HEADS + _iota((GATE_W, MLSTM_HEADS * LANES), 1) // LANES).astype(BF16)
        for b in range(nb):
            cum = _dot(tri, _log_sigmoid(g_ref[b] + brow_ref[...]), precision=HIGHEST)
            g_t = gt_ref[b, 0, GC_MI:GC_MI + 16, :] + bcol_ref[...]
            lf_t = _log_sigmoid(g_t)
            cum_rep = sum(_dot(piece, sel) for piece in _split_bf16_3(cum))
            gate[d, b] = (cum_rep, g_t, lf_t, _dot(lf_t, tri_t, precision=HIGHEST))

    probs = [(d, b, h) for d in range(N_DIR) for b in range(nb) for h in range(MLSTM_HEADS)]
    st = []
    for d, b, h in probs:
        p = (d * nb + b) * MLSTM_HEADS + h
        x_ref, kt_ref = refs[d][0], refs[d][1]
        cum_rep, g_t, lf_t, cum_t = gate[d, b]
        ri = d * MLSTM_HEADS + h
        rf = GC_MF + ri
        bcum_col = cum_rep[:, h * LANES:h * LANES + CHUNK]
        bcum_row = cum_t[rf - GC_MI:rf - GC_MI + 1, :]
        i_row = g_t[ri:ri + 1, :]
        b_last = jnp.sum(lf_t[rf - GC_MI:rf - GC_MI + 1, :], axis=-1, keepdims=True)
        m_old = m_sc[p, 0:1, 0:1]
        w_end = b_last - bcum_row + i_row
        m_new = jnp.maximum(b_last + m_old, jnp.max(w_end, axis=-1, keepdims=True))
        a_row = jnp.exp(w_end - m_new)
        dec = jnp.exp(b_last + m_old - m_new)
        dmat = jnp.where(masks[d][0], bcum_col - bcum_row + i_row, -jnp.inf)
        inter = bcum_col + m_sc[p, 0:1, 0:CHUNK]
        m_t = jnp.maximum(inter, jnp.max(dmat, axis=-1, keepdims=True))
        st.append(dict(p=p, q=x_ref[b, h], k=x_ref[b, MLSTM_HEADS + h], v=x_ref[b, 2 * MLSTM_HEADS + h],
                       k_t=kt_ref[b, h, 0], c_old=c_sc[p], n_old=n_sc[p], a_row=a_row, dec=dec, m_new=m_new,
                       m_t=m_t, w_in=jnp.exp(inter - m_t), dexp=jnp.exp(dmat - m_t), out=refs[d][4], b=b, h=h))
    for e in st:
        e["s"] = _dot(e["q"], e["k_t"]) * e["dexp"]
    for e in st:
        e["qc"] = _dot(e["q"], e["c_old"])
    for e in st:
        e["sv"] = _dot(e["s"], e["v"])
    for e in st:
        e["kv"] = _dot(e["k_t"] * e["a_row"], e["v"])
    for e in st:
        e["ak"] = _dot(jnp.broadcast_to(e["a_row"], (SUBLANES, CHUNK)), e["k"])
    for e in st:
        num = e["w_in"] * e["qc"] + e["sv"]
        den = (e["w_in"] * jnp.sum(e["q"] * e["n_old"][0:1, :], axis=-1, keepdims=True)
               + jnp.sum(e["s"], axis=-1, keepdims=True))
        e["out"][e["b"], e["h"]] = num / jnp.maximum(jnp.abs(den), jnp.exp(-e["m_t"]))
    for e in st:
        p = e["p"]
        c_sc[p] = e["dec"] * e["c_old"] + e["kv"]
        n_sc[p] = e["dec"] * e["n_old"] + e["ak"]
        m_sc[p] = jnp.broadcast_to(e["m_new"], (SUBLANES, LANES))


def _mlstm_scan(mqkv, k_t, gates, gates_t, brow, bcol, ctx_len):
    b, _, t, _ = mqkv.shape
    nc_c, nc_x = ctx_len // CHUNK, (t - ctx_len) // CHUNK
    fwd, rev = _chunk_maps(nc_c, nc_x)
    nprob = N_DIR * b * MLSTM_HEADS
    xs = lambda f: pl.BlockSpec((b, 12, CHUNK, HEAD_DIM), lambda j: (0, 0, f(j), 0))
    ks = lambda f: pl.BlockSpec((b, MLSTM_HEADS, 1, HEAD_DIM, CHUNK), lambda j: (0, 0, f(j), 0, 0))
    gs = lambda f: pl.BlockSpec((b, CHUNK, GATE_W), lambda j: (0, f(j), 0))
    gts = lambda f: pl.BlockSpec((b, 1, GATE_ROWS, CHUNK), lambda j: (0, f(j), 0, 0))
    hs = lambda f: pl.BlockSpec((b, MLSTM_HEADS, CHUNK, HEAD_DIM), lambda j: (0, 0, f(j), 0))
    out = jax.ShapeDtypeStruct((b, MLSTM_HEADS, t, HEAD_DIM), F32)
    return pl.pallas_call(
        functools.partial(_mlstm_kernel, nb=b),
        grid=(nc_c + nc_x,),
        in_specs=[xs(fwd), xs(rev), ks(fwd), ks(rev), gs(fwd), gs(rev), gts(fwd), gts(rev),
                  pl.BlockSpec((1, GATE_W), lambda j: (0, 0)),
                  pl.BlockSpec((16, CHUNK), lambda j: (0, 0))],
        out_specs=(hs(fwd), hs(rev)),
        out_shape=(out, out),
        scratch_shapes=[pltpu.VMEM((nprob, HEAD_DIM, HEAD_DIM), F32),
                        pltpu.VMEM((nprob, SUBLANES, HEAD_DIM), F32),
                        pltpu.VMEM((nprob, SUBLANES, LANES), F32)],
        compiler_params=_cparams(("arbitrary",)),
        name="mlstm_scan",
    )(mqkv, mqkv, k_t, k_t, gates, gates, gates_t, gates_t, brow, bcol)


def _gdn_prep_kernel(x_ref, w_ref, o_ref, pad_sc, *, ctx_len, t):
    part = pl.program_id(1) // GDN_HEADS
    zeros = jnp.zeros((PAD_ROWS, HEAD_DIM), F32)
    pad_sc[0:PAD_ROWS] = zeros
    pad_sc[PAD_ROWS:PAD_ROWS + ctx_len] = x_ref[0, 0, 0:ctx_len]
    pad_sc[PAD_ROWS + ctx_len:2 * PAD_ROWS + ctx_len] = zeros
    pad_sc[2 * PAD_ROWS + ctx_len:2 * PAD_ROWS + t] = x_ref[0, 0, ctx_len:t]
    pad_sc[2 * PAD_ROWS + t:3 * PAD_ROWS + t] = zeros
    w = w_ref[0]
    is_qk = part < 2
    scale = jnp.where(part == 0, HEAD_DIM ** -0.5, 1.0)
    for c in range(t // CONV_ROWS):
        r0 = c * CONV_ROWS
        base = r0 + (PAD_ROWS if r0 < ctx_len else 2 * PAD_ROWS) - CONV_K // 2
        y = w[0:1, :] * pad_sc[base:base + CONV_ROWS]
        for j in range(1, CONV_K):
            y = y + w[j:j + 1, :] * pad_sc[base + j:base + j + CONV_ROWS]
        y = _silu(y)
        yn = y * lax.rsqrt(jnp.sum(y * y, axis=-1, keepdims=True) + EPS) * scale
        o_ref[0, 0, r0:r0 + CONV_ROWS] = jnp.where(is_qk, yn, y)


def _gdn_prep(gqkv, conv_w12, ctx_len):
    b, np_, t, _ = gqkv.shape
    return pl.pallas_call(
        functools.partial(_gdn_prep_kernel, ctx_len=ctx_len, t=t),
        grid=(b, np_),
        in_specs=[pl.BlockSpec((1, 1, t, HEAD_DIM), lambda bi, p: (bi, p, 0, 0)),
                  pl.BlockSpec((1, CONV_K, HEAD_DIM), lambda bi, p: (p, 0, 0))],
        out_specs=pl.BlockSpec((1, 1, t, HEAD_DIM), lambda bi, p: (bi, p, 0, 0)),
        out_shape=jax.ShapeDtypeStruct(gqkv.shape, F32),
        scratch_shapes=[pltpu.VMEM((t + 3 * PAD_ROWS, HEAD_DIM), F32)],
        compiler_params=_cparams(("parallel", "parallel")),
        name="gdn_prep",
    )(gqkv, conv_w12)


def _gdn_kernel(xf_ref, xr_ref, gf_ref, gr_ref, gtf_ref, gtr_ref, arow_ref, drow_ref, acol_ref, dcol_ref,
                of_ref, or_ref, s_sc, rhs_sc, *, nb):
    @pl.when(pl.program_id(0) == 0)
    def _():
        s_sc[...] = jnp.zeros(s_sc.shape, F32)

    refs = ((xf_ref, gf_ref, gtf_ref, of_ref), (xr_ref, gr_ref, gtr_ref, or_ref))
    masks = [_dir_masks(d) for d in range(N_DIR)]
    gate = {}
    for d in range(N_DIR):
        g_ref, gt_ref = refs[d][1], refs[d][2]
        _, _, tri, tri_t = masks[d]
        for b in range(nb):
            g = g_ref[b]
            gval = arow_ref[...] * _softplus(g + drow_ref[...])
            g_t = gt_ref[b, 0, GC_GA:GC_GA + 16, :]
            gval_t = acol_ref[...] * _softplus(g_t + dcol_ref[...])
            gate[d, b] = (_dot(tri, gval, precision=HIGHEST), _sigmoid(g), gval_t,
                          _dot(gval_t, tri_t, precision=HIGHEST))

    probs = [(d, b, h) for d in range(N_DIR) for b in range(nb) for h in range(GDN_HEADS)]
    eye = _eye_rows(HEAD_DIM, HEAD_DIM, 0).astype(BF16)
    st = []
    for d, b, h in probs:
        x_ref = refs[d][0]
        st.append(dict(p=(d * nb + b) * GDN_HEADS + h, d=d, b=b, h=h, q=x_ref[b, h], k=x_ref[b, GDN_HEADS + h],
                       v=x_ref[b, 2 * GDN_HEADS + h]))
    for e in st:
        e["k_t"] = sum(_dot_nt(eye, piece) for piece in _split_bf16(e["k"]))
    for e in st:
        d, b, h, p = e["d"], e["b"], e["h"], e["p"]
        incl, strict, _, _ = masks[d]
        gcum, beta_all, gval_t, gcum_t = gate[d, b]
        ra = d * GDN_HEADS + h
        g_col = gcum[:, GC_GA + ra:GC_GA + ra + 1]
        g_row = gcum_t[ra:ra + 1, :]
        beta = beta_all[:, GC_GB + ra:GC_GB + ra + 1]
        g_last = jnp.sum(gval_t[ra:ra + 1, :], axis=-1, keepdims=True)
        eg = jnp.exp(g_col)
        kb = e["k"] * beta
        rhs_sc[p, :, 0:HEAD_DIM] = e["v"] * beta
        rhs_sc[p, :, HEAD_DIM:] = kb * eg
        e.update(out=refs[d][3], strict=strict, eg_last=jnp.exp(g_last), kb=kb, qg=e["q"] * eg,
                 decay=jnp.where(incl, jnp.exp(jnp.where(incl, g_col - g_row, 0.0)), 0.0),
                 ktg=e["k_t"] * jnp.exp(g_last - g_row), s_old=s_sc[p])
    for e in st:
        e["sol"] = rhs_sc[e["p"]]
    for e in st:
        e["pw"] = -jnp.where(e["strict"], _dot3(e["kb"], e["k_t"]) * e["decay"], 0.0)
    n_fac = CHUNK.bit_length() - 1
    for it in range(n_fac):
        for e in st:
            e["sol"] = e["sol"] + _dot3(e["pw"], e["sol"])
        if it < n_fac - 1:
            for e in st:
                e["pw"] = _dot3(e["pw"], e["pw"])
    for e in st:
        e["ws"] = _dot(e["sol"][:, HEAD_DIM:], e["s_old"])
    for e in st:
        e["qs"] = _dot(e["qg"], e["s_old"])
    for e in st:
        e["qk"] = _dot(e["q"], e["k_t"]) * e["decay"]
    for e in st:
        e["v_new"] = e["sol"][:, :HEAD_DIM] - e["ws"]
    for e in st:
        e["out"][e["b"], e["h"]] = e["qs"] + _dot(e["qk"], e["v_new"])
    for e in st:
        s_sc[e["p"]] = e["eg_last"] * e["s_old"] + _dot(e["ktg"], e["v_new"])


def _gdn_scan(gp, gates, gates_t, arow, drow, acol, dcol, ctx_len):
    b, _, t, _ = gp.shape
    nc_c, nc_x = ctx_len // CHUNK, (t - ctx_len) // CHUNK
    fwd, rev = _chunk_maps(nc_c, nc_x)
    nprob = N_DIR * b * GDN_HEADS
    xs = lambda f: pl.BlockSpec((b, 3 * GDN_HEADS, CHUNK, HEAD_DIM), lambda j: (0, 0, f(j), 0))
    gs = lambda f: pl.BlockSpec((b, CHUNK, GATE_W), lambda j: (0, f(j), 0))
    gts = lambda f: pl.BlockSpec((b, 1, GATE_ROWS, CHUNK), lambda j: (0, f(j), 0, 0))
    hs = lambda f: pl.BlockSpec((b, GDN_HEADS, CHUNK, HEAD_DIM), lambda j: (0, 0, f(j), 0))
    row = pl.BlockSpec((1, GATE_W), lambda j: (0, 0))
    col = pl.BlockSpec((16, CHUNK), lambda j: (0, 0))
    out = jax.ShapeDtypeStruct((b, GDN_HEADS, t, HEAD_DIM), F32)
    return pl.pallas_call(
        functools.partial(_gdn_kernel, nb=b),
        grid=(nc_c + nc_x,),
        in_specs=[xs(fwd), xs(rev), gs(fwd), gs(rev), gts(fwd), gts(rev), row, row, col, col],
        out_specs=(hs(fwd), hs(rev)),
        out_shape=(out, out),
        scratch_shapes=[pltpu.VMEM((nprob, HEAD_DIM, HEAD_DIM), F32),
                        pltpu.VMEM((nprob, CHUNK, 2 * HEAD_DIM), F32)],
        compiler_params=_cparams(("arbitrary",)),
        name="gdn_scan",
    )(gp, gp, gates, gates, gates_t, gates_t, arow, drow, acol, dcol)


def _outproj_kernel(x_ref, mod_ref, ao_ref, mhf_ref, mhr_ref, mo_ref, mg_ref, ghf_ref, ghr_ref, gz_ref, gg_ref,
                    w_ref, n2_ref, rw_ref, x1_ref, hp_ref, aff_ref, mix_sc, *, d_model):
    d = d_model

    def put(j, val):
        mix_sc[:, j * HEAD_DIM:(j + 1) * HEAD_DIM] = val.astype(BF16)

    for h in range(ATTN_HEADS):
        put(h, ao_ref[0, h])
    for h in range(MLSTM_HEADS):
        hh = mhf_ref[0, h] + mhr_ref[0, h]
        hn = hh * lax.rsqrt(jnp.mean(hh * hh, axis=-1, keepdims=True) + EPS) * mg_ref[h:h + 1, :]
        put(ATTN_HEADS + h, _sigmoid(mo_ref[0, h]) * hn)
    for h in range(GDN_HEADS):
        oo = ghf_ref[0, h] + ghr_ref[0, h]
        on = oo * lax.rsqrt(jnp.mean(oo * oo, axis=-1, keepdims=True) + EPS) * gg_ref[...]
        put(ATTN_HEADS + MLSTM_HEADS + h, on * _silu(gz_ref[0, h]))
    x1 = x_ref[0] + mod_ref[:, 2 * d:3 * d] * _dot(mix_sc[...], w_ref[...])
    x1_ref[0] = x1
    xn = x1 * lax.rsqrt(jnp.mean(x1 * x1, axis=-1, keepdims=True) + EPS) * n2_ref[...]
    h2 = xn * (1.0 + mod_ref[:, 4 * d:5 * d]) + mod_ref[:, 3 * d:4 * d]
    logits = _dot_nt(rw_ref[...], h2, precision=HIGHEST)
    e = jnp.exp(logits - jnp.max(logits, axis=0, keepdims=True))
    aff_ref[0] = e / jnp.sum(e, axis=0, keepdims=True)
    hp_ref[0] = h2


def _outproj(xa, modsel, ao, mhf, mhr, mo, mg, ghf, ghr, gz, gg, w_hm, n2, rw_t, tm, blk0):
    b, t, d = xa.shape
    nb = t // tm - blk0
    hm = lambda nh: pl.BlockSpec((1, nh, tm, HEAD_DIM), lambda bi, i: (bi, 0, i + blk0, 0))
    full = lambda a: pl.BlockSpec(a.shape, lambda bi, i: (0,) * a.ndim)
    return pl.pallas_call(
        functools.partial(_outproj_kernel, d_model=d),
        grid=(b, nb),
        in_specs=[
            pl.BlockSpec((1, tm, d), lambda bi, i: (bi, i + blk0, 0)),
            pl.BlockSpec((None, None, 1, N_MOD * d), lambda bi, i: (bi, jnp.minimum(i + blk0, 1), 0, 0)),
            hm(ATTN_HEADS), hm(4), hm(4), hm(4), full(mg), hm(4), hm(4), hm(4), full(gg),
            full(w_hm), full(n2), full(rw_t),
        ],
        out_specs=(pl.BlockSpec((1, tm, d), lambda bi, i: (bi, i + blk0, 0)),
                   pl.BlockSpec((1, tm, d), lambda bi, i: (bi, i + blk0, 0)),
                   pl.BlockSpec((1, N_EXPERTS, tm), lambda bi, i: (bi, 0, i + blk0))),
        out_shape=(jax.ShapeDtypeStruct((b, t, d), F32),
                   jax.ShapeDtypeStruct((b, t, d), F32),
                   jax.ShapeDtypeStruct((b, N_EXPERTS, t), F32)),
        scratch_shapes=[pltpu.VMEM((tm, w_hm.shape[0]), BF16)],
        compiler_params=_cparams(("parallel", "parallel")),
        name="outproj",
    )(xa, modsel, ao, mhf, mhr, mo, mg, ghf, ghr, gz, gg, w_hm, n2, rw_t)


def _route_kernel(aff_ref, idx_ref, val_ref, *, cap, rows):
    ne = N_EXPERTS
    a = aff_ref[0]

    def count(mask):
        return jnp.sum(jnp.sum(mask.astype(I32), axis=2, keepdims=True), axis=1, keepdims=True)

    tau_bits = jnp.zeros((ne, 1, 1), I32)
    for bit in range(30, -1, -1):
        cand = tau_bits | (1 << bit)
        keep = count(a >= lax.bitcast_convert_type(cand, F32)) >= cap
        tau_bits = jnp.where(keep, cand, tau_bits)
    tau = lax.bitcast_convert_type(tau_bits, F32)
    gt = a > tau
    eq = a == tau
    need = cap - count(gt)

    triu = (_iota((LANES, LANES), 0) <= _iota((LANES, LANES), 1)).astype(BF16)
    strict_lower = (_iota((rows, rows), 0) > _iota((rows, rows), 1)).astype(BF16)
    triu_r = (_iota((rows, rows), 0) <= _iota((rows, rows), 1)).astype(BF16)
    ones_r = jnp.ones((SUBLANES, LANES), BF16)

    def prefix(mask2d):
        m = mask2d.astype(BF16)
        within = _dot(m, triu)
        tot = jnp.broadcast_to(within[:, LANES - 1:LANES], (rows, LANES)).astype(BF16)
        return within, _dot(strict_lower, tot)

    lane_r = _iota((cap, rows), 1).astype(F32)
    lane_l = _iota((cap, LANES), 1).astype(F32)
    slot = _iota((cap, 1), 0).astype(F32)
    for e in range(ne):
        eq_e = eq[e]
        w_eq, before_eq = prefix(eq_e)
        rank_eq = w_eq - eq_e.astype(F32) + before_eq
        sel = gt[e] | (eq_e & (rank_eq < need[e].astype(F32)))
        rel, _ = prefix(sel)
        sel_b = sel.astype(BF16)
        row_tot = _dot_nt(ones_r, sel_b)
        row_incl = _dot(row_tot.astype(BF16), triu_r)
        row_excl = row_incl - row_tot
        kstar = jnp.sum((row_incl[0:1, :] <= slot).astype(F32), axis=-1, keepdims=True)
        onehot = (lane_r == kstar).astype(F32)
        base = jnp.sum(onehot * row_excl[0:1, :], axis=-1, keepdims=True)
        g_rel = _dot(onehot.astype(BF16), rel.astype(BF16))
        within = jnp.sum((g_rel <= slot - base).astype(F32), axis=-1, keepdims=True)
        g_aff = _dot(onehot, a[e], precision=HIGHEST)
        val_ref[0, e] = jnp.sum(jnp.where(lane_l == within, g_aff, 0.0), axis=-1, keepdims=True)
        idx_ref[0, e] = (kstar * LANES + within).astype(I32)


def _route(aff_tiles, cap):
    b, ne, rows, _ = aff_tiles.shape
    return pl.pallas_call(
        functools.partial(_route_kernel, cap=cap, rows=rows),
        grid=(b,),
        in_specs=[pl.BlockSpec((1, ne, rows, LANES), lambda bi: (bi, 0, 0, 0))],
        out_specs=(pl.BlockSpec((1, ne, cap, 1), lambda bi: (bi, 0, 0, 0)),
                   pl.BlockSpec((1, ne, cap, 1), lambda bi: (bi, 0, 0, 0))),
        out_shape=(jax.ShapeDtypeStruct((b, ne, cap, 1), I32), jax.ShapeDtypeStruct((b, ne, cap, 1), F32)),
        compiler_params=_cparams(("parallel",)),
        name="route",
    )(aff_tiles)


def _gather_kernel(idx_ref, h_ref, o_ref, rows_sc, *, cap, row_off):
    base = (pl.program_id(0) * pl.num_programs(2) + pl.program_id(2)) * cap

    def body(s, carry):
        t = idx_ref[base + s] + row_off
        rows_sc[pl.ds(s, 1), :] = h_ref[0, pl.ds(t, 1), :]
        return carry

    lax.fori_loop(0, cap, body, 0, unroll=8)
    o_ref[0, 0] = rows_sc[...].astype(BF16)


def _gather(idx_flat, h2, cap, rows_block, row_off):
    b, _, d = h2.shape
    dh = d // 2
    return pl.pallas_call(
        functools.partial(_gather_kernel, cap=cap, row_off=row_off),
        grid_spec=pltpu.PrefetchScalarGridSpec(
            num_scalar_prefetch=1,
            grid=(b, 2, N_EXPERTS),
            in_specs=[pl.BlockSpec((1, rows_block, dh), lambda bi, c, e, idx: (bi, 0, c))],
            out_specs=pl.BlockSpec((1, 1, cap, dh), lambda bi, c, e, idx: (e, bi, 0, c)),
            scratch_shapes=[pltpu.VMEM((cap, dh), F32)],
        ),
        out_shape=jax.ShapeDtypeStruct((N_EXPERTS, b, cap, d), BF16),
        compiler_params=_cparams(("arbitrary", "arbitrary", "arbitrary")),
        name="moe_gather",
    )(idx_flat, h2)


def _ffn_kernel(*refs, n_streams):
    xg_refs = refs[0:n_streams]
    val_refs = refs[n_streams:2 * n_streams]
    w1_ref, w3_ref, w2_ref = refs[2 * n_streams:2 * n_streams + 3]
    y_refs = refs[2 * n_streams + 3:]
    f = pl.program_id(1)
    w1 = w1_ref[0].astype(BF16)
    w3 = w3_ref[0].astype(BF16)
    w2 = w2_ref[0].astype(BF16)
    for xg_ref, val_ref, y_ref in zip(xg_refs, val_refs, y_refs):
        xg = xg_ref[0].reshape(-1, w1.shape[0])
        y = _dot((_silu(_dot(xg, w1)) * _dot(xg, w3)).astype(BF16), w2)

        @pl.when(f == 0)
        def _():
            y_ref[0] = y

        @pl.when(f > 0)
        def _():
            y_ref[0] = y_ref[0] + y

        @pl.when(f == pl.num_programs(1) - 1)
        def _():
            y_ref[0] = y_ref[0] * val_ref[0]


def _ffn(xgs, vals, w1, w3, w2, layer, tf):
    _, ne, d, ff = w1.shape
    n_streams = len(xgs)
    in_specs, out_specs, out_shapes = [], [], []
    for xg in xgs:
        in_specs.append(pl.BlockSpec((1,) + xg.shape[1:], lambda e, f: (e, 0, 0, 0)))
    for v in vals:
        in_specs.append(pl.BlockSpec((1,) + v.shape[1:], lambda e, f: (e, 0, 0)))
    in_specs += [pl.BlockSpec((None, 1, d, tf), lambda e, f: (layer, e, 0, f)),
                 pl.BlockSpec((None, 1, d, tf), lambda e, f: (layer, e, 0, f)),
                 pl.BlockSpec((None, 1, tf, d), lambda e, f: (layer, e, f, 0))]
    for xg in xgs:
        m = xg.shape[1] * xg.shape[2]
        out_specs.append(pl.BlockSpec((1, m, d), lambda e, f: (e, 0, 0)))
        out_shapes.append(jax.ShapeDtypeStruct((ne, m, d), F32))
    return pl.pallas_call(
        functools.partial(_ffn_kernel, n_streams=n_streams),
        grid=(ne, ff // tf),
        in_specs=in_specs,
        out_specs=tuple(out_specs),
        out_shape=tuple(out_shapes),
        compiler_params=_cparams(("parallel", "arbitrary")),
        name="moe_ffn",
    )(*xgs, *vals, w1, w3, w2)


def _combine_kernel(idx_ref, y_ref, o_ref, *, cap):
    e = pl.program_id(2)

    @pl.when(e == 0)
    def _():
        o_ref[...] = jnp.zeros(o_ref.shape, F32)

    base = (pl.program_id(0) * pl.num_programs(2) + e) * cap

    def body(g, carry):
        s0 = pl.multiple_of(g * COMBINE_GROUP, COMBINE_GROUP)
        ts = [idx_ref[base + s0 + r] for r in range(COMBINE_GROUP)]
        ys = y_ref[0, pl.ds(s0, COMBINE_GROUP), :]
        rows = [o_ref[0, pl.ds(t, 1), :] for t in ts]
        for r, t in enumerate(ts):
            o_ref[0, pl.ds(t, 1), :] = rows[r] + ys[r:r + 1, :]
        return carry

    lax.fori_loop(0, cap // COMBINE_GROUP, body, 0)


def _combine(idx_flat, y, b, n_tok, cap):
    ne, _, d = y.shape
    dh = d // 2
    return pl.pallas_call(
        functools.partial(_combine_kernel, cap=cap),
        grid_spec=pltpu.PrefetchScalarGridSpec(
            num_scalar_prefetch=1,
            grid=(b, 2, ne),
            in_specs=[pl.BlockSpec((1, cap, dh), lambda bi, c, e, idx: (e, bi, c))],
            out_specs=pl.BlockSpec((1, n_tok, dh), lambda bi, c, e, idx: (bi, 0, c)),
        ),
        out_shape=jax.ShapeDtypeStruct((b, n_tok, d), F32),
        compiler_params=_cparams(("arbitrary", "arbitrary", "arbitrary")),
        name="moe_combine",
    )(idx_flat, y)


def _residual_kernel(x1_ref, mod_ref, mx_ref, mc_ref, o_ref, *, d_model, blk0):
    i = pl.program_id(1) + blk0
    moe = jnp.where(i == 0, mc_ref[0], mx_ref[0])
    o_ref[0] = x1_ref[0] + mod_ref[:, 5 * d_model:6 * d_model] * moe


def _residual(x1, modsel, moe_x, moe_c, tm, blk0):
    b, t, d = x1.shape
    nb = t // tm - blk0
    return pl.pallas_call(
        functools.partial(_residual_kernel, d_model=d, blk0=blk0),
        grid=(b, nb),
        in_specs=[pl.BlockSpec((1, tm, d), lambda bi, i: (bi, i + blk0, 0)),
                  pl.BlockSpec((None, None, 1, N_MOD * d), lambda bi, i: (bi, jnp.minimum(i + blk0, 1), 0, 0)),
                  pl.BlockSpec((1, tm, d), lambda bi, i: (bi, jnp.maximum(i + blk0 - 1, 0), 0)),
                  pl.BlockSpec((1, tm, d), lambda bi, i: (bi, 0, 0))],
        out_specs=pl.BlockSpec((1, tm, d), lambda bi, i: (bi, i, 0)),
        out_shape=jax.ShapeDtypeStruct((b, nb * tm, d), F32),
        compiler_params=_cparams(("parallel", "parallel")),
        name="moe_residual",
    )(x1, modsel, moe_x, moe_c)


def _rope_tables(ctx_len, seq):
    n = jnp.arange(seq)
    pos = jnp.stack([n // GRID_W, n % GRID_W], axis=-1).astype(F32)
    lane = jnp.arange(LANES) % HEAD_DIM
    axis = lane // 32
    n_freq = HEAD_DIM // 4
    inv = ROPE_THETA ** (-(lane % n_freq).astype(F32) / n_freq)
    ang = pos[:, axis] * inv
    sign = jnp.where((lane % 32) < 16, -1.0, 1.0)
    cos_t = jnp.concatenate([jnp.ones((ctx_len, LANES), F32), jnp.cos(ang)], axis=0)
    sin_t = jnp.concatenate([jnp.zeros((ctx_len, LANES), F32), jnp.sin(ang) * sign], axis=0)
    return cos_t, sin_t


def _reorder_w_in(w):
    d = w.shape[0]
    qk = w[:, :ATTN_W + KV_W]
    v = w[:, ATTN_W + KV_W:ATTN_W + 2 * KV_W]
    o = ATTN_W + 2 * KV_W
    ml = w[:, o:o + 4 * MLSTM_W]
    o += 4 * MLSTM_W
    mgate = w[:, o:o + 2 * N_DIR * MLSTM_HEADS]
    o += 2 * N_DIR * MLSTM_HEADS
    gd = w[:, o:o + 4 * GDN_W]
    o += 4 * GDN_W
    ggate = w[:, o:o + 2 * N_DIR * GDN_HEADS]
    pad = jnp.zeros((d, GATE_W - mgate.shape[1] - ggate.shape[1]), w.dtype)
    w_r = jnp.concatenate([qk, ml, gd, mgate, ggate, pad], axis=1)
    w_t = jnp.concatenate([v, ml[:, MLSTM_W:2 * MLSTM_W], mgate, ggate], axis=1).T
    return w_r.astype(BF16), w_t.astype(BF16)


def _gate_row(vals, col0):
    flat = vals.reshape(-1).astype(F32)
    return jnp.zeros((1, GATE_W), F32).at[0, col0:col0 + flat.shape[0]].set(flat)


def _gate_col(first, second):
    flat = jnp.concatenate([first.reshape(-1), second.reshape(-1)]).astype(F32)
    return jnp.broadcast_to(flat[:, None], (flat.shape[0], CHUNK))


def kernel(x, c, ctx, c_ctx, mod_w, mod_b, norm1_g, w_in, q_norm_g, k_norm_g, mlstm_i_bias, mlstm_f_bias,
           mlstm_out_g, gdn_conv_w, gdn_a_log, gdn_dt_bias, gdn_out_g, w_out, norm2_g, router_w, w1, w3, w2):
    b, seq, d = x.shape
    ctx_len = ctx.shape[1]
    depth = mod_w.shape[0]
    tm = ctx_len
    t = ctx_len + seq
    ne = N_EXPERTS
    cap_x = CAPACITY_FACTOR * seq // ne
    cap_c = CAPACITY_FACTOR * ctx_len // ne
    rows_c = 2 * SUBLANES

    cvec = jnp.concatenate([c, c_ctx[None, :], jnp.zeros((SUBLANES - b - 1, d), F32)], axis=0)
    mod = _modulation(cvec, mod_w, mod_b)
    cos_t, sin_t = _rope_tables(ctx_len, seq)
    bd = jnp.kron(jnp.eye(LANES // HEAD_DIM, dtype=F32), jnp.full((HEAD_DIM, HEAD_DIM), 1.0 / HEAD_DIM, F32)).astype(BF16)
    xa = jnp.concatenate([ctx, x], axis=1)
    zeros8 = jnp.zeros((N_DIR, MLSTM_HEADS), F32)

    for l in range(depth):
        need_ctx = l < depth - 1
        blk0 = 0 if need_ctx else 1
        modsel = jnp.stack([jnp.broadcast_to(mod[l, b], (b, N_MOD * d)), mod[l, :b]], axis=1)[:, :, None, :]
        w_r, w_t = _reorder_w_in(w_in[l])
        qg = jnp.tile(q_norm_g[l], LANES // HEAD_DIM)[None, :]
        kg = jnp.tile(k_norm_g[l], LANES // HEAD_DIM)[None, :]
        q, k, vt, mqkv, mkt, mo, gqkv, gz, gates, gates_t = _inproj(
            xa, modsel, norm1_g[l][None, :], w_r, w_t, cos_t, sin_t, qg, kg, bd, tm)

        ao = _attention(q, k, vt, ctx_len, tm, blk0)

        brow = _gate_row(mlstm_i_bias[l], GC_MI) + _gate_row(mlstm_f_bias[l], GC_MF)
        bcol = _gate_col(mlstm_i_bias[l], mlstm_f_bias[l])
        mhf, mhr = _mlstm_scan(mqkv, mkt, gates, gates_t, brow, bcol, ctx_len)

        conv12 = gdn_conv_w[l].reshape(CONV_K, 3 * GDN_HEADS, HEAD_DIM).transpose(1, 0, 2)
        gp = _gdn_prep(gqkv, conv12, ctx_len)
        neg_a = -jnp.exp(gdn_a_log[l].astype(F32))
        ghf, ghr = _gdn_scan(gp, gates, gates_t, _gate_row(neg_a, GC_GA), _gate_row(gdn_dt_bias[l], GC_GA),
                             _gate_col(neg_a, zeros8), _gate_col(gdn_dt_bias[l], zeros8), ctx_len)

        w_hm = w_out[l].astype(BF16)
        x1, hp, aff = _outproj(xa, modsel, ao, mhf, mhr, mo, mlstm_out_g[l], ghf, ghr, gz, gdn_out_g[l][None, :],
                               w_hm, norm2_g[l][None, :], router_w[l].T, tm, blk0)

        idx_x, val_x = _route(aff[:, :, ctx_len:].reshape(b, ne, seq // LANES, LANES), cap_x)
        idx_xf = idx_x.reshape(-1)
        xgs = [_gather(idx_xf, hp, cap_x, t, ctx_len)]
        vals = [val_x.transpose(1, 0, 2, 3).reshape(ne, b * cap_x, 1)]
        if need_ctx:
            aff_c = jnp.pad(aff[:, :, :ctx_len], ((0, 0), (0, 0), (0, rows_c * LANES - ctx_len)), constant_values=-1.0)
            idx_c, val_c = _route(aff_c.reshape(b, ne, rows_c, LANES), cap_c)
            idx_cf = idx_c.reshape(-1)
            xgs.append(_gather(idx_cf, hp, cap_c, ctx_len, 0))
            vals.append(val_c.transpose(1, 0, 2, 3).reshape(ne, b * cap_c, 1))
        ys = _ffn(xgs, vals, w1, w3, w2, l, min(FFN_TF, w1.shape[3]))
        moe_x = _combine(idx_xf, ys[0], b, seq, cap_x)
        moe_c = _combine(idx_cf, ys[1], b, ctx_len, cap_c) if need_ctx else moe_x
        xa = _residual(x1, modsel, moe_x, moe_c, tm, blk0)
    return xa
```

```python
import functools
import math

import jax
import jax.numpy as jnp
from jax import lax
from jax.experimental import pallas as pl
from jax.experimental.pallas import tpu as pltpu

F32 = jnp.float32
BF16 = jnp.bfloat16
I32 = jnp.int32
U32 = jnp.uint32
HIGHEST = lax.Precision.HIGHEST

HEAD_DIM = 64
ATTN_HEADS = 8
ATTN_KV_HEADS = 2
ATTN_REP = ATTN_HEADS // ATTN_KV_HEADS
MLSTM_HEADS = 4
GDN_HEADS = 4
N_DIR = 2
CHUNK = 64
CONV_K = 5
GRID_W = 64
ROPE_THETA = 10000.0
N_EXPERTS = 16
CAPACITY_FACTOR = 2
N_MOD = 6
EPS = 1e-6
ATTN_W = ATTN_HEADS * HEAD_DIM
KV_W = ATTN_KV_HEADS * HEAD_DIM
MLSTM_W = MLSTM_HEADS * HEAD_DIM
GDN_W = GDN_HEADS * HEAD_DIM
LANES = 128
SUBLANES = 8
GATE_W = LANES
GATE_ROWS = 32
GC_MI, GC_MF, GC_GA, GC_GB = 0, 8, 16, 24
VMEM_LIMIT = 56 * 1024 * 1024
ATTN_COLS = 512
ATTN_VPAD = 16
LOG2E = 1.4426950408889634
FFN_TF = 512
CONV_ROWS = 256
PAD_ROWS = 8
COMBINE_GROUP = 8


def _cparams(sem):
    return pltpu.CompilerParams(dimension_semantics=sem, vmem_limit_bytes=VMEM_LIMIT)


def _sigmoid(x):
    return 1.0 / (1.0 + jnp.exp(-x))


def _silu(x):
    return x * _sigmoid(x)


def _log_sigmoid(x):
    return jnp.minimum(x, 0.0) - jnp.log1p(jnp.exp(-jnp.abs(x)))


def _softplus(x):
    return jnp.maximum(x, 0.0) + jnp.log1p(jnp.exp(-jnp.abs(x)))


def _dot(a, b, precision=None):
    return jnp.dot(a, b, preferred_element_type=F32, precision=precision)


def _dot_nt(a, b, precision=None):
    return lax.dot_general(a, b, (((1,), (1,)), ((), ())), preferred_element_type=F32, precision=precision)


def _split_bf16(a):
    hi = a.astype(BF16)
    lo = (a - hi.astype(F32)).astype(BF16)
    return hi, lo


def _split_bf16_3(a):
    hi = a.astype(BF16)
    r = a - hi.astype(F32)
    mid = r.astype(BF16)
    return hi, mid, (r - mid.astype(F32)).astype(BF16)


def _dot3(a, b):
    ah, al = _split_bf16(a)
    bh, bl = _split_bf16(b)
    return _dot(ah, bh) + (_dot(ah, bl) + _dot(al, bh))


def _iota(shape, dim):
    return lax.broadcasted_iota(I32, shape, dim)


def _eye_rows(rows, cols, first):
    return (_iota((rows, cols), 0) + first == _iota((rows, cols), 1)).astype(F32)


def _mod_kernel(c_ref, w_ref, b_ref, o_ref):
    s = _silu(c_ref[...])
    o_ref[0] = _dot(s, w_ref[0], precision=HIGHEST) + b_ref[0]


def _modulation(cvec, mod_w, mod_b):
    depth, d, n = mod_w.shape
    rows = cvec.shape[0]
    tn = d
    return pl.pallas_call(
        _mod_kernel,
        grid=(depth, n // tn),
        in_specs=[
            pl.BlockSpec((rows, d), lambda l, j: (0, 0)),
            pl.BlockSpec((1, d, tn), lambda l, j: (l, 0, j)),
            pl.BlockSpec((1, 1, tn), lambda l, j: (l, 0, j)),
        ],
        out_specs=pl.BlockSpec((1, rows, tn), lambda l, j: (l, 0, j)),
        out_shape=jax.ShapeDtypeStruct((depth, rows, n), F32),
        compiler_params=_cparams(("parallel", "parallel")),
        name="modulation",
    )(cvec, mod_w, mod_b.reshape(depth, 1, n))


def _inproj_kernel(x_ref, mod_ref, g_ref, w_ref, wt_ref, cos_ref, sin_ref, qg_ref, kg_ref, bd_ref,
                   q_ref, k_ref, vt_ref, ml_ref, mkt_ref, mo_ref, gd_ref, gz_ref, gate_ref, gatet_ref, *, d_model):
    x = x_ref[0]
    tm = x.shape[0]
    sh = mod_ref[:, 0:d_model]
    sc = mod_ref[:, d_model:2 * d_model]
    xn = x * lax.rsqrt(jnp.mean(x * x, axis=-1, keepdims=True) + EPS) * g_ref[...]
    h = (xn * (1.0 + sc) + sh).astype(BF16)
    p = _dot(h, w_ref[...])
    pt = _dot_nt(wt_ref[...], h)
    for g in range(ATTN_KV_HEADS):
        vt_ref[0, g, 0:HEAD_DIM, :] = pt[g * HEAD_DIM:(g + 1) * HEAD_DIM].astype(BF16)
        vt_ref[0, g, HEAD_DIM:, :] = (_iota((ATTN_VPAD, tm), 0) == 0).astype(BF16)
    for c in range(tm // CHUNK):
        cs = slice(c * CHUNK, (c + 1) * CHUNK)
        for j in range(MLSTM_HEADS):
            mkt_ref[0, j, c] = pt[KV_W + j * HEAD_DIM:KV_W + (j + 1) * HEAD_DIM, cs] * HEAD_DIM ** -0.5
        gatet_ref[0, c] = pt[KV_W + MLSTM_W:KV_W + MLSTM_W + GATE_ROWS, cs]

    cos = cos_ref[...]
    sin = sin_ref[...]
    first_half = (_iota(cos.shape, 1) % 32) < 16

    def norm_rope(xs, g, scale):
        ms = sum(_dot(piece, bd_ref[...]) for piece in _split_bf16(xs * xs))
        xn_ = xs * lax.rsqrt(ms + EPS) * g
        sw = jnp.where(first_half, pltpu.roll(xn_, LANES - 16, 1), pltpu.roll(xn_, 16, 1))
        return (xn_ * cos + sw * sin) * scale

    for j in range(ATTN_W // LANES):
        qs = norm_rope(p[:, j * LANES:(j + 1) * LANES], qg_ref[...], LOG2E * HEAD_DIM ** -0.5).astype(BF16)
        q_ref[0, 2 * j] = qs[:, 0:HEAD_DIM]
        q_ref[0, 2 * j + 1] = qs[:, HEAD_DIM:LANES]
    ks = norm_rope(p[:, ATTN_W:ATTN_W + KV_W], kg_ref[...], 1.0).astype(BF16)
    k_ref[0, 0] = ks[:, 0:HEAD_DIM]
    k_ref[0, 1] = ks[:, HEAD_DIM:LANES]

    off = ATTN_W + KV_W

    def head(j):
        return p[:, off + j * HEAD_DIM: off + (j + 1) * HEAD_DIM]

    for j in range(12):
        ml_ref[0, j] = head(j) * HEAD_DIM ** -0.5 if 4 <= j < 8 else head(j)
    for j in range(4):
        mo_ref[0, j] = head(12 + j)
    off += 4 * MLSTM_W
    for j in range(12):
        gd_ref[0, j] = head(j)
    for j in range(4):
        gz_ref[0, j] = head(12 + j)
    off += 4 * GDN_W
    gate_ref[0] = p[:, off:off + GATE_W]


def _inproj(xa, modsel, g1, w_r, w_t, cos_t, sin_t, qg, kg, bd, tm):
    b, t, d = xa.shape
    nb = t // tm
    nc = tm // CHUNK
    kern = functools.partial(_inproj_kernel, d_model=d)
    hm_shape = lambda nh, dt: jax.ShapeDtypeStruct((b, nh, t, HEAD_DIM), dt)
    out_shapes = (hm_shape(ATTN_HEADS, BF16), hm_shape(ATTN_KV_HEADS, BF16),
                  jax.ShapeDtypeStruct((b, ATTN_KV_HEADS, HEAD_DIM + ATTN_VPAD, t), BF16),
                  hm_shape(12, F32), jax.ShapeDtypeStruct((b, MLSTM_HEADS, t // CHUNK, HEAD_DIM, CHUNK), F32),
                  hm_shape(4, F32), hm_shape(12, F32), hm_shape(4, F32),
                  jax.ShapeDtypeStruct((b, t, GATE_W), F32),
                  jax.ShapeDtypeStruct((b, t // CHUNK, GATE_ROWS, CHUNK), F32))
    hm = lambda nh: pl.BlockSpec((1, nh, tm, HEAD_DIM), lambda bi, i: (bi, 0, i, 0))
    const = lambda a: pl.BlockSpec(a.shape, lambda bi, i: (0,) * a.ndim)
    return pl.pallas_call(
        kern,
        grid=(b, nb),
        in_specs=[
            pl.BlockSpec((1, tm, d), lambda bi, i: (bi, i, 0)),
            pl.BlockSpec((None, None, 1, N_MOD * d), lambda bi, i: (bi, jnp.minimum(i, 1), 0, 0)),
            const(g1), const(w_r), const(w_t),
            pl.BlockSpec((tm, LANES), lambda bi, i: (i, 0)),
            pl.BlockSpec((tm, LANES), lambda bi, i: (i, 0)),
            const(qg), const(kg), const(bd),
        ],
        out_specs=(hm(ATTN_HEADS), hm(ATTN_KV_HEADS),
                   pl.BlockSpec((1, ATTN_KV_HEADS, HEAD_DIM + ATTN_VPAD, tm), lambda bi, i: (bi, 0, 0, i)),
                   hm(12), pl.BlockSpec((1, MLSTM_HEADS, nc, HEAD_DIM, CHUNK), lambda bi, i: (bi, 0, i, 0, 0)),
                   hm(4), hm(12), hm(4),
                   pl.BlockSpec((1, tm, GATE_W), lambda bi, i: (bi, i, 0)),
                   pl.BlockSpec((1, nc, GATE_ROWS, CHUNK), lambda bi, i: (bi, i, 0, 0))),
        out_shape=out_shapes,
        compiler_params=_cparams(("parallel", "parallel")),
        name="inproj",
    )(xa, modsel, g1, w_r, w_t, cos_t, sin_t, qg, kg, bd)


def _attn_kernel(q_ref, k_ref, vt_ref, o_ref, sa_sc, sb_sc, xa_sc, xb_sc, m_sc, acc_sc, *, tq, tk, n_pairs, blk0):
    i = pl.program_id(2) + blk0
    q = q_ref[0].reshape(ATTN_REP * tq, HEAD_DIM)
    cols = [slice(c * ATTN_COLS, (c + 1) * ATTN_COLS) for c in range(ATTN_REP * tq // ATTN_COLS)]

    m_sc[...] = jnp.full(m_sc.shape, -jnp.inf, F32)
    acc_sc[...] = jnp.zeros(acc_sc.shape, F32)

    def score(s_ref, x_ref, tile):
        start = pl.multiple_of(tile * tk, tk)
        k = k_ref[0, 0, pl.ds(start, tk), :]
        for cs in cols:
            s = _dot_nt(k, q[cs])
            s_ref[:, cs] = s
            x_ref[:, cs] = jnp.max(s, axis=0, keepdims=True)

    def consume(s_ref, x_ref, tile, half):
        start = pl.multiple_of(tile * tk, tk)
        vt = vt_ref[0, 0, :, pl.ds(start, tk)]
        for cs in cols:
            mo = m_sc[half, :, cs]
            mn = jnp.maximum(mo, x_ref[:, cs])
            p = jnp.exp2(s_ref[:, cs] - mn)
            acc_sc[half, :, cs] = jnp.exp2(mo - mn) * acc_sc[half, :, cs] + _dot(vt, p.astype(BF16))
            m_sc[half, :, cs] = mn

    score(sa_sc, xa_sc, 0)

    def body(j, carry):
        score(sb_sc, xb_sc, 2 * j + 1)
        consume(sa_sc, xa_sc, 2 * j, 0)
        score(sa_sc, xa_sc, 2 * j + 2)
        consume(sb_sc, xb_sc, 2 * j + 1, 1)
        return carry

    n = jnp.where(i == 0, 0, n_pairs)
    lax.fori_loop(0, n, body, 0)
    consume(sa_sc, xa_sc, 2 * n, 0)
    m = jnp.maximum(m_sc[0], m_sc[1])
    acc = jnp.exp2(m_sc[0] - m) * acc_sc[0] + jnp.exp2(m_sc[1] - m) * acc_sc[1]
    o_t = (acc[0:HEAD_DIM, :] / acc[HEAD_DIM:HEAD_DIM + 1, :]).astype(BF16)
    eye = (_iota((tq, tq), 0) == _iota((tq, tq), 1)).astype(BF16)
    for r in range(ATTN_REP):
        o_ref[0, r] = _dot_nt(eye, o_t[:, r * tq:(r + 1) * tq]).astype(o_ref.dtype)


def _attention(q, k, vt, ctx_len, tq, blk0):
    b, _, t, _ = q.shape
    tk = ctx_len
    assert (t // tk) % 2 == 1
    nq = t // tq - blk0
    kern = functools.partial(_attn_kernel, tq=tq, tk=tk, n_pairs=(t // tk - 1) // 2, blk0=blk0)
    rows = ATTN_REP * tq
    return pl.pallas_call(
        kern,
        grid=(b, ATTN_KV_HEADS, nq),
        in_specs=[
            pl.BlockSpec((1, ATTN_REP, tq, HEAD_DIM), lambda bi, g, i: (bi, g, i + blk0, 0)),
            pl.BlockSpec((1, 1, t, HEAD_DIM), lambda bi, g, i: (bi, g, 0, 0)),
            pl.BlockSpec((1, 1, HEAD_DIM + ATTN_VPAD, t), lambda bi, g, i: (bi, g, 0, 0)),
        ],
        out_specs=pl.BlockSpec((1, ATTN_REP, tq, HEAD_DIM), lambda bi, g, i: (bi, g, i + blk0, 0)),
        out_shape=jax.ShapeDtypeStruct((b, ATTN_HEADS, t, HEAD_DIM), BF16),
        scratch_shapes=[pltpu.VMEM((tk, rows), F32), pltpu.VMEM((tk, rows), F32),
                        pltpu.VMEM((1, rows), F32), pltpu.VMEM((1, rows), F32),
                        pltpu.VMEM((2, 1, rows), F32), pltpu.VMEM((2, HEAD_DIM + ATTN_VPAD, rows), F32)],
        compiler_params=_cparams(("parallel", "parallel", "arbitrary")),
        name="attention",
    )(q, k, vt)


def _chunk_maps(nc_c, nc_x):
    fwd = lambda j: j
    rev = lambda j: jnp.where(j < nc_c, nc_c - 1 - j, 2 * nc_c + nc_x - 1 - j)
    return fwd, rev


def _dir_masks(d):
    r = _iota((CHUNK, CHUNK), 0)
    c = _iota((CHUNK, CHUNK), 1)
    incl = (r >= c) if d == 0 else (r <= c)
    strict = (r > c) if d == 0 else (r < c)
    incl_t = (r <= c) if d == 0 else (r >= c)
    return incl, strict, incl.astype(F32), incl_t.astype(F32)


def _mlstm_kernel(xf_ref, xr_ref, ktf_ref, ktr_ref, gf_ref, gr_ref, gtf_ref, gtr_ref, brow_ref, bcol_ref,
                  hf_ref, hr_ref, c_sc, n_sc, m_sc, *, nb):
    @pl.when(pl.program_id(0) == 0)
    def _():
        c_sc[...] = jnp.zeros(c_sc.shape, F32)
        n_sc[...] = jnp.zeros(n_sc.shape, F32)
        m_sc[...] = jnp.zeros(m_sc.shape, F32)

    refs = ((xf_ref, ktf_ref, gf_ref, gtf_ref, hf_ref), (xr_ref, ktr_ref, gr_ref, gtr_ref, hr_ref))
    masks = [_dir_masks(d) for d in range(N_DIR)]
    gate = {}
    for d in range(N_DIR):
        _, _, g_ref, gt_ref, _ = refs[d]
        _, _, tri, tri_t = masks[d]
        sel = (_iota((GATE_W, MLSTM_HEADS * LANES), 0)
               == GC_MF + d * MLSTM_HEADS + _iota((GATE_W, MLSTM_HEADS * LANES), 1) // LANES).astype(BF16)
        for b in range(nb):
            cum = _dot(tri, _log_sigmoid(g_ref[b] + brow_ref[...]), precision=HIGHEST)
            g_t = gt_ref[b, 0, GC_MI:GC_MI + 16, :] + bcol_ref[...]
            lf_t = _log_sigmoid(g_t)
            cum_rep = sum(_dot(piece, sel) for piece in _split_bf16_3(cum))
            gate[d, b] = (cum_rep, g_t, lf_t, _dot(lf_t, tri_t, precision=HIGHEST))

    probs = [(d, b, h) for d in range(N_DIR) for b in range(nb) for h in range(MLSTM_HEADS)]
    st = []
    for d, b, h in probs:
        p = (d * nb + b) * MLSTM_HEADS + h
        x_ref, kt_ref = refs[d][0], refs[d][1]
        cum_rep, g_t, lf_t, cum_t = gate[d, b]
        ri = d * MLSTM_HEADS + h
        rf = GC_MF + ri
        bcum_col = cum_rep[:, h * LANES:h * LANES + CHUNK]
        bcum_row = cum_t[rf - GC_MI:rf - GC_MI + 1, :]
        i_row = g_t[ri:ri + 1, :]
        b_last = jnp.sum(lf_t[rf - GC_MI:rf - GC_MI + 1, :], axis=-1, keepdims=True)
        m_old = m_sc[p, 0:1, 0:1]
        w_end = b_last - bcum_row + i_row
        m_new = jnp.maximum(b_last + m_old, jnp.max(w_end, axis=-1, keepdims=True))
        a_row = jnp.exp(w_end - m_new)
        dec = jnp.exp(b_last + m_old - m_new)
        dmat = jnp.where(masks[d][0], bcum_col - bcum_row + i_row, -jnp.inf)
        inter = bcum_col + m_sc[p, 0:1, 0:CHUNK]
        m_t = jnp.maximum(inter, jnp.max(dmat, axis=-1, keepdims=True))
        st.append(dict(p=p, q=x_ref[b, h], k=x_ref[b, MLSTM_HEADS + h], v=x_ref[b, 2 * MLSTM_HEADS + h],
                       k_t=kt_ref[b, h, 0], c_old=c_sc[p], n_old=n_sc[p], a_row=a_row, dec=dec, m_new=m_new,
                       m_t=m_t, w_in=jnp.exp(inter - m_t), dexp=jnp.exp(dmat - m_t), out=refs[d][4], b=b, h=h))
    for e in st:
        e["s"] = _dot(e["q"], e["k_t"]) * e["dexp"]
    for e in st:
        e["qc"] = _dot(e["q"], e["c_old"])
    for e in st:
        e["sv"] = _dot(e["s"], e["v"])
    for e in st:
        e["kv"] = _dot(e["k_t"] * e["a_row"], e["v"])
    for e in st:
        e["ak"] = _dot(jnp.broadcast_to(e["a_row"], (SUBLANES, CHUNK)), e["k"])
    for e in st:
        num = e["w_in"] * e["qc"] + e["sv"]
        den = (e["w_in"] * jnp.sum(e["q"] * e["n_old"][0:1, :], axis=-1, keepdims=True)
               + jnp.sum(e["s"], axis=-1, keepdims=True))
        e["out"][e["b"], e["h"]] = num / jnp.maximum(jnp.abs(den), jnp.exp(-e["m_t"]))
    for e in st:
        p = e["p"]
        c_sc[p] = e["dec"] * e["c_old"] + e["kv"]
        n_sc[p] = e["dec"] * e["n_old"] + e["ak"]
        m_sc[p] = jnp.broadcast_to(e["m_new"], (SUBLANES, LANES))


def _mlstm_scan(mqkv, k_t, gates, gates_t, brow, bcol, ctx_len):
    b, _, t, _ = mqkv.shape
    nc_c, nc_x = ctx_len // CHUNK, (t - ctx_len) // CHUNK
    fwd, rev = _chunk_maps(nc_c, nc_x)
    nprob = N_DIR * b * MLSTM_HEADS
    xs = lambda f: pl.BlockSpec((b, 12, CHUNK, HEAD_DIM), lambda j: (0, 0, f(j), 0))
    ks = lambda f: pl.BlockSpec((b, MLSTM_HEADS, 1, HEAD_DIM, CHUNK), lambda j: (0, 0, f(j), 0, 0))
    gs = lambda f: pl.BlockSpec((b, CHUNK, GATE_W), lambda j: (0, f(j), 0))
    gts = lambda f: pl.BlockSpec((b, 1, GATE_ROWS, CHUNK), lambda j: (0, f(j), 0, 0))
    hs = lambda f: pl.BlockSpec((b, MLSTM_HEADS, CHUNK, HEAD_DIM), lambda j: (0, 0, f(j), 0))
    out = jax.ShapeDtypeStruct((b, MLSTM_HEADS, t, HEAD_DIM), F32)
    return pl.pallas_call(
        functools.partial(_mlstm_kernel, nb=b),
        grid=(nc_c + nc_x,),
        in_specs=[xs(fwd), xs(rev), ks(fwd), ks(rev), gs(fwd), gs(rev), gts(fwd), gts(rev),
                  pl.BlockSpec((1, GATE_W), lambda j: (0, 0)),
                  pl.BlockSpec((16, CHUNK), lambda j: (0, 0))],
        out_specs=(hs(fwd), hs(rev)),
        out_shape=(out, out),
        scratch_shapes=[pltpu.VMEM((nprob, HEAD_DIM, HEAD_DIM), F32),
                        pltpu.VMEM((nprob, SUBLANES, HEAD_DIM), F32),
                        pltpu.VMEM((nprob, SUBLANES, LANES), F32)],
        compiler_params=_cparams(("arbitrary",)),
        name="mlstm_scan",
    )(mqkv, mqkv, k_t, k_t, gates, gates, gates_t, gates_t, brow, bcol)


def _gdn_prep_kernel(x_ref, w_ref, o_ref, pad_sc, *, ctx_len, t):
    part = pl.program_id(1) // GDN_HEADS
    zeros = jnp.zeros((PAD_ROWS, HEAD_DIM), F32)
    pad_sc[0:PAD_ROWS] = zeros
    pad_sc[PAD_ROWS:PAD_ROWS + ctx_len] = x_ref[0, 0, 0:ctx_len]
    pad_sc[PAD_ROWS + ctx_len:2 * PAD_ROWS + ctx_len] = zeros
    pad_sc[2 * PAD_ROWS + ctx_len:2 * PAD_ROWS + t] = x_ref[0, 0, ctx_len:t]
    pad_sc[2 * PAD_ROWS + t:3 * PAD_ROWS + t] = zeros
    w = w_ref[0]
    is_qk = part < 2
    scale = jnp.where(part == 0, HEAD_DIM ** -0.5, 1.0)
    for c in range(t // CONV_ROWS):
        r0 = c * CONV_ROWS
        base = r0 + (PAD_ROWS if r0 < ctx_len else 2 * PAD_ROWS) - CONV_K // 2
        y = w[0:1, :] * pad_sc[base:base + CONV_ROWS]
        for j in range(1, CONV_K):
            y = y + w[j:j + 1, :] * pad_sc[base + j:base + j + CONV_ROWS]
        y = _silu(y)
        yn = y * lax.rsqrt(jnp.sum(y * y, axis=-1, keepdims=True) + EPS) * scale
        o_ref[0, 0, r0:r0 + CONV_ROWS] = jnp.where(is_qk, yn, y)


def _gdn_prep(gqkv, conv_w12, ctx_len):
    b, np_, t, _ = gqkv.shape
    return pl.pallas_call(
        functools.partial(_gdn_prep_kernel, ctx_len=ctx_len, t=t),
        grid=(b, np_),
        in_specs=[pl.BlockSpec((1, 1, t, HEAD_DIM), lambda bi, p: (bi, p, 0, 0)),
                  pl.BlockSpec((1, CONV_K, HEAD_DIM), lambda bi, p: (p, 0, 0))],
        out_specs=pl.BlockSpec((1, 1, t, HEAD_DIM), lambda bi, p: (bi, p, 0, 0)),
        out_shape=jax.ShapeDtypeStruct(gqkv.shape, F32),
        scratch_shapes=[pltpu.VMEM((t + 3 * PAD_ROWS, HEAD_DIM), F32)],
        compiler_params=_cparams(("parallel", "parallel")),
        name="gdn_prep",
    )(gqkv, conv_w12)


def _gdn_kernel(xf_ref, xr_ref, gf_ref, gr_ref, gtf_ref, gtr_ref, arow_ref, drow_ref, acol_ref, dcol_ref,
                of_ref, or_ref, s_sc, rhs_sc, *, nb):
    @pl.when(pl.program_id(0) == 0)
    def _():
        s_sc[...] = jnp.zeros(s_sc.shape, F32)

    refs = ((xf_ref, gf_ref, gtf_ref, of_ref), (xr_ref, gr_ref, gtr_ref, or_ref))
    masks = [_dir_masks(d) for d in range(N_DIR)]
    gate = {}
    for d in range(N_DIR):
        g_ref, gt_ref = refs[d][1], refs[d][2]
        _, _, tri, tri_t = masks[d]
        for b in range(nb):
            g = g_ref[b]
            gval = arow_ref[...] * _softplus(g + drow_ref[...])
            g_t = gt_ref[b, 0, GC_GA:GC_GA + 16, :]
            gval_t = acol_ref[...] * _softplus(g_t + dcol_ref[...])
            gate[d, b] = (_dot(tri, gval, precision=HIGHEST), _sigmoid(g), gval_t,
                          _dot(gval_t, tri_t, precision=HIGHEST))

    probs = [(d, b, h) for d in range(N_DIR) for b in range(nb) for h in range(GDN_HEADS)]
    eye = _eye_rows(HEAD_DIM, HEAD_DIM, 0).astype(BF16)
    st = []
    for d, b, h in probs:
        x_ref = refs[d][0]
        st.append(dict(p=(d * nb + b) * GDN_HEADS + h, d=d, b=b, h=h, q=x_ref[b, h], k=x_ref[b, GDN_HEADS + h],
                       v=x_ref[b, 2 * GDN_HEADS + h]))
    for e in st:
        e["k_t"] = sum(_dot_nt(eye, piece) for piece in _split_bf16(e["k"]))
    for e in st:
        d, b, h, p = e["d"], e["b"], e["h"], e["p"]
        incl, strict, _, _ = masks[d]
        gcum, beta_all, gval_t, gcum_t = gate[d, b]
        ra = d * GDN_HEADS + h
        g_col = gcum[:, GC_GA + ra:GC_GA + ra + 1]
        g_row = gcum_t[ra:ra + 1, :]
        beta = beta_all[:, GC_GB + ra:GC_GB + ra + 1]
        g_last = jnp.sum(gval_t[ra:ra + 1, :], axis=-1, keepdims=True)
        eg = jnp.exp(g_col)
        kb = e["k"] * beta
        rhs_sc[p, :, 0:HEAD_DIM] = e["v"] * beta
        rhs_sc[p, :, HEAD_DIM:] = kb * eg
        e.update(out=refs[d][3], strict=strict, eg_last=jnp.exp(g_last), kb=kb, qg=e["q"] * eg,
                 decay=jnp.where(incl, jnp.exp(jnp.where(incl, g_col - g_row, 0.0)), 0.0),
                 ktg=e["k_t"] * jnp.exp(g_last - g_row), s_old=s_sc[p])
    for e in st:
        e["sol"] = rhs_sc[e["p"]]
    for e in st:
        e["pw"] = -jnp.where(e["strict"], _dot3(e["kb"], e["k_t"]) * e["decay"], 0.0)
    n_fac = CHUNK.bit_length() - 1
    for it in range(n_fac):
        for e in st:
            e["sol"] = e["sol"] + _dot3(e["pw"], e["sol"])
        if it < n_fac - 1:
            for e in st:
                e["pw"] = _dot3(e["pw"], e["pw"])
    for e in st:
        e["ws"] = _dot(e["sol"][:, HEAD_DIM:], e["s_old"])
    for e in st:
        e["qs"] = _dot(e["qg"], e["s_old"])
    for e in st:
        e["qk"] = _dot(e["q"], e["k_t"]) * e["decay"]
    for e in st:
        e["v_new"] = e["sol"][:, :HEAD_DIM] - e["ws"]
    for e in st:
        e["out"][e["b"], e["h"]] = e["qs"] + _dot(e["qk"], e["v_new"])
    for e in st:
        s_sc[e["p"]] = e["eg_last"] * e["s_old"] + _dot(e["ktg"], e["v_new"])


def _gdn_scan(gp, gates, gates_t, arow, drow, acol, dcol, ctx_len):
    b, _, t, _ = gp.shape
    nc_c, nc_x = ctx_len // CHUNK, (t - ctx_len) // CHUNK
    fwd, rev = _chunk_maps(nc_c, nc_x)
    nprob = N_DIR * b * GDN_HEADS
    xs = lambda f: pl.BlockSpec((b, 3 * GDN_HEADS, CHUNK, HEAD_DIM), lambda j: (0, 0, f(j), 0))
    gs = lambda f: pl.BlockSpec((b, CHUNK, GATE_W), lambda j: (0, f(j), 0))
    gts = lambda f: pl.BlockSpec((b, 1, GATE_ROWS, CHUNK), lambda j: (0, f(j), 0, 0))
    hs = lambda f: pl.BlockSpec((b, GDN_HEADS, CHUNK, HEAD_DIM), lambda j: (0, 0, f(j), 0))
    row = pl.BlockSpec((1, GATE_W), lambda j: (0, 0))
    col = pl.BlockSpec((16, CHUNK), lambda j: (0, 0))
    out = jax.ShapeDtypeStruct((b, GDN_HEADS, t, HEAD_DIM), F32)
    return pl.pallas_call(
        functools.partial(_gdn_kernel, nb=b),
        grid=(nc_c + nc_x,),
        in_specs=[xs(fwd), xs(rev), gs(fwd), gs(rev), gts(fwd), gts(rev), row, row, col, col],
        out_specs=(hs(fwd), hs(rev)),
        out_shape=(out, out),
        scratch_shapes=[pltpu.VMEM((nprob, HEAD_DIM, HEAD_DIM), F32),
                        pltpu.VMEM((nprob, CHUNK, 2 * HEAD_DIM), F32)],
        compiler_params=_cparams(("arbitrary",)),
        name="gdn_scan",
    )(gp, gp, gates, gates, gates_t, gates_t, arow, drow, acol, dcol)


def _outproj_kernel(x_ref, mod_ref, ao_ref, mhf_ref, mhr_ref, mo_ref, mg_ref, ghf_ref, ghr_ref, gz_ref, gg_ref,
                    w_ref, n2_ref, rw_ref, x1_ref, hp_ref, aff_ref, mix_sc, *, d_model):
    d = d_model

    def put(j, val):
        mix_sc[:, j * HEAD_DIM:(j + 1) * HEAD_DIM] = val.astype(BF16)

    for h in range(ATTN_HEADS):
        put(h, ao_ref[0, h])
    for h in range(MLSTM_HEADS):
        hh = mhf_ref[0, h] + mhr_ref[0, h]
        hn = hh * lax.rsqrt(jnp.mean(hh * hh, axis=-1, keepdims=True) + EPS) * mg_ref[h:h + 1, :]
        put(ATTN_HEADS + h, _sigmoid(mo_ref[0, h]) * hn)
    for h in range(GDN_HEADS):
        oo = ghf_ref[0, h] + ghr_ref[0, h]
        on = oo * lax.rsqrt(jnp.mean(oo * oo, axis=-1, keepdims=True) + EPS) * gg_ref[...]
        put(ATTN_HEADS + MLSTM_HEADS + h, on * _silu(gz_ref[0, h]))
    x1 = x_ref[0] + mod_ref[:, 2 * d:3 * d] * _dot(mix_sc[...], w_ref[...])
    x1_ref[0] = x1
    xn = x1 * lax.rsqrt(jnp.mean(x1 * x1, axis=-1, keepdims=True) + EPS) * n2_ref[...]
    h2 = xn * (1.0 + mod_ref[:, 4 * d:5 * d]) + mod_ref[:, 3 * d:4 * d]
    logits = _dot_nt(rw_ref[...], h2, precision=HIGHEST)
    e = jnp.exp(logits - jnp.max(logits, axis=0, keepdims=True))
    aff_ref[0] = e / jnp.sum(e, axis=0, keepdims=True)
    hp_ref[0] = h2


def _outproj(xa, modsel, ao, mhf, mhr, mo, mg, ghf, ghr, gz, gg, w_hm, n2, rw_t, tm, blk0):
    b, t, d = xa.shape
    nb = t // tm - blk0
    hm = lambda nh: pl.BlockSpec((1, nh, tm, HEAD_DIM), lambda bi, i: (bi, 0, i + blk0, 0))
    full = lambda a: pl.BlockSpec(a.shape, lambda bi, i: (0,) * a.ndim)
    return pl.pallas_call(
        functools.partial(_outproj_kernel, d_model=d),
        grid=(b, nb),
        in_specs=[
            pl.BlockSpec((1, tm, d), lambda bi, i: (bi, i + blk0, 0)),
            pl.BlockSpec((None, None, 1, N_MOD * d), lambda bi, i: (bi, jnp.minimum(i + blk0, 1), 0, 0)),
            hm(ATTN_HEADS), hm(4), hm(4), hm(4), full(mg), hm(4), hm(4), hm(4), full(gg),
            full(w_hm), full(n2), full(rw_t),
        ],
        out_specs=(pl.BlockSpec((1, tm, d), lambda bi, i: (bi, i + blk0, 0)),
                   pl.BlockSpec((1, tm, d), lambda bi, i: (bi, i + blk0, 0)),
                   pl.BlockSpec((1, N_EXPERTS, tm), lambda bi, i: (bi, 0, i + blk0))),
        out_shape=(jax.ShapeDtypeStruct((b, t, d), F32),
                   jax.ShapeDtypeStruct((b, t, d), F32),
                   jax.ShapeDtypeStruct((b, N_EXPERTS, t), F32)),
        scratch_shapes=[pltpu.VMEM((tm, w_hm.shape[0]), BF16)],
        compiler_params=_cparams(("parallel", "parallel")),
        name="outproj",
    )(xa, modsel, ao, mhf, mhr, mo, mg, ghf, ghr, gz, gg, w_hm, n2, rw_t)


def _route_kernel(aff_ref, idx_ref, val_ref, *, cap, rows):
    ne = N_EXPERTS
    a = aff_ref[0]

    def count(mask):
        return jnp.sum(jnp.sum(mask.astype(I32), axis=2, keepdims=True), axis=1, keepdims=True)

    tau_bits = jnp.zeros((ne, 1, 1), I32)
    for bit in range(30, -1, -1):
        cand = tau_bits | (1 << bit)
        keep = count(a >= lax.bitcast_convert_type(cand, F32)) >= cap
        tau_bits = jnp.where(keep, cand, tau_bits)
    tau = lax.bitcast_convert_type(tau_bits, F32)
    gt = a > tau
    eq = a == tau
    need = cap - count(gt)

    triu = (_iota((LANES, LANES), 0) <= _iota((LANES, LANES), 1)).astype(BF16)
    strict_lower = (_iota((rows, rows), 0) > _iota((rows, rows), 1)).astype(BF16)
    triu_r = (_iota((rows, rows), 0) <= _iota((rows, rows), 1)).astype(BF16)
    ones_r = jnp.ones((SUBLANES, LANES), BF16)

    def prefix(mask2d):
        m = mask2d.astype(BF16)
        within = _dot(m, triu)
        tot = jnp.broadcast_to(within[:, LANES - 1:LANES], (rows, LANES)).astype(BF16)
        return within, _dot(strict_lower, tot)

    lane_r = _iota((cap, rows), 1).astype(F32)
    lane_l = _iota((cap, LANES), 1).astype(F32)
    slot = _iota((cap, 1), 0).astype(F32)
    for e in range(ne):
        eq_e = eq[e]
        w_eq, before_eq = prefix(eq_e)
        rank_eq = w_eq - eq_e.astype(F32) + before_eq
        sel = gt[e] | (eq_e & (rank_eq < need[e].astype(F32)))
        rel, _ = prefix(sel)
        sel_b = sel.astype(BF16)
        row_tot = _dot_nt(ones_r, sel_b)
        row_incl = _dot(row_tot.astype(BF16), triu_r)
        row_excl = row_incl - row_tot
        kstar = jnp.sum((row_incl[0:1, :] <= slot).astype(F32), axis=-1, keepdims=True)
        onehot = (lane_r == kstar).astype(F32)
        base = jnp.sum(onehot * row_excl[0:1, :], axis=-1, keepdims=True)
        g_rel = _dot(onehot.astype(BF16), rel.astype(BF16))
        within = jnp.sum((g_rel <= slot - base).astype(F32), axis=-1, keepdims=True)
        g_aff = _dot(onehot, a[e], precision=HIGHEST)
        val_ref[0, e] = jnp.sum(jnp.where(lane_l == within, g_aff, 0.0), axis=-1, keepdims=True)
        idx_ref[0, e] = (kstar * LANES + within).astype(I32)


def _route(aff_tiles, cap):
    b, ne, rows, _ = aff_tiles.shape
    return pl.pallas_call(
        functools.partial(_route_kernel, cap=cap, rows=rows),
        grid=(b,),
        in_specs=[pl.BlockSpec((1, ne, rows, LANES), lambda bi: (bi, 0, 0, 0))],
        out_specs=(pl.BlockSpec((1, ne, cap, 1), lambda bi: (bi, 0, 0, 0)),
                   pl.BlockSpec((1, ne, cap, 1), lambda bi: (bi, 0, 0, 0))),
        out_shape=(jax.ShapeDtypeStruct((b, ne, cap, 1), I32), jax.ShapeDtypeStruct((b, ne, cap, 1), F32)),
        compiler_params=_cparams(("parallel",)),
        name="route",
    )(aff_tiles)


def _gather_kernel(idx_ref, h_ref, o_ref, rows_sc, *, cap, row_off):
    base = (pl.program_id(0) * pl.num_programs(2) + pl.program_id(2)) * cap

    def body(s, carry):
        t = idx_ref[base + s] + row_off
        rows_sc[pl.ds(s, 1), :] = h_ref[0, pl.ds(t, 1), :]
        return carry

    lax.fori_loop(0, cap, body, 0, unroll=8)
    o_ref[0, 0] = rows_sc[...].astype(BF16)


def _gather(idx_flat, h2, cap, rows_block, row_off):
    b, _, d = h2.shape
    dh = d // 2
    return pl.pallas_call(
        functools.partial(_gather_kernel, cap=cap, row_off=row_off),
        grid_spec=pltpu.PrefetchScalarGridSpec(
            num_scalar_prefetch=1,
            grid=(b, 2, N_EXPERTS),
            in_specs=[pl.BlockSpec((1, rows_block, dh), lambda bi, c, e, idx: (bi, 0, c))],
            out_specs=pl.BlockSpec((1, 1, cap, dh), lambda bi, c, e, idx: (e, bi, 0, c)),
            scratch_shapes=[pltpu.VMEM((cap, dh), F32)],
        ),
        out_shape=jax.ShapeDtypeStruct((N_EXPERTS, b, cap, d), BF16),
        compiler_params=_cparams(("arbitrary", "arbitrary", "arbitrary")),
        name="moe_gather",
    )(idx_flat, h2)


def _ffn_kernel(*refs, n_streams):
    xg_refs = refs[0:n_streams]
    val_refs = refs[n_streams:2 * n_streams]
    w1_ref, w3_ref, w2_ref = refs[2 * n_streams:2 * n_streams + 3]
    y_refs = refs[2 * n_streams + 3:]
    f = pl.program_id(1)
    w1 = w1_ref[0].astype(BF16)
    w3 = w3_ref[0].astype(BF16)
    w2 = w2_ref[0].astype(BF16)
    for xg_ref, val_ref, y_ref in zip(xg_refs, val_refs, y_refs):
        cap = xg_ref.shape[2]
        for bi in range(xg_ref.shape[1]):
            rows = slice(bi * cap, (bi + 1) * cap)
            xg = xg_ref[0, bi]
            y = _dot((_silu(_dot(xg, w1)) * _dot(xg, w3)).astype(BF16), w2)

            @pl.when(f == 0)
            def _():
                y_ref[0, rows] = y

            @pl.when(f > 0)
            def _():
                y_ref[0, rows] = y_ref[0, rows] + y

            @pl.when(f == pl.num_programs(1) - 1)
            def _():
                y_ref[0, rows] = y_ref[0, rows] * val_ref[0, rows]


def _ffn(xgs, vals, w1, w3, w2, layer, tf):
    _, ne, d, ff = w1.shape
    n_streams = len(xgs)
    in_specs, out_specs, out_shapes = [], [], []
    for xg in xgs:
        in_specs.append(pl.BlockSpec((1,) + xg.shape[1:], lambda e, f: (e, 0, 0, 0)))
    for v in vals:
        in_specs.append(pl.BlockSpec((1,) + v.shape[1:], lambda e, f: (e, 0, 0)))
    in_specs += [pl.BlockSpec((None, 1, d, tf), lambda e, f: (layer, e, 0, f)),
                 pl.BlockSpec((None, 1, d, tf), lambda e, f: (layer, e, 0, f)),
                 pl.BlockSpec((None, 1, tf, d), lambda e, f: (layer, e, f, 0))]
    for xg in xgs:
        m = xg.shape[1] * xg.shape[2]
        out_specs.append(pl.BlockSpec((1, m, d), lambda e, f: (e, 0, 0)))
        out_shapes.append(jax.ShapeDtypeStruct((ne, m, d), F32))
    return pl.pallas_call(
        functools.partial(_ffn_kernel, n_streams=n_streams),
        grid=(ne, ff // tf),
        in_specs=in_specs,
        out_specs=tuple(out_specs),
        out_shape=tuple(out_shapes),
        compiler_params=_cparams(("parallel", "arbitrary")),
        name="moe_ffn",
    )(*xgs, *vals, w1, w3, w2)


def _combine_kernel(idx_ref, y_ref, o_ref, *, cap):
    e = pl.program_id(2)

    @pl.when(e == 0)
    def _():
        o_ref[...] = jnp.zeros(o_ref.shape, F32)

    base = (pl.program_id(0) * pl.num_programs(2) + e) * cap

    def body(g, carry):
        s0 = pl.multiple_of(g * COMBINE_GROUP, COMBINE_GROUP)
        ts = [idx_ref[base + s0 + r] for r in range(COMBINE_GROUP)]
        ys = y_ref[0, pl.ds(s0, COMBINE_GROUP), :]
        rows = [o_ref[0, pl.ds(t, 1), :] for t in ts]
        for r, t in enumerate(ts):
            o_ref[0, pl.ds(t, 1), :] = rows[r] + ys[r:r + 1, :]
        return carry

    lax.fori_loop(0, cap // COMBINE_GROUP, body, 0)


def _combine(idx_flat, y, b, n_tok, cap):
    ne, _, d = y.shape
    dh = d // 2
    return pl.pallas_call(
        functools.partial(_combine_kernel, cap=cap),
        grid_spec=pltpu.PrefetchScalarGridSpec(
            num_scalar_prefetch=1,
            grid=(b, 2, ne),
            in_specs=[pl.BlockSpec((1, cap, dh), lambda bi, c, e, idx: (e, bi, c))],
            out_specs=pl.BlockSpec((1, n_tok, dh), lambda bi, c, e, idx: (bi, 0, c)),
        ),
        out_shape=jax.ShapeDtypeStruct((b, n_tok, d), F32),
        compiler_params=_cparams(("arbitrary", "arbitrary", "arbitrary")),
        name="moe_combine",
    )(idx_flat, y)


def _residual_kernel(x1_ref, mod_ref, mx_ref, mc_ref, o_ref, *, d_model, blk0):
    i = pl.program_id(1) + blk0
    moe = jnp.where(i == 0, mc_ref[0], mx_ref[0])
    o_ref[0] = x1_ref[0] + mod_ref[:, 5 * d_model:6 * d_model] * moe


def _residual(x1, modsel, moe_x, moe_c, tm, blk0):
    b, t, d = x1.shape
    nb = t // tm - blk0
    return pl.pallas_call(
        functools.partial(_residual_kernel, d_model=d, blk0=blk0),
        grid=(b, nb),
        in_specs=[pl.BlockSpec((1, tm, d), lambda bi, i: (bi, i + blk0, 0)),
                  pl.BlockSpec((None, None, 1, N_MOD * d), lambda bi, i: (bi, jnp.minimum(i + blk0, 1), 0, 0)),
                  pl.BlockSpec((1, tm, d), lambda bi, i: (bi, jnp.maximum(i + blk0 - 1, 0), 0)),
                  pl.BlockSpec((1, tm, d), lambda bi, i: (bi, 0, 0))],
        out_specs=pl.BlockSpec((1, tm, d), lambda bi, i: (bi, i, 0)),
        out_shape=jax.ShapeDtypeStruct((b, nb * tm, d), F32),
        compiler_params=_cparams(("parallel", "parallel")),
        name="moe_residual",
    )(x1, modsel, moe_x, moe_c)


def _rope_tables(ctx_len, seq):
    n = jnp.arange(seq)
    pos = jnp.stack([n // GRID_W, n % GRID_W], axis=-1).astype(F32)
    lane = jnp.arange(LANES) % HEAD_DIM
    axis = lane // 32
    n_freq = HEAD_DIM // 4
    inv = ROPE_THETA ** (-(lane % n_freq).astype(F32) / n_freq)
    ang = pos[:, axis] * inv
    sign = jnp.where((lane % 32) < 16, -1.0, 1.0)
    cos_t = jnp.concatenate([jnp.ones((ctx_len, LANES), F32), jnp.cos(ang)], axis=0)
    sin_t = jnp.concatenate([jnp.zeros((ctx_len, LANES), F32), jnp.sin(ang) * sign], axis=0)
    return cos_t, sin_t


def _reorder_w_in(w):
    d = w.shape[0]
    qk = w[:, :ATTN_W + KV_W]
    v = w[:, ATTN_W + KV_W:ATTN_W + 2 * KV_W]
    o = ATTN_W + 2 * KV_W
    ml = w[:, o:o + 4 * MLSTM_W]
    o += 4 * MLSTM_W
    mgate = w[:, o:o + 2 * N_DIR * MLSTM_HEADS]
    o += 2 * N_DIR * MLSTM_HEADS
    gd = w[:, o:o + 4 * GDN_W]
    o += 4 * GDN_W
    ggate = w[:, o:o + 2 * N_DIR * GDN_HEADS]
    pad = jnp.zeros((d, GATE_W - mgate.shape[1] - ggate.shape[1]), w.dtype)
    w_r = jnp.concatenate([qk, ml, gd, mgate, ggate, pad], axis=1)
    w_t = jnp.concatenate([v, ml[:, MLSTM_W:2 * MLSTM_W], mgate, ggate], axis=1).T
    return w_r.astype(BF16), w_t.astype(BF16)


def _gate_row(vals, col0):
    flat = vals.reshape(-1).astype(F32)
    return jnp.zeros((1, GATE_W), F32).at[0, col0:col0 + flat.shape[0]].set(flat)


def _gate_col(first, second):
    flat = jnp.concatenate([first.reshape(-1), second.reshape(-1)]).astype(F32)
    return jnp.broadcast_to(flat[:, None], (flat.shape[0], CHUNK))


def kernel(x, c, ctx, c_ctx, mod_w, mod_b, norm1_g, w_in, q_norm_g, k_norm_g, mlstm_i_bias, mlstm_f_bias,
           mlstm_out_g, gdn_conv_w, gdn_a_log, gdn_dt_bias, gdn_out_g, w_out, norm2_g, router_w, w1, w3, w2):
    b, seq, d = x.shape
    ctx_len = ctx.shape[1]
    depth = mod_w.shape[0]
    tm = ctx_len
    t = ctx_len + seq
    ne = N_EXPERTS
    cap_x = CAPACITY_FACTOR * seq // ne
    cap_c = CAPACITY_FACTOR * ctx_len // ne
    rows_c = 2 * SUBLANES

    cvec = jnp.concatenate([c, c_ctx[None, :], jnp.zeros((SUBLANES - b - 1, d), F32)], axis=0)
    mod = _modulation(cvec, mod_w, mod_b)
    cos_t, sin_t = _rope_tables(ctx_len, seq)
    bd = jnp.kron(jnp.eye(LANES // HEAD_DIM, dtype=F32), jnp.full((HEAD_DIM, HEAD_DIM), 1.0 / HEAD_DIM, F32)).astype(BF16)
    xa = jnp.concatenate([ctx, x], axis=1)
    zeros8 = jnp.zeros((N_DIR, MLSTM_HEADS), F32)

    for l in range(depth):
        need_ctx = l < depth - 1
        blk0 = 0 if need_ctx else 1
        modsel = jnp.stack([jnp.broadcast_to(mod[l, b], (b, N_MOD * d)), mod[l, :b]], axis=1)[:, :, None, :]
        w_r, w_t = _reorder_w_in(w_in[l])
        qg = jnp.tile(q_norm_g[l], LANES // HEAD_DIM)[None, :]
        kg = jnp.tile(k_norm_g[l], LANES // HEAD_DIM)[None, :]
        q, k, vt, mqkv, mkt, mo, gqkv, gz, gates, gates_t = _inproj(
            xa, modsel, norm1_g[l][None, :], w_r, w_t, cos_t, sin_t, qg, kg, bd, tm)

        ao = _attention(q, k, vt, ctx_len, tm, blk0)

        brow = _gate_row(mlstm_i_bias[l], GC_MI) + _gate_row(mlstm_f_bias[l], GC_MF)
        bcol = _gate_col(mlstm_i_bias[l], mlstm_f_bias[l])
        mhf, mhr = _mlstm_scan(mqkv, mkt, gates, gates_t, brow, bcol, ctx_len)

        conv12 = gdn_conv_w[l].reshape(CONV_K, 3 * GDN_HEADS, HEAD_DIM).transpose(1, 0, 2)
        gp = _gdn_prep(gqkv, conv12, ctx_len)
        neg_a = -jnp.exp(gdn_a_log[l].astype(F32))
        ghf, ghr = _gdn_scan(gp, gates, gates_t, _gate_row(neg_a, GC_GA), _gate_row(gdn_dt_bias[l], GC_GA),
                             _gate_col(neg_a, zeros8), _gate_col(gdn_dt_bias[l], zeros8), ctx_len)

        w_hm = w_out[l].astype(BF16)
        x1, hp, aff = _outproj(xa, modsel, ao, mhf, mhr, mo, mlstm_out_g[l], ghf, ghr, gz, gdn_out_g[l][None, :],
                               w_hm, norm2_g[l][None, :], router_w[l].T, tm, blk0)

        idx_x, val_x = _route(aff[:, :, ctx_len:].reshape(b, ne, seq // LANES, LANES), cap_x)
        idx_xf = idx_x.reshape(-1)
        xgs = [_gather(idx_xf, hp, cap_x, t, ctx_len)]
        vals = [val_x.transpose(1, 0, 2, 3).reshape(ne, b * cap_x, 1)]
        if need_ctx:
            aff_c = jnp.pad(aff[:, :, :ctx_len], ((0, 0), (0, 0), (0, rows_c * LANES - ctx_len)), constant_values=-1.0)
            idx_c, val_c = _route(aff_c.reshape(b, ne, rows_c, LANES), cap_c)
            idx_cf = idx_c.reshape(-1)
            xgs.append(_gather(idx_cf, hp, cap_c, ctx_len, 0))
            vals.append(val_c.transpose(1, 0, 2, 3).reshape(ne, b * cap_c, 1))
        ys = _ffn(xgs, vals, w1, w3, w2, l, min(FFN_TF, w1.shape[3]))
        moe_x = _combine(idx_xf, ys[0], b, seq, cap_x)
        moe_c = _combine(idx_cf, ys[1], b, ctx_len, cap_c) if need_ctx else moe_x
        xa = _residual(x1, modsel, moe_x, moe_c, tm, blk0)
    return xa
```

```python
import functools
import math

import jax
import jax.numpy as jnp
from jax import lax
from jax.experimental import pallas as pl
from jax.experimental.pallas import tpu as pltpu

F32 = jnp.float32
BF16 = jnp.bfloat16
I32 = jnp.int32
U32 = jnp.uint32
HIGHEST = lax.Precision.HIGHEST

HEAD_DIM = 64
ATTN_HEADS = 8
ATTN_KV_HEADS = 2
ATTN_REP = ATTN_HEADS // ATTN_KV_HEADS
MLSTM_HEADS = 4
GDN_HEADS = 4
N_DIR = 2
CHUNK = 64
CONV_K = 5
GRID_W = 64
ROPE_THETA = 10000.0
N_EXPERTS = 16
CAPACITY_FACTOR = 2
N_MOD = 6
EPS = 1e-6
ATTN_W = ATTN_HEADS * HEAD_DIM
KV_W = ATTN_KV_HEADS * HEAD_DIM
MLSTM_W = MLSTM_HEADS * HEAD_DIM
GDN_W = GDN_HEADS * HEAD_DIM
LANES = 128
SUBLANES = 8
GATE_W = LANES
GATE_ROWS = 32
GC_MI, GC_MF, GC_GA, GC_GB = 0, 8, 16, 24
VMEM_LIMIT = 56 * 1024 * 1024
ATTN_COLS = 512
ATTN_VPAD = 16
LOG2E = 1.4426950408889634
FFN_TF = 512
CONV_ROWS = 256
PAD_ROWS = 8
COMBINE_GROUP = 8


def _cparams(sem):
    return pltpu.CompilerParams(dimension_semantics=sem, vmem_limit_bytes=VMEM_LIMIT)


def _sigmoid(x):
    return 1.0 / (1.0 + jnp.exp(-x))


def _silu(x):
    return x * _sigmoid(x)


def _log_sigmoid(x):
    return jnp.minimum(x, 0.0) - jnp.log1p(jnp.exp(-jnp.abs(x)))


def _softplus(x):
    return jnp.maximum(x, 0.0) + jnp.log1p(jnp.exp(-jnp.abs(x)))


def _dot(a, b, precision=None):
    return jnp.dot(a, b, preferred_element_type=F32, precision=precision)


def _dot_nt(a, b, precision=None):
    return lax.dot_general(a, b, (((1,), (1,)), ((), ())), preferred_element_type=F32, precision=precision)


def _split_bf16(a):
    hi = a.astype(BF16)
    lo = (a - hi.astype(F32)).astype(BF16)
    return hi, lo


def _split_bf16_3(a):
    hi = a.astype(BF16)
    r = a - hi.astype(F32)
    mid = r.astype(BF16)
    return hi, mid, (r - mid.astype(F32)).astype(BF16)


def _dot3(a, b):
    ah, al = _split_bf16(a)
    bh, bl = _split_bf16(b)
    return _dot(ah, bh) + (_dot(ah, bl) + _dot(al, bh))


def _iota(shape, dim):
    return lax.broadcasted_iota(I32, shape, dim)


def _eye_rows(rows, cols, first):
    return (_iota((rows, cols), 0) + first == _iota((rows, cols), 1)).astype(F32)


def _mod_kernel(c_ref, w_ref, b_ref, o_ref):
    s = _silu(c_ref[...])
    o_ref[0] = _dot(s, w_ref[0], precision=HIGHEST) + b_ref[0]


def _modulation(cvec, mod_w, mod_b):
    depth, d, n = mod_w.shape
    rows = cvec.shape[0]
    tn = d
    return pl.pallas_call(
        _mod_kernel,
        grid=(depth, n // tn),
        in_specs=[
            pl.BlockSpec((rows, d), lambda l, j: (0, 0)),
            pl.BlockSpec((1, d, tn), lambda l, j: (l, 0, j)),
            pl.BlockSpec((1, 1, tn), lambda l, j: (l, 0, j)),
        ],
        out_specs=pl.BlockSpec((1, rows, tn), lambda l, j: (l, 0, j)),
        out_shape=jax.ShapeDtypeStruct((depth, rows, n), F32),
        compiler_params=_cparams(("parallel", "parallel")),
        name="modulation",
    )(cvec, mod_w, mod_b.reshape(depth, 1, n))


def _inproj_kernel(x_ref, mod_ref, g_ref, w_ref, wt_ref, cos_ref, sin_ref, qg_ref, kg_ref, bd_ref,
                   q_ref, k_ref, vt_ref, ml_ref, mkt_ref, mo_ref, gd_ref, gz_ref, gate_ref, gatet_ref, *, d_model):
    x = x_ref[0]
    tm = x.shape[0]
    sh = mod_ref[:, 0:d_model]
    sc = mod_ref[:, d_model:2 * d_model]
    xn = x * lax.rsqrt(jnp.mean(x * x, axis=-1, keepdims=True) + EPS) * g_ref[...]
    h = (xn * (1.0 + sc) + sh).astype(BF16)
    p = _dot(h, w_ref[...])
    pt = _dot_nt(wt_ref[...], h)
    for g in range(ATTN_KV_HEADS):
        vt_ref[0, g, 0:HEAD_DIM, :] = pt[g * HEAD_DIM:(g + 1) * HEAD_DIM].astype(BF16)
        vt_ref[0, g, HEAD_DIM:, :] = (_iota((ATTN_VPAD, tm), 0) == 0).astype(BF16)
    for c in range(tm // CHUNK):
        cs = slice(c * CHUNK, (c + 1) * CHUNK)
        for j in range(MLSTM_HEADS):
            mkt_ref[0, j, c] = pt[KV_W + j * HEAD_DIM:KV_W + (j + 1) * HEAD_DIM, cs] * HEAD_DIM ** -0.5
        gatet_ref[0, c] = pt[KV_W + MLSTM_W:KV_W + MLSTM_W + GATE_ROWS, cs]

    cos = cos_ref[...]
    sin = sin_ref[...]
    first_half = (_iota(cos.shape, 1) % 32) < 16

    def norm_rope(xs, g, scale):
        ms = sum(_dot(piece, bd_ref[...]) for piece in _split_bf16(xs * xs))
        xn_ = xs * lax.rsqrt(ms + EPS) * g
        sw = jnp.where(first_half, pltpu.roll(xn_, LANES - 16, 1), pltpu.roll(xn_, 16, 1))
        return (xn_ * cos + sw * sin) * scale

    for j in range(ATTN_W // LANES):
        qs = norm_rope(p[:, j * LANES:(j + 1) * LANES], qg_ref[...], LOG2E * HEAD_DIM ** -0.5).astype(BF16)
        q_ref[0, 2 * j] = qs[:, 0:HEAD_DIM]
        q_ref[0, 2 * j + 1] = qs[:, HEAD_DIM:LANES]
    ks = norm_rope(p[:, ATTN_W:ATTN_W + KV_W], kg_ref[...], 1.0).astype(BF16)
    k_ref[0, 0] = ks[:, 0:HEAD_DIM]
    k_ref[0, 1] = ks[:, HEAD_DIM:LANES]

    off = ATTN_W + KV_W

    def head(j):
        return p[:, off + j * HEAD_DIM: off + (j + 1) * HEAD_DIM]

    for j in range(12):
        ml_ref[0, j] = head(j) * HEAD_DIM ** -0.5 if 4 <= j < 8 else head(j)
    for j in range(4):
        mo_ref[0, j] = head(12 + j)
    off += 4 * MLSTM_W
    for j in range(12):
        gd_ref[0, j] = head(j)
    for j in range(4):
        gz_ref[0, j] = head(12 + j)
    off += 4 * GDN_W
    gate_ref[0] = p[:, off:off + GATE_W]


def _inproj(xa, modsel, g1, w_r, w_t, cos_t, sin_t, qg, kg, bd, tm):
    b, t, d = xa.shape
    nb = t // tm
    nc = tm // CHUNK
    kern = functools.partial(_inproj_kernel, d_model=d)
    hm_shape = lambda nh, dt: jax.ShapeDtypeStruct((b, nh, t, HEAD_DIM), dt)
    out_shapes = (hm_shape(ATTN_HEADS, BF16), hm_shape(ATTN_KV_HEADS, BF16),
                  jax.ShapeDtypeStruct((b, ATTN_KV_HEADS, HEAD_DIM + ATTN_VPAD, t), BF16),
                  hm_shape(12, F32), jax.ShapeDtypeStruct((b, MLSTM_HEADS, t // CHUNK, HEAD_DIM, CHUNK), F32),
                  hm_shape(4, F32), hm_shape(12, F32), hm_shape(4, F32),
                  jax.ShapeDtypeStruct((b, t, GATE_W), F32),
                  jax.ShapeDtypeStruct((b, t // CHUNK, GATE_ROWS, CHUNK), F32))
    hm = lambda nh: pl.BlockSpec((1, nh, tm, HEAD_DIM), lambda bi, i: (bi, 0, i, 0))
    const = lambda a: pl.BlockSpec(a.shape, lambda bi, i: (0,) * a.ndim)
    return pl.pallas_call(
        kern,
        grid=(b, nb),
        in_specs=[
            pl.BlockSpec((1, tm, d), lambda bi, i: (bi, i, 0)),
            pl.BlockSpec((None, None, 1, N_MOD * d), lambda bi, i: (bi, jnp.minimum(i, 1), 0, 0)),
            const(g1), const(w_r), const(w_t),
            pl.BlockSpec((tm, LANES), lambda bi, i: (i, 0)),
            pl.BlockSpec((tm, LANES), lambda bi, i: (i, 0)),
            const(qg), const(kg), const(bd),
        ],
        out_specs=(hm(ATTN_HEADS), hm(ATTN_KV_HEADS),
                   pl.BlockSpec((1, ATTN_KV_HEADS, HEAD_DIM + ATTN_VPAD, tm), lambda bi, i: (bi, 0, 0, i)),
                   hm(12), pl.BlockSpec((1, MLSTM_HEADS, nc, HEAD_DIM, CHUNK), lambda bi, i: (bi, 0, i, 0, 0)),
                   hm(4), hm(12), hm(4),
                   pl.BlockSpec((1, tm, GATE_W), lambda bi, i: (bi, i, 0)),
                   pl.BlockSpec((1, nc, GATE_ROWS, CHUNK), lambda bi, i: (bi, i, 0, 0))),
        out_shape=out_shapes,
        compiler_params=_cparams(("parallel", "parallel")),
        name="inproj",
    )(xa, modsel, g1, w_r, w_t, cos_t, sin_t, qg, kg, bd)


def _attn_kernel(q_ref, k_ref, vt_ref, o_ref, sa_sc, sb_sc, xa_sc, xb_sc, m_sc, acc_sc, *, tq, tk, n_pairs, blk0):
    i = pl.program_id(2) + blk0
    q = q_ref[0].reshape(ATTN_REP * tq, HEAD_DIM)
    cols = [slice(c * ATTN_COLS, (c + 1) * ATTN_COLS) for c in range(ATTN_REP * tq // ATTN_COLS)]

    m_sc[...] = jnp.full(m_sc.shape, -jnp.inf, F32)
    acc_sc[...] = jnp.zeros(acc_sc.shape, F32)

    def score(s_ref, x_ref, tile):
        start = pl.multiple_of(tile * tk, tk)
        k = k_ref[0, 0, pl.ds(start, tk), :]
        for cs in cols:
            s = _dot_nt(k, q[cs])
            s_ref[:, cs] = s
            x_ref[:, cs] = jnp.max(s, axis=0, keepdims=True)

    def consume(s_ref, x_ref, tile, half):
        start = pl.multiple_of(tile * tk, tk)
        vt = vt_ref[0, 0, :, pl.ds(start, tk)]
        for cs in cols:
            mo = m_sc[half, :, cs]
            mn = jnp.maximum(mo, x_ref[:, cs])
            p = jnp.exp2(s_ref[:, cs] - mn)
            acc_sc[half, :, cs] = jnp.exp2(mo - mn) * acc_sc[half, :, cs] + _dot(vt, p.astype(BF16))
            m_sc[half, :, cs] = mn

    score(sa_sc, xa_sc, 0)

    def body(j, carry):
        score(sb_sc, xb_sc, 2 * j + 1)
        consume(sa_sc, xa_sc, 2 * j, 0)
        score(sa_sc, xa_sc, 2 * j + 2)
        consume(sb_sc, xb_sc, 2 * j + 1, 1)
        return carry

    n = jnp.where(i == 0, 0, n_pairs)
    lax.fori_loop(0, n, body, 0)
    consume(sa_sc, xa_sc, 2 * n, 0)
    m = jnp.maximum(m_sc[0], m_sc[1])
    acc = jnp.exp2(m_sc[0] - m) * acc_sc[0] + jnp.exp2(m_sc[1] - m) * acc_sc[1]
    o_t = (acc[0:HEAD_DIM, :] / acc[HEAD_DIM:HEAD_DIM + 1, :]).astype(BF16)
    eye = (_iota((tq, tq), 0) == _iota((tq, tq), 1)).astype(BF16)
    for r in range(ATTN_REP):
        o_ref[0, r] = _dot_nt(eye, o_t[:, r * tq:(r + 1) * tq]).astype(o_ref.dtype)


def _attention(q, k, vt, ctx_len, tq, blk0):
    b, _, t, _ = q.shape
    tk = ctx_len
    assert (t // tk) % 2 == 1
    nq = t // tq - blk0
    kern = functools.partial(_attn_kernel, tq=tq, tk=tk, n_pairs=(t // tk - 1) // 2, blk0=blk0)
    rows = ATTN_REP * tq
    return pl.pallas_call(
        kern,
        grid=(b, ATTN_KV_HEADS, nq),
        in_specs=[
            pl.BlockSpec((1, ATTN_REP, tq, HEAD_DIM), lambda bi, g, i: (bi, g, i + blk0, 0)),
            pl.BlockSpec((1, 1, t, HEAD_DIM), lambda bi, g, i: (bi, g, 0, 0)),
            pl.BlockSpec((1, 1, HEAD_DIM + ATTN_VPAD, t), lambda bi, g, i: (bi, g, 0, 0)),
        ],
        out_specs=pl.BlockSpec((1, ATTN_REP, tq, HEAD_DIM), lambda bi, g, i: (bi, g, i, 0)),
        out_shape=jax.ShapeDtypeStruct((b, ATTN_HEADS, nq * tq, HEAD_DIM), BF16),
        scratch_shapes=[pltpu.VMEM((tk, rows), F32), pltpu.VMEM((tk, rows), F32),
                        pltpu.VMEM((1, rows), F32), pltpu.VMEM((1, rows), F32),
                        pltpu.VMEM((2, 1, rows), F32), pltpu.VMEM((2, HEAD_DIM + ATTN_VPAD, rows), F32)],
        compiler_params=_cparams(("parallel", "parallel", "arbitrary")),
        name="attention",
    )(q, k, vt)


def _chunk_maps(nc_c, nc_x):
    fwd = lambda j: j
    rev = lambda j: jnp.where(j < nc_c, nc_c - 1 - j, 2 * nc_c + nc_x - 1 - j)
    return fwd, rev


def _dir_masks(d):
    r = _iota((CHUNK, CHUNK), 0)
    c = _iota((CHUNK, CHUNK), 1)
    incl = (r >= c) if d == 0 else (r <= c)
    strict = (r > c) if d == 0 else (r < c)
    incl_t = (r <= c) if d == 0 else (r >= c)
    return incl, strict, incl.astype(F32), incl_t.astype(F32)


def _mlstm_kernel(xf_ref, xr_ref, ktf_ref, ktr_ref, gf_ref, gr_ref, gtf_ref, gtr_ref, brow_ref, bcol_ref,
                  hf_ref, hr_ref, c_sc, n_sc, m_sc, *, nb):
    @pl.when(pl.program_id(0) == 0)
    def _():
        c_sc[...] = jnp.zeros(c_sc.shape, F32)
        n_sc[...] = jnp.zeros(n_sc.shape, F32)
        m_sc[...] = jnp.zeros(m_sc.shape, F32)

    refs = ((xf_ref, ktf_ref, gf_ref, gtf_ref, hf_ref), (xr_ref, ktr_ref, gr_ref, gtr_ref, hr_ref))
    masks = [_dir_masks(d) for d in range(N_DIR)]
    gate = {}
    for d in range(N_DIR):
        _, _, g_ref, gt_ref, _ = refs[d]
        _, _, tri, tri_t = masks[d]
        sel = (_iota((GATE_W, MLSTM_HEADS * LANES), 0)
               == GC_MF + d * MLSTM_HEADS + _iota((GATE_W, MLSTM_HEADS * LANES), 1) // LANES).astype(BF16)
        for b in range(nb):
            cum = _dot(tri, _log_sigmoid(g_ref[b] + brow_ref[...]), precision=HIGHEST)
            g_t = gt_ref[b, 0, GC_MI:GC_MI + 16, :] + bcol_ref[...]
            lf_t = _log_sigmoid(g_t)
            cum_rep = sum(_dot(piece, sel) for piece in _split_bf16_3(cum))
            gate[d, b] = (cum_rep, g_t, lf_t, _dot(lf_t, tri_t, precision=HIGHEST))

    probs = [(d, b, h) for d in range(N_DIR) for b in range(nb) for h in range(MLSTM_HEADS)]
    st = []
    for d, b, h in probs:
        p = (d * nb + b) * MLSTM_HEADS + h
        x_ref, kt_ref = refs[d][0], refs[d][1]
        cum_rep, g_t, lf_t, cum_t = gate[d, b]
        ri = d * MLSTM_HEADS + h
        rf = GC_MF + ri
        bcum_col = cum_rep[:, h * LANES:h * LANES + CHUNK]
        bcum_row = cum_t[rf - GC_MI:rf - GC_MI + 1, :]
        i_row = g_t[ri:ri + 1, :]
        b_last = jnp.sum(lf_t[rf - GC_MI:rf - GC_MI + 1, :], axis=-1, keepdims=True)
        m_old = m_sc[p, 0:1, 0:1]
        w_end = b_last - bcum_row + i_row
        m_new = jnp.maximum(b_last + m_old, jnp.max(w_end, axis=-1, keepdims=True))
        a_row = jnp.exp(w_end - m_new)
        dec = jnp.exp(b_last + m_old - m_new)
        dmat = jnp.where(masks[d][0], bcum_col - bcum_row + i_row, -jnp.inf)
        inter = bcum_col + m_sc[p, 0:1, 0:CHUNK]
        m_t = jnp.maximum(inter, jnp.max(dmat, axis=-1, keepdims=True))
        st.append(dict(p=p, q=x_ref[b, h], k=x_ref[b, MLSTM_HEADS + h], v=x_ref[b, 2 * MLSTM_HEADS + h],
                       k_t=kt_ref[b, h, 0], c_old=c_sc[p], n_old=n_sc[p], a_row=a_row, dec=dec, m_new=m_new,
                       m_t=m_t, w_in=jnp.exp(inter - m_t), dexp=jnp.exp(dmat - m_t), out=refs[d][4], b=b, h=h))
    for e in st:
        e["s"] = _dot(e["q"], e["k_t"]) * e["dexp"]
    for e in st:
        e["qc"] = _dot(e["q"], e["c_old"])
    for e in st:
        e["sv"] = _dot(e["s"], e["v"])
    for e in st:
        e["kv"] = _dot(e["k_t"] * e["a_row"], e["v"])
    for e in st:
        e["ak"] = _dot(jnp.broadcast_to(e["a_row"], (SUBLANES, CHUNK)), e["k"])
    for e in st:
        num = e["w_in"] * e["qc"] + e["sv"]
        den = (e["w_in"] * jnp.sum(e["q"] * e["n_old"][0:1, :], axis=-1, keepdims=True)
               + jnp.sum(e["s"], axis=-1, keepdims=True))
        e["out"][e["b"], e["h"]] = num / jnp.maximum(jnp.abs(den), jnp.exp(-e["m_t"]))
    for e in st:
        p = e["p"]
        c_sc[p] = e["dec"] * e["c_old"] + e["kv"]
        n_sc[p] = e["dec"] * e["n_old"] + e["ak"]
        m_sc[p] = jnp.broadcast_to(e["m_new"], (SUBLANES, LANES))


def _mlstm_scan(mqkv, k_t, gates, gates_t, brow, bcol, ctx_len):
    b, _, t, _ = mqkv.shape
    nc_c, nc_x = ctx_len // CHUNK, (t - ctx_len) // CHUNK
    fwd, rev = _chunk_maps(nc_c, nc_x)
    nprob = N_DIR * b * MLSTM_HEADS
    xs = lambda f: pl.BlockSpec((b, 12, CHUNK, HEAD_DIM), lambda j: (0, 0, f(j), 0))
    ks = lambda f: pl.BlockSpec((b, MLSTM_HEADS, 1, HEAD_DIM, CHUNK), lambda j: (0, 0, f(j), 0, 0))
    gs = lambda f: pl.BlockSpec((b, CHUNK, GATE_W), lambda j: (0, f(j), 0))
    gts = lambda f: pl.BlockSpec((b, 1, GATE_ROWS, CHUNK), lambda j: (0, f(j), 0, 0))
    hs = lambda f: pl.BlockSpec((b, MLSTM_HEADS, CHUNK, HEAD_DIM), lambda j: (0, 0, f(j), 0))
    out = jax.ShapeDtypeStruct((b, MLSTM_HEADS, t, HEAD_DIM), F32)
    return pl.pallas_call(
        functools.partial(_mlstm_kernel, nb=b),
        grid=(nc_c + nc_x,),
        in_specs=[xs(fwd), xs(rev), ks(fwd), ks(rev), gs(fwd), gs(rev), gts(fwd), gts(rev),
                  pl.BlockSpec((1, GATE_W), lambda j: (0, 0)),
                  pl.BlockSpec((16, CHUNK), lambda j: (0, 0))],
        out_specs=(hs(fwd), hs(rev)),
        out_shape=(out, out),
        scratch_shapes=[pltpu.VMEM((nprob, HEAD_DIM, HEAD_DIM), F32),
                        pltpu.VMEM((nprob, SUBLANES, HEAD_DIM), F32),
                        pltpu.VMEM((nprob, SUBLANES, LANES), F32)],
        compiler_params=_cparams(("arbitrary",)),
        name="mlstm_scan",
    )(mqkv, mqkv, k_t, k_t, gates, gates, gates_t, gates_t, brow, bcol)


def _gdn_prep_kernel(x_ref, w_ref, o_ref, pad_sc, *, ctx_len, t):
    part = pl.program_id(1) // GDN_HEADS
    zeros = jnp.zeros((PAD_ROWS, HEAD_DIM), F32)
    pad_sc[0:PAD_ROWS] = zeros
    pad_sc[PAD_ROWS:PAD_ROWS + ctx_len] = x_ref[0, 0, 0:ctx_len]
    pad_sc[PAD_ROWS + ctx_len:2 * PAD_ROWS + ctx_len] = zeros
    pad_sc[2 * PAD_ROWS + ctx_len:2 * PAD_ROWS + t] = x_ref[0, 0, ctx_len:t]
    pad_sc[2 * PAD_ROWS + t:3 * PAD_ROWS + t] = zeros
    w = w_ref[0]
    is_qk = part < 2
    scale = jnp.where(part == 0, HEAD_DIM ** -0.5, 1.0)
    for c in range(t // CONV_ROWS):
        r0 = c * CONV_ROWS
        base = r0 + (PAD_ROWS if r0 < ctx_len else 2 * PAD_ROWS) - CONV_K // 2
        y = w[0:1, :] * pad_sc[base:base + CONV_ROWS]
        for j in range(1, CONV_K):
            y = y + w[j:j + 1, :] * pad_sc[base + j:base + j + CONV_ROWS]
        y = _silu(y)
        yn = y * lax.rsqrt(jnp.sum(y * y, axis=-1, keepdims=True) + EPS) * scale
        o_ref[0, 0, r0:r0 + CONV_ROWS] = jnp.where(is_qk, yn, y)


def _gdn_prep(gqkv, conv_w12, ctx_len):
    b, np_, t, _ = gqkv.shape
    return pl.pallas_call(
        functools.partial(_gdn_prep_kernel, ctx_len=ctx_len, t=t),
        grid=(b, np_),
        in_specs=[pl.BlockSpec((1, 1, t, HEAD_DIM), lambda bi, p: (bi, p, 0, 0)),
                  pl.BlockSpec((1, CONV_K, HEAD_DIM), lambda bi, p: (p, 0, 0))],
        out_specs=pl.BlockSpec((1, 1, t, HEAD_DIM), lambda bi, p: (bi, p, 0, 0)),
        out_shape=jax.ShapeDtypeStruct(gqkv.shape, F32),
        scratch_shapes=[pltpu.VMEM((t + 3 * PAD_ROWS, HEAD_DIM), F32)],
        compiler_params=_cparams(("parallel", "parallel")),
        name="gdn_prep",
    )(gqkv, conv_w12)


def _gdn_kernel(xf_ref, xr_ref, gf_ref, gr_ref, gtf_ref, gtr_ref, arow_ref, drow_ref, acol_ref, dcol_ref,
                of_ref, or_ref, s_sc, rhs_sc, *, nb):
    @pl.when(pl.program_id(0) == 0)
    def _():
        s_sc[...] = jnp.zeros(s_sc.shape, F32)

    refs = ((xf_ref, gf_ref, gtf_ref, of_ref), (xr_ref, gr_ref, gtr_ref, or_ref))
    masks = [_dir_masks(d) for d in range(N_DIR)]
    gate = {}
    for d in range(N_DIR):
        g_ref, gt_ref = refs[d][1], refs[d][2]
        _, _, tri, tri_t = masks[d]
        for b in range(nb):
            g = g_ref[b]
            gval = arow_ref[...] * _softplus(g + drow_ref[...])
            g_t = gt_ref[b, 0, GC_GA:GC_GA + 16, :]
            gval_t = acol_ref[...] * _softplus(g_t + dcol_ref[...])
            gate[d, b] = (_dot(tri, gval, precision=HIGHEST), _sigmoid(g), gval_t,
                          _dot(gval_t, tri_t, precision=HIGHEST))

    probs = [(d, b, h) for d in range(N_DIR) for b in range(nb) for h in range(GDN_HEADS)]
    eye = _eye_rows(HEAD_DIM, HEAD_DIM, 0).astype(BF16)
    st = []
    for d, b, h in probs:
        x_ref = refs[d][0]
        st.append(dict(p=(d * nb + b) * GDN_HEADS + h, d=d, b=b, h=h, q=x_ref[b, h], k=x_ref[b, GDN_HEADS + h],
                       v=x_ref[b, 2 * GDN_HEADS + h]))
    for e in st:
        e["k_t"] = sum(_dot_nt(eye, piece) for piece in _split_bf16(e["k"]))
    for e in st:
        d, b, h, p = e["d"], e["b"], e["h"], e["p"]
        incl, strict, _, _ = masks[d]
        gcum, beta_all, gval_t, gcum_t = gate[d, b]
        ra = d * GDN_HEADS + h
        g_col = gcum[:, GC_GA + ra:GC_GA + ra + 1]
        g_row = gcum_t[ra:ra + 1, :]
        beta = beta_all[:, GC_GB + ra:GC_GB + ra + 1]
        g_last = jnp.sum(gval_t[ra:ra + 1, :], axis=-1, keepdims=True)
        eg = jnp.exp(g_col)
        kb = e["k"] * beta
        rhs_sc[p, :, 0:HEAD_DIM] = e["v"] * beta
        rhs_sc[p, :, HEAD_DIM:] = kb * eg
        e.update(out=refs[d][3], strict=strict, eg_last=jnp.exp(g_last), kb=kb, qg=e["q"] * eg,
                 decay=jnp.where(incl, jnp.exp(jnp.where(incl, g_col - g_row, 0.0)), 0.0),
                 ktg=e["k_t"] * jnp.exp(g_last - g_row), s_old=s_sc[p])
    for e in st:
        e["sol"] = rhs_sc[e["p"]]
    for e in st:
        e["pw"] = -jnp.where(e["strict"], _dot3(e["kb"], e["k_t"]) * e["decay"], 0.0)
    n_fac = CHUNK.bit_length() - 1
    for it in range(n_fac):
        for e in st:
            e["sol"] = e["sol"] + _dot3(e["pw"], e["sol"])
        if it < n_fac - 1:
            for e in st:
                e["pw"] = _dot3(e["pw"], e["pw"])
    for e in st:
        e["ws"] = _dot(e["sol"][:, HEAD_DIM:], e["s_old"])
    for e in st:
        e["qs"] = _dot(e["qg"], e["s_old"])
    for e in st:
        e["qk"] = _dot(e["q"], e["k_t"]) * e["decay"]
    for e in st:
        e["v_new"] = e["sol"][:, :HEAD_DIM] - e["ws"]
    for e in st:
        e["out"][e["b"], e["h"]] = e["qs"] + _dot(e["qk"], e["v_new"])
    for e in st:
        s_sc[e["p"]] = e["eg_last"] * e["s_old"] + _dot(e["ktg"], e["v_new"])


def _gdn_scan(gp, gates, gates_t, arow, drow, acol, dcol, ctx_len):
    b, _, t, _ = gp.shape
    nc_c, nc_x = ctx_len // CHUNK, (t - ctx_len) // CHUNK
    fwd, rev = _chunk_maps(nc_c, nc_x)
    nprob = N_DIR * b * GDN_HEADS
    xs = lambda f: pl.BlockSpec((b, 3 * GDN_HEADS, CHUNK, HEAD_DIM), lambda j: (0, 0, f(j), 0))
    gs = lambda f: pl.BlockSpec((b, CHUNK, GATE_W), lambda j: (0, f(j), 0))
    gts = lambda f: pl.BlockSpec((b, 1, GATE_ROWS, CHUNK), lambda j: (0, f(j), 0, 0))
    hs = lambda f: pl.BlockSpec((b, GDN_HEADS, CHUNK, HEAD_DIM), lambda j: (0, 0, f(j), 0))
    row = pl.BlockSpec((1, GATE_W), lambda j: (0, 0))
    col = pl.BlockSpec((16, CHUNK), lambda j: (0, 0))
    out = jax.ShapeDtypeStruct((b, GDN_HEADS, t, HEAD_DIM), F32)
    return pl.pallas_call(
        functools.partial(_gdn_kernel, nb=b),
        grid=(nc_c + nc_x,),
        in_specs=[xs(fwd), xs(rev), gs(fwd), gs(rev), gts(fwd), gts(rev), row, row, col, col],
        out_specs=(hs(fwd), hs(rev)),
        out_shape=(out, out),
        scratch_shapes=[pltpu.VMEM((nprob, HEAD_DIM, HEAD_DIM), F32),
                        pltpu.VMEM((nprob, CHUNK, 2 * HEAD_DIM), F32)],
        compiler_params=_cparams(("arbitrary",)),
        name="gdn_scan",
    )(gp, gp, gates, gates, gates_t, gates_t, arow, drow, acol, dcol)


def _outproj_kernel(x_ref, mod_ref, ao_ref, mhf_ref, mhr_ref, mo_ref, mg_ref, ghf_ref, ghr_ref, gz_ref, gg_ref,
                    w_ref, n2_ref, rw_ref, x1_ref, hp_ref, aff_ref, mix_sc, *, d_model):
    d = d_model

    def put(j, val):
        mix_sc[:, j * HEAD_DIM:(j + 1) * HEAD_DIM] = val.astype(BF16)

    for h in range(ATTN_HEADS):
        put(h, ao_ref[0, h])
    for h in range(MLSTM_HEADS):
        hh = mhf_ref[0, h] + mhr_ref[0, h]
        hn = hh * lax.rsqrt(jnp.mean(hh * hh, axis=-1, keepdims=True) + EPS) * mg_ref[h:h + 1, :]
        put(ATTN_HEADS + h, _sigmoid(mo_ref[0, h]) * hn)
    for h in range(GDN_HEADS):
        oo = ghf_ref[0, h] + ghr_ref[0, h]
        on = oo * lax.rsqrt(jnp.mean(oo * oo, axis=-1, keepdims=True) + EPS) * gg_ref[...]
        put(ATTN_HEADS + MLSTM_HEADS + h, on * _silu(gz_ref[0, h]))
    x1 = x_ref[0] + mod_ref[:, 2 * d:3 * d] * _dot(mix_sc[...], w_ref[...])
    x1_ref[0] = x1
    xn = x1 * lax.rsqrt(jnp.mean(x1 * x1, axis=-1, keepdims=True) + EPS) * n2_ref[...]
    h2 = xn * (1.0 + mod_ref[:, 4 * d:5 * d]) + mod_ref[:, 3 * d:4 * d]
    logits = _dot_nt(rw_ref[...], h2, precision=HIGHEST)
    e = jnp.exp(logits - jnp.max(logits, axis=0, keepdims=True))
    aff_ref[0] = e / jnp.sum(e, axis=0, keepdims=True)
    hp_ref[0] = h2


def _outproj(xa, modsel, ao, mhf, mhr, mo, mg, ghf, ghr, gz, gg, w_hm, n2, rw_t, tm, blk0):
    b, t, d = xa.shape
    nb = t // tm - blk0
    t_out = nb * tm
    hm = lambda nh: pl.BlockSpec((1, nh, tm, HEAD_DIM), lambda bi, i: (bi, 0, i + blk0, 0))
    full = lambda a: pl.BlockSpec(a.shape, lambda bi, i: (0,) * a.ndim)
    return pl.pallas_call(
        functools.partial(_outproj_kernel, d_model=d),
        grid=(b, nb),
        in_specs=[
            pl.BlockSpec((1, tm, d), lambda bi, i: (bi, i + blk0, 0)),
            pl.BlockSpec((None, None, 1, N_MOD * d), lambda bi, i: (bi, jnp.minimum(i + blk0, 1), 0, 0)),
            pl.BlockSpec((1, ATTN_HEADS, tm, HEAD_DIM), lambda bi, i: (bi, 0, i, 0)),
            hm(4), hm(4), hm(4), full(mg), hm(4), hm(4), hm(4), full(gg),
            full(w_hm), full(n2), full(rw_t),
        ],
        out_specs=(pl.BlockSpec((1, tm, d), lambda bi, i: (bi, i, 0)),
                   pl.BlockSpec((1, tm, d), lambda bi, i: (bi, i, 0)),
                   pl.BlockSpec((1, N_EXPERTS, tm), lambda bi, i: (bi, 0, i))),
        out_shape=(jax.ShapeDtypeStruct((b, t_out, d), F32),
                   jax.ShapeDtypeStruct((b, t_out, d), F32),
                   jax.ShapeDtypeStruct((b, N_EXPERTS, t_out), F32)),
        scratch_shapes=[pltpu.VMEM((tm, w_hm.shape[0]), BF16)],
        compiler_params=_cparams(("parallel", "parallel")),
        name="outproj",
    )(xa, modsel, ao, mhf, mhr, mo, mg, ghf, ghr, gz, gg, w_hm, n2, rw_t)


def _route_kernel(aff_ref, idx_ref, val_ref, *, cap, rows):
    ne = N_EXPERTS
    a = aff_ref[0]

    def count(mask):
        return jnp.sum(jnp.sum(mask.astype(I32), axis=2, keepdims=True), axis=1, keepdims=True)

    tau_bits = jnp.zeros((ne, 1, 1), I32)
    for bit in range(30, -1, -1):
        cand = tau_bits | (1 << bit)
        keep = count(a >= lax.bitcast_convert_type(cand, F32)) >= cap
        tau_bits = jnp.where(keep, cand, tau_bits)
    tau = lax.bitcast_convert_type(tau_bits, F32)
    gt = a > tau
    eq = a == tau
    need = cap - count(gt)

    triu = (_iota((LANES, LANES), 0) <= _iota((LANES, LANES), 1)).astype(BF16)
    strict_lower = (_iota((rows, rows), 0) > _iota((rows, rows), 1)).astype(BF16)
    triu_r = (_iota((rows, rows), 0) <= _iota((rows, rows), 1)).astype(BF16)
    ones_r = jnp.ones((SUBLANES, LANES), BF16)

    def prefix(mask2d):
        m = mask2d.astype(BF16)
        within = _dot(m, triu)
        tot = jnp.broadcast_to(within[:, LANES - 1:LANES], (rows, LANES)).astype(BF16)
        return within, _dot(strict_lower, tot)

    lane_r = _iota((cap, rows), 1).astype(F32)
    lane_l = _iota((cap, LANES), 1).astype(F32)
    slot = _iota((cap, 1), 0).astype(F32)
    for e in range(ne):
        eq_e = eq[e]
        w_eq, before_eq = prefix(eq_e)
        rank_eq = w_eq - eq_e.astype(F32) + before_eq
        sel = gt[e] | (eq_e & (rank_eq < need[e].astype(F32)))
        rel, _ = prefix(sel)
        sel_b = sel.astype(BF16)
        row_tot = _dot_nt(ones_r, sel_b)
        row_incl = _dot(row_tot.astype(BF16), triu_r)
        row_excl = row_incl - row_tot
        kstar = jnp.sum((row_incl[0:1, :] <= slot).astype(F32), axis=-1, keepdims=True)
        onehot = (lane_r == kstar).astype(F32)
        base = jnp.sum(onehot * row_excl[0:1, :], axis=-1, keepdims=True)
        g_rel = _dot(onehot.astype(BF16), rel.astype(BF16))
        within = jnp.sum((g_rel <= slot - base).astype(F32), axis=-1, keepdims=True)
        g_aff = _dot(onehot, a[e], precision=HIGHEST)
        val_ref[0, e] = jnp.sum(jnp.where(lane_l == within, g_aff, 0.0), axis=-1, keepdims=True)
        idx_ref[0, e] = (kstar * LANES + within).astype(I32)


def _route(aff_tiles, cap):
    b, ne, rows, _ = aff_tiles.shape
    return pl.pallas_call(
        functools.partial(_route_kernel, cap=cap, rows=rows),
        grid=(b,),
        in_specs=[pl.BlockSpec((1, ne, rows, LANES), lambda bi: (bi, 0, 0, 0))],
        out_specs=(pl.BlockSpec((1, ne, cap, 1), lambda bi: (bi, 0, 0, 0)),
                   pl.BlockSpec((1, ne, cap, 1), lambda bi: (bi, 0, 0, 0))),
        out_shape=(jax.ShapeDtypeStruct((b, ne, cap, 1), I32), jax.ShapeDtypeStruct((b, ne, cap, 1), F32)),
        compiler_params=_cparams(("parallel",)),
        name="route",
    )(aff_tiles)


def _gather_kernel(idx_ref, h_ref, o_ref, rows_sc, *, cap, row_off):
    base = (pl.program_id(0) * pl.num_programs(2) + pl.program_id(2)) * cap

    def body(s, carry):
        t = idx_ref[base + s] + row_off
        rows_sc[pl.ds(s, 1), :] = h_ref[0, pl.ds(t, 1), :]
        return carry

    lax.fori_loop(0, cap, body, 0, unroll=8)
    o_ref[0, 0] = rows_sc[...].astype(BF16)


def _gather(idx_flat, h2, cap, rows_block, row_off):
    b, _, d = h2.shape
    dh = d // 2
    return pl.pallas_call(
        functools.partial(_gather_kernel, cap=cap, row_off=row_off),
        grid_spec=pltpu.PrefetchScalarGridSpec(
            num_scalar_prefetch=1,
            grid=(b, 2, N_EXPERTS),
            in_specs=[pl.BlockSpec((1, rows_block, dh), lambda bi, c, e, idx: (bi, 0, c))],
            out_specs=pl.BlockSpec((1, 1, cap, dh), lambda bi, c, e, idx: (e, bi, 0, c)),
            scratch_shapes=[pltpu.VMEM((cap, dh), F32)],
        ),
        out_shape=jax.ShapeDtypeStruct((N_EXPERTS, b, cap, d), BF16),
        compiler_params=_cparams(("arbitrary", "arbitrary", "arbitrary")),
        name="moe_gather",
    )(idx_flat, h2)


def _ffn_kernel(*refs, n_streams):
    xg_refs = refs[0:n_streams]
    val_refs = refs[n_streams:2 * n_streams]
    w1_ref, w3_ref, w2_ref = refs[2 * n_streams:2 * n_streams + 3]
    y_refs = refs[2 * n_streams + 3:]
    f = pl.program_id(1)
    w1 = w1_ref[0].astype(BF16)
    w3 = w3_ref[0].astype(BF16)
    w2 = w2_ref[0].astype(BF16)
    for xg_ref, val_ref, y_ref in zip(xg_refs, val_refs, y_refs):
        cap = xg_ref.shape[2]
        for bi in range(xg_ref.shape[1]):
            rows = slice(bi * cap, (bi + 1) * cap)
            xg = xg_ref[0, bi]
            y = _dot((_silu(_dot(xg, w1)) * _dot(xg, w3)).astype(BF16), w2)

            @pl.when(f == 0)
            def _():
                y_ref[0, rows] = y

            @pl.when(f > 0)
            def _():
                y_ref[0, rows] = y_ref[0, rows] + y

            @pl.when(f == pl.num_programs(1) - 1)
            def _():
                y_ref[0, rows] = y_ref[0, rows] * val_ref[0, rows]


def _ffn(xgs, vals, w1, w3, w2, layer, tf):
    _, ne, d, ff = w1.shape
    n_streams = len(xgs)
    in_specs, out_specs, out_shapes = [], [], []
    for xg in xgs:
        in_specs.append(pl.BlockSpec((1,) + xg.shape[1:], lambda e, f: (e, 0, 0, 0)))
    for v in vals:
        in_specs.append(pl.BlockSpec((1,) + v.shape[1:], lambda e, f: (e, 0, 0)))
    in_specs += [pl.BlockSpec((None, 1, d, tf), lambda e, f: (layer, e, 0, f)),
                 pl.BlockSpec((None, 1, d, tf), lambda e, f: (layer, e, 0, f)),
                 pl.BlockSpec((None, 1, tf, d), lambda e, f: (layer, e, f, 0))]
    for xg in xgs:
        m = xg.shape[1] * xg.shape[2]
        out_specs.append(pl.BlockSpec((1, m, d), lambda e, f: (e, 0, 0)))
        out_shapes.append(jax.ShapeDtypeStruct((ne, m, d), F32))
    return pl.pallas_call(
        functools.partial(_ffn_kernel, n_streams=n_streams),
        grid=(ne, ff // tf),
        in_specs=in_specs,
        out_specs=tuple(out_specs),
        out_shape=tuple(out_shapes),
        compiler_params=_cparams(("parallel", "arbitrary")),
        name="moe_ffn",
    )(*xgs, *vals, w1, w3, w2)


def _combine_kernel(idx_ref, y_ref, o_ref, *, cap):
    e = pl.program_id(2)

    @pl.when(e == 0)
    def _():
        o_ref[...] = jnp.zeros(o_ref.shape, F32)

    base = (pl.program_id(0) * pl.num_programs(2) + e) * cap

    def body(g, carry):
        s0 = pl.multiple_of(g * COMBINE_GROUP, COMBINE_GROUP)
        ts = [idx_ref[base + s0 + r] for r in range(COMBINE_GROUP)]
        ys = y_ref[0, pl.ds(s0, COMBINE_GROUP), :]
        rows = [o_ref[0, pl.ds(t, 1), :] for t in ts]
        for r, t in enumerate(ts):
            o_ref[0, pl.ds(t, 1), :] = rows[r] + ys[r:r + 1, :]
        return carry

    lax.fori_loop(0, cap // COMBINE_GROUP, body, 0)


def _combine(idx_flat, y, b, n_tok, cap):
    ne, _, d = y.shape
    dh = d // 2
    return pl.pallas_call(
        functools.partial(_combine_kernel, cap=cap),
        grid_spec=pltpu.PrefetchScalarGridSpec(
            num_scalar_prefetch=1,
            grid=(b, 2, ne),
            in_specs=[pl.BlockSpec((1, cap, dh), lambda bi, c, e, idx: (e, bi, c))],
            out_specs=pl.BlockSpec((1, n_tok, dh), lambda bi, c, e, idx: (bi, 0, c)),
        ),
        out_shape=jax.ShapeDtypeStruct((b, n_tok, d), F32),
        compiler_params=_cparams(("arbitrary", "arbitrary", "arbitrary")),
        name="moe_combine",
    )(idx_flat, y)


def _residual_kernel(x1_ref, mod_ref, mx_ref, mc_ref, o_ref, *, d_model, blk0):
    i = pl.program_id(1) + blk0
    moe = jnp.where(i == 0, mc_ref[0], mx_ref[0])
    o_ref[0] = x1_ref[0] + mod_ref[:, 5 * d_model:6 * d_model] * moe


def _residual(x1, modsel, moe_x, moe_c, tm, blk0):
    b, t, d = x1.shape
    nb = t // tm
    return pl.pallas_call(
        functools.partial(_residual_kernel, d_model=d, blk0=blk0),
        grid=(b, nb),
        in_specs=[pl.BlockSpec((1, tm, d), lambda bi, i: (bi, i, 0)),
                  pl.BlockSpec((None, None, 1, N_MOD * d), lambda bi, i: (bi, jnp.minimum(i + blk0, 1), 0, 0)),
                  pl.BlockSpec((1, tm, d), lambda bi, i: (bi, jnp.maximum(i + blk0 - 1, 0), 0)),
                  pl.BlockSpec((1, tm, d), lambda bi, i: (bi, 0, 0))],
        out_specs=pl.BlockSpec((1, tm, d), lambda bi, i: (bi, i, 0)),
        out_shape=jax.ShapeDtypeStruct((b, nb * tm, d), F32),
        compiler_params=_cparams(("parallel", "parallel")),
        name="moe_residual",
    )(x1, modsel, moe_x, moe_c)


def _rope_tables(ctx_len, seq):
    n = jnp.arange(seq)
    pos = jnp.stack([n // GRID_W, n % GRID_W], axis=-1).astype(F32)
    lane = jnp.arange(LANES) % HEAD_DIM
    axis = lane // 32
    n_freq = HEAD_DIM // 4
    inv = ROPE_THETA ** (-(lane % n_freq).astype(F32) / n_freq)
    ang = pos[:, axis] * inv
    sign = jnp.where((lane % 32) < 16, -1.0, 1.0)
    cos_t = jnp.concatenate([jnp.ones((ctx_len, LANES), F32), jnp.cos(ang)], axis=0)
    sin_t = jnp.concatenate([jnp.zeros((ctx_len, LANES), F32), jnp.sin(ang) * sign], axis=0)
    return cos_t, sin_t


def _reorder_w_in(w):
    d = w.shape[0]
    qk = w[:, :ATTN_W + KV_W]
    v = w[:, ATTN_W + KV_W:ATTN_W + 2 * KV_W]
    o = ATTN_W + 2 * KV_W
    ml = w[:, o:o + 4 * MLSTM_W]
    o += 4 * MLSTM_W
    mgate = w[:, o:o + 2 * N_DIR * MLSTM_HEADS]
    o += 2 * N_DIR * MLSTM_HEADS
    gd = w[:, o:o + 4 * GDN_W]
    o += 4 * GDN_W
    ggate = w[:, o:o + 2 * N_DIR * GDN_HEADS]
    pad = jnp.zeros((d, GATE_W - mgate.shape[1] - ggate.shape[1]), w.dtype)
    w_r = jnp.concatenate([qk, ml, gd, mgate, ggate, pad], axis=1)
    w_t = jnp.concatenate([v, ml[:, MLSTM_W:2 * MLSTM_W], mgate, ggate], axis=1).T
    return w_r.astype(BF16), w_t.astype(BF16)


def _gate_row(vals, col0):
    flat = vals.reshape(-1).astype(F32)
    return jnp.zeros((1, GATE_W), F32).at[0, col0:col0 + flat.shape[0]].set(flat)


def _gate_col(first, second):
    flat = jnp.concatenate([first.reshape(-1), second.reshape(-1)]).astype(F32)
    return jnp.broadcast_to(flat[:, None], (flat.shape[0], CHUNK))


def kernel(x, c, ctx, c_ctx, mod_w, mod_b, norm1_g, w_in, q_norm_g, k_norm_g, mlstm_i_bias, mlstm_f_bias,
           mlstm_out_g, gdn_conv_w, gdn_a_log, gdn_dt_bias, gdn_out_g, w_out, norm2_g, router_w, w1, w3, w2):
    b, seq, d = x.shape
    ctx_len = ctx.shape[1]
    depth = mod_w.shape[0]
    tm = ctx_len
    t = ctx_len + seq
    ne = N_EXPERTS
    cap_x = CAPACITY_FACTOR * seq // ne
    cap_c = CAPACITY_FACTOR * ctx_len // ne
    rows_c = 2 * SUBLANES

    cvec = jnp.concatenate([c, c_ctx[None, :], jnp.zeros((SUBLANES - b - 1, d), F32)], axis=0)
    mod = _modulation(cvec, mod_w, mod_b)
    cos_t, sin_t = _rope_tables(ctx_len, seq)
    bd = jnp.kron(jnp.eye(LANES // HEAD_DIM, dtype=F32), jnp.full((HEAD_DIM, HEAD_DIM), 1.0 / HEAD_DIM, F32)).astype(BF16)
    xa = jnp.concatenate([ctx, x], axis=1)
    zeros8 = jnp.zeros((N_DIR, MLSTM_HEADS), F32)

    for l in range(depth):
        need_ctx = l < depth - 1
        blk0 = 0 if need_ctx else 1
        modsel = jnp.stack([jnp.broadcast_to(mod[l, b], (b, N_MOD * d)), mod[l, :b]], axis=1)[:, :, None, :]
        w_r, w_t = _reorder_w_in(w_in[l])
        qg = jnp.tile(q_norm_g[l], LANES // HEAD_DIM)[None, :]
        kg = jnp.tile(k_norm_g[l], LANES // HEAD_DIM)[None, :]
        q, k, vt, mqkv, mkt, mo, gqkv, gz, gates, gates_t = _inproj(
            xa, modsel, norm1_g[l][None, :], w_r, w_t, cos_t, sin_t, qg, kg, bd, tm)

        ao = _attention(q, k, vt, ctx_len, tm, blk0)

        brow = _gate_row(mlstm_i_bias[l], GC_MI) + _gate_row(mlstm_f_bias[l], GC_MF)
        bcol = _gate_col(mlstm_i_bias[l], mlstm_f_bias[l])
        mhf, mhr = _mlstm_scan(mqkv, mkt, gates, gates_t, brow, bcol, ctx_len)

        conv12 = gdn_conv_w[l].reshape(CONV_K, 3 * GDN_HEADS, HEAD_DIM).transpose(1, 0, 2)
        gp = _gdn_prep(gqkv, conv12, ctx_len)
        neg_a = -jnp.exp(gdn_a_log[l].astype(F32))
        ghf, ghr = _gdn_scan(gp, gates, gates_t, _gate_row(neg_a, GC_GA), _gate_row(gdn_dt_bias[l], GC_GA),
                             _gate_col(neg_a, zeros8), _gate_col(gdn_dt_bias[l], zeros8), ctx_len)

        w_hm = w_out[l].astype(BF16)
        x1, hp, aff = _outproj(xa, modsel, ao, mhf, mhr, mo, mlstm_out_g[l], ghf, ghr, gz, gdn_out_g[l][None, :],
                               w_hm, norm2_g[l][None, :], router_w[l].T, tm, blk0)

        x_row0 = ctx_len - blk0 * tm
        idx_x, val_x = _route(aff[:, :, x_row0:].reshape(b, ne, seq // LANES, LANES), cap_x)
        idx_xf = idx_x.reshape(-1)
        xgs = [_gather(idx_xf, hp, cap_x, hp.shape[1], x_row0)]
        vals = [val_x.transpose(1, 0, 2, 3).reshape(ne, b * cap_x, 1)]
        if need_ctx:
            aff_c = jnp.pad(aff[:, :, :ctx_len], ((0, 0), (0, 0), (0, rows_c * LANES - ctx_len)), constant_values=-1.0)
            idx_c, val_c = _route(aff_c.reshape(b, ne, rows_c, LANES), cap_c)
            idx_cf = idx_c.reshape(-1)
            xgs.append(_gather(idx_cf, hp, cap_c, ctx_len, 0))
            vals.append(val_c.transpose(1, 0, 2, 3).reshape(ne, b * cap_c, 1))
        ys = _ffn(xgs, vals, w1, w3, w2, l, min(FFN_TF, w1.shape[3]))
        moe_x = _combine(idx_xf, ys[0], b, seq, cap_x)
        moe_c = _combine(idx_cf, ys[1], b, ctx_len, cap_c) if need_ctx else moe_x
        xa = _residual(x1, modsel, moe_x, moe_c, tm, blk0)
    return xa
```

```python
import functools
import math

import jax
import jax.numpy as jnp
from jax import lax
from jax.experimental import pallas as pl
from jax.experimental.pallas import tpu as pltpu

F32 = jnp.float32
BF16 = jnp.bfloat16
I32 = jnp.int32
U32 = jnp.uint32
HIGHEST = lax.Precision.HIGHEST

HEAD_DIM = 64
ATTN_HEADS = 8
ATTN_KV_HEADS = 2
ATTN_REP = ATTN_HEADS // ATTN_KV_HEADS
MLSTM_HEADS = 4
GDN_HEADS = 4
N_DIR = 2
CHUNK = 64
CONV_K = 5
GRID_W = 64
ROPE_THETA = 10000.0
N_EXPERTS = 16
CAPACITY_FACTOR = 2
N_MOD = 6
EPS = 1e-6
ATTN_W = ATTN_HEADS * HEAD_DIM
KV_W = ATTN_KV_HEADS * HEAD_DIM
MLSTM_W = MLSTM_HEADS * HEAD_DIM
GDN_W = GDN_HEADS * HEAD_DIM
LANES = 128
SUBLANES = 8
GATE_W = LANES
GATE_ROWS = 32
GC_MI, GC_MF, GC_GA, GC_GB = 0, 8, 16, 24
VMEM_LIMIT = 56 * 1024 * 1024
ATTN_COLS = 512
ATTN_VPAD = 16
LOG2E = 1.4426950408889634
FFN_TF = 512
CONV_ROWS = 256
PAD_ROWS = 8
COMBINE_GROUP = 8


def _cparams(sem):
    return pltpu.CompilerParams(dimension_semantics=sem, vmem_limit_bytes=VMEM_LIMIT)


def _sigmoid(x):
    return 1.0 / (1.0 + jnp.exp(-x))


def _silu(x):
    return x * _sigmoid(x)


def _log_sigmoid(x):
    return jnp.minimum(x, 0.0) - jnp.log1p(jnp.exp(-jnp.abs(x)))


def _softplus(x):
    return jnp.maximum(x, 0.0) + jnp.log1p(jnp.exp(-jnp.abs(x)))


def _dot(a, b, precision=None):
    return jnp.dot(a, b, preferred_element_type=F32, precision=precision)


def _dot_nt(a, b, precision=None):
    return lax.dot_general(a, b, (((1,), (1,)), ((), ())), preferred_element_type=F32, precision=precision)


def _split_bf16(a):
    hi = a.astype(BF16)
    lo = (a - hi.astype(F32)).astype(BF16)
    return hi, lo


def _split_bf16_3(a):
    hi = a.astype(BF16)
    r = a - hi.astype(F32)
    mid = r.astype(BF16)
    return hi, mid, (r - mid.astype(F32)).astype(BF16)


def _dot3(a, b):
    ah, al = _split_bf16(a)
    bh, bl = _split_bf16(b)
    return _dot(ah, bh) + (_dot(ah, bl) + _dot(al, bh))


def _iota(shape, dim):
    return lax.broadcasted_iota(I32, shape, dim)


def _eye_rows(rows, cols, first):
    return (_iota((rows, cols), 0) + first == _iota((rows, cols), 1)).astype(F32)


def _mod_kernel(c_ref, w_ref, b_ref, o_ref):
    s = _silu(c_ref[...])
    o_ref[0] = _dot(s, w_ref[0], precision=HIGHEST) + b_ref[0]


def _modulation(cvec, mod_w, mod_b):
    depth, d, n = mod_w.shape
    rows = cvec.shape[0]
    tn = d
    return pl.pallas_call(
        _mod_kernel,
        grid=(depth, n // tn),
        in_specs=[
            pl.BlockSpec((rows, d), lambda l, j: (0, 0)),
            pl.BlockSpec((1, d, tn), lambda l, j: (l, 0, j)),
            pl.BlockSpec((1, 1, tn), lambda l, j: (l, 0, j)),
        ],
        out_specs=pl.BlockSpec((1, rows, tn), lambda l, j: (l, 0, j)),
        out_shape=jax.ShapeDtypeStruct((depth, rows, n), F32),
        compiler_params=_cparams(("parallel", "parallel")),
        name="modulation",
    )(cvec, mod_w, mod_b.reshape(depth, 1, n))


def _inproj_kernel(x_ref, mod_ref, g_ref, w_ref, wt_ref, cos_ref, sin_ref, qg_ref, kg_ref, bd_ref,
                   q_ref, k_ref, vt_ref, ml_ref, mkt_ref, mo_ref, gd_ref, gz_ref, gate_ref, gatet_ref, *, d_model):
    x = x_ref[0]
    tm = x.shape[0]
    sh = mod_ref[:, 0:d_model]
    sc = mod_ref[:, d_model:2 * d_model]
    xn = x * lax.rsqrt(jnp.mean(x * x, axis=-1, keepdims=True) + EPS) * g_ref[...]
    h = (xn * (1.0 + sc) + sh).astype(BF16)
    p = _dot(h, w_ref[...])
    pt = _dot_nt(wt_ref[...], h)
    for g in range(ATTN_KV_HEADS):
        vt_ref[0, g, 0:HEAD_DIM, :] = pt[g * HEAD_DIM:(g + 1) * HEAD_DIM].astype(BF16)
        vt_ref[0, g, HEAD_DIM:, :] = (_iota((ATTN_VPAD, tm), 0) == 0).astype(BF16)
    for c in range(tm // CHUNK):
        cs = slice(c * CHUNK, (c + 1) * CHUNK)
        for j in range(MLSTM_HEADS):
            mkt_ref[0, j, c] = pt[KV_W + j * HEAD_DIM:KV_W + (j + 1) * HEAD_DIM, cs] * HEAD_DIM ** -0.5
        gatet_ref[0, c] = pt[KV_W + MLSTM_W:KV_W + MLSTM_W + GATE_ROWS, cs]

    cos = cos_ref[...]
    sin = sin_ref[...]
    first_half = (_iota(cos.shape, 1) % 32) < 16

    def norm_rope(xs, g, scale):
        ms = sum(_dot(piece, bd_ref[...]) for piece in _split_bf16(xs * xs))
        xn_ = xs * lax.rsqrt(ms + EPS) * g
        sw = jnp.where(first_half, pltpu.roll(xn_, LANES - 16, 1), pltpu.roll(xn_, 16, 1))
        return (xn_ * cos + sw * sin) * scale

    for j in range(ATTN_W // LANES):
        qs = norm_rope(p[:, j * LANES:(j + 1) * LANES], qg_ref[...], LOG2E * HEAD_DIM ** -0.5).astype(BF16)
        q_ref[0, 2 * j] = qs[:, 0:HEAD_DIM]
        q_ref[0, 2 * j + 1] = qs[:, HEAD_DIM:LANES]
    ks = norm_rope(p[:, ATTN_W:ATTN_W + KV_W], kg_ref[...], 1.0).astype(BF16)
    k_ref[0, 0] = ks[:, 0:HEAD_DIM]
    k_ref[0, 1] = ks[:, HEAD_DIM:LANES]

    off = ATTN_W + KV_W

    def head(j):
        return p[:, off + j * HEAD_DIM: off + (j + 1) * HEAD_DIM]

    for j in range(12):
        ml_ref[0, j] = head(j) * HEAD_DIM ** -0.5 if 4 <= j < 8 else head(j)
    for j in range(4):
        mo_ref[0, j] = head(12 + j)
    off += 4 * MLSTM_W
    for j in range(12):
        gd_ref[0, j] = head(j)
    for j in range(4):
        gz_ref[0, j] = head(12 + j)
    off += 4 * GDN_W
    gate_ref[0] = p[:, off:off + GATE_W]


def _inproj(xa, modsel, g1, w_r, w_t, cos_t, sin_t, qg, kg, bd, tm):
    b, t, d = xa.shape
    nb = t // tm
    nc = tm // CHUNK
    kern = functools.partial(_inproj_kernel, d_model=d)
    hm_shape = lambda nh, dt: jax.ShapeDtypeStruct((b, nh, t, HEAD_DIM), dt)
    out_shapes = (hm_shape(ATTN_HEADS, BF16), hm_shape(ATTN_KV_HEADS, BF16),
                  jax.ShapeDtypeStruct((b, ATTN_KV_HEADS, HEAD_DIM + ATTN_VPAD, t), BF16),
                  hm_shape(12, F32), jax.ShapeDtypeStruct((b, MLSTM_HEADS, t // CHUNK, HEAD_DIM, CHUNK), F32),
                  hm_shape(4, F32), hm_shape(12, F32), hm_shape(4, F32),
                  jax.ShapeDtypeStruct((b, t, GATE_W), F32),
                  jax.ShapeDtypeStruct((b, t // CHUNK, GATE_ROWS, CHUNK), F32))
    hm = lambda nh: pl.BlockSpec((1, nh, tm, HEAD_DIM), lambda bi, i: (bi, 0, i, 0))
    const = lambda a: pl.BlockSpec(a.shape, lambda bi, i: (0,) * a.ndim)
    return pl.pallas_call(
        kern,
        grid=(b, nb),
        in_specs=[
            pl.BlockSpec((1, tm, d), lambda bi, i: (bi, i, 0)),
            pl.BlockSpec((None, None, 1, N_MOD * d), lambda bi, i: (bi, jnp.minimum(i, 1), 0, 0)),
            const(g1), const(w_r), const(w_t),
            pl.BlockSpec((tm, LANES), lambda bi, i: (i, 0)),
            pl.BlockSpec((tm, LANES), lambda bi, i: (i, 0)),
            const(qg), const(kg), const(bd),
        ],
        out_specs=(hm(ATTN_HEADS), hm(ATTN_KV_HEADS),
                   pl.BlockSpec((1, ATTN_KV_HEADS, HEAD_DIM + ATTN_VPAD, tm), lambda bi, i: (bi, 0, 0, i)),
                   hm(12), pl.BlockSpec((1, MLSTM_HEADS, nc, HEAD_DIM, CHUNK), lambda bi, i: (bi, 0, i, 0, 0)),
                   hm(4), hm(12), hm(4),
                   pl.BlockSpec((1, tm, GATE_W), lambda bi, i: (bi, i, 0)),
                   pl.BlockSpec((1, nc, GATE_ROWS, CHUNK), lambda bi, i: (bi, i, 0, 0))),
        out_shape=out_shapes,
        compiler_params=_cparams(("parallel", "parallel")),
        name="inproj",
    )(xa, modsel, g1, w_r, w_t, cos_t, sin_t, qg, kg, bd)


def _attn_kernel(q_ref, k_ref, vt_ref, o_ref, sa_sc, sb_sc, xa_sc, xb_sc, m_sc, acc_sc, *, tq, tk, n_pairs, blk0):
    i = pl.program_id(2) + blk0
    q = q_ref[0].reshape(ATTN_REP * tq, HEAD_DIM)
    cols = [slice(c * ATTN_COLS, (c + 1) * ATTN_COLS) for c in range(ATTN_REP * tq // ATTN_COLS)]

    m_sc[...] = jnp.full(m_sc.shape, -jnp.inf, F32)
    acc_sc[...] = jnp.zeros(acc_sc.shape, F32)

    def score(s_ref, x_ref, tile):
        start = pl.multiple_of(tile * tk, tk)
        k = k_ref[0, 0, pl.ds(start, tk), :]
        for cs in cols:
            s = _dot_nt(k, q[cs])
            s_ref[:, cs] = s
            x_ref[:, cs] = jnp.max(s, axis=0, keepdims=True)

    def consume(s_ref, x_ref, tile, half):
        start = pl.multiple_of(tile * tk, tk)
        vt = vt_ref[0, 0, :, pl.ds(start, tk)]
        for cs in cols:
            mo = m_sc[half, :, cs]
            mn = jnp.maximum(mo, x_ref[:, cs])
            p = jnp.exp2((s_ref[:, cs] - mn).astype(BF16))
            acc_sc[half, :, cs] = jnp.exp2(mo - mn) * acc_sc[half, :, cs] + _dot(vt, p)
            m_sc[half, :, cs] = mn

    score(sa_sc, xa_sc, 0)

    def body(j, carry):
        score(sb_sc, xb_sc, 2 * j + 1)
        consume(sa_sc, xa_sc, 2 * j, 0)
        score(sa_sc, xa_sc, 2 * j + 2)
        consume(sb_sc, xb_sc, 2 * j + 1, 1)
        return carry

    n = jnp.where(i == 0, 0, n_pairs)
    lax.fori_loop(0, n, body, 0)
    consume(sa_sc, xa_sc, 2 * n, 0)
    m = jnp.maximum(m_sc[0], m_sc[1])
    acc = jnp.exp2(m_sc[0] - m) * acc_sc[0] + jnp.exp2(m_sc[1] - m) * acc_sc[1]
    o_t = (acc[0:HEAD_DIM, :] / acc[HEAD_DIM:HEAD_DIM + 1, :]).astype(BF16)
    eye = (_iota((tq, tq), 0) == _iota((tq, tq), 1)).astype(BF16)
    for r in range(ATTN_REP):
        o_ref[0, r] = _dot_nt(eye, o_t[:, r * tq:(r + 1) * tq]).astype(o_ref.dtype)


def _attention(q, k, vt, ctx_len, tq, blk0):
    b, _, t, _ = q.shape
    tk = ctx_len
    assert (t // tk) % 2 == 1
    nq = t // tq - blk0
    kern = functools.partial(_attn_kernel, tq=tq, tk=tk, n_pairs=(t // tk - 1) // 2, blk0=blk0)
    rows = ATTN_REP * tq
    return pl.pallas_call(
        kern,
        grid=(b, ATTN_KV_HEADS, nq),
        in_specs=[
            pl.BlockSpec((1, ATTN_REP, tq, HEAD_DIM), lambda bi, g, i: (bi, g, i + blk0, 0)),
            pl.BlockSpec((1, 1, t, HEAD_DIM), lambda bi, g, i: (bi, g, 0, 0)),
            pl.BlockSpec((1, 1, HEAD_DIM + ATTN_VPAD, t), lambda bi, g, i: (bi, g, 0, 0)),
        ],
        out_specs=pl.BlockSpec((1, ATTN_REP, tq, HEAD_DIM), lambda bi, g, i: (bi, g, i, 0)),
        out_shape=jax.ShapeDtypeStruct((b, ATTN_HEADS, nq * tq, HEAD_DIM), BF16),
        scratch_shapes=[pltpu.VMEM((tk, rows), F32), pltpu.VMEM((tk, rows), F32),
                        pltpu.VMEM((1, rows), F32), pltpu.VMEM((1, rows), F32),
                        pltpu.VMEM((2, 1, rows), F32), pltpu.VMEM((2, HEAD_DIM + ATTN_VPAD, rows), F32)],
        compiler_params=_cparams(("parallel", "parallel", "arbitrary")),
        name="attention",
    )(q, k, vt)


def _chunk_maps(nc_c, nc_x):
    fwd = lambda j: j
    rev = lambda j: jnp.where(j < nc_c, nc_c - 1 - j, 2 * nc_c + nc_x - 1 - j)
    return fwd, rev


def _dir_masks(d):
    r = _iota((CHUNK, CHUNK), 0)
    c = _iota((CHUNK, CHUNK), 1)
    incl = (r >= c) if d == 0 else (r <= c)
    strict = (r > c) if d == 0 else (r < c)
    incl_t = (r <= c) if d == 0 else (r >= c)
    return incl, strict, incl.astype(F32), incl_t.astype(F32)


def _mlstm_kernel(xf_ref, xr_ref, ktf_ref, ktr_ref, gf_ref, gr_ref, gtf_ref, gtr_ref, brow_ref, bcol_ref,
                  hf_ref, hr_ref, c_sc, n_sc, m_sc, *, nb):
    @pl.when(pl.program_id(0) == 0)
    def _():
        c_sc[...] = jnp.zeros(c_sc.shape, F32)
        n_sc[...] = jnp.zeros(n_sc.shape, F32)
        m_sc[...] = jnp.zeros(m_sc.shape, F32)

    refs = ((xf_ref, ktf_ref, gf_ref, gtf_ref, hf_ref), (xr_ref, ktr_ref, gr_ref, gtr_ref, hr_ref))
    masks = [_dir_masks(d) for d in range(N_DIR)]
    gate = {}
    for d in range(N_DIR):
        _, _, g_ref, gt_ref, _ = refs[d]
        _, _, tri, tri_t = masks[d]
        sel = (_iota((GATE_W, MLSTM_HEADS * LANES), 0)
               == GC_MF + d * MLSTM_HEADS + _iota((GATE_W, MLSTM_HEADS * LANES), 1) // LANES).astype(BF16)
        for b in range(nb):
            cum = _dot(tri, _log_sigmoid(g_ref[b] + brow_ref[...]), precision=HIGHEST)
            g_t = gt_ref[b, 0, GC_MI:GC_MI + 16, :] + bcol_ref[...]
            lf_t = _log_sigmoid(g_t)
            cum_rep = sum(_dot(piece, sel) for piece in _split_bf16_3(cum))
            gate[d, b] = (cum_rep, g_t, lf_t, _dot(lf_t, tri_t, precision=HIGHEST))

    probs = [(d, b, h) for d in range(N_DIR) for b in range(nb) for h in range(MLSTM_HEADS)]
    st = []
    for d, b, h in probs:
        p = (d * nb + b) * MLSTM_HEADS + h
        x_ref, kt_ref = refs[d][0], refs[d][1]
        cum_rep, g_t, lf_t, cum_t = gate[d, b]
        ri = d * MLSTM_HEADS + h
        rf = GC_MF + ri
        bcum_col = cum_rep[:, h * LANES:h * LANES + CHUNK]
        bcum_row = cum_t[rf - GC_MI:rf - GC_MI + 1, :]
        i_row = g_t[ri:ri + 1, :]
        b_last = jnp.sum(lf_t[rf - GC_MI:rf - GC_MI + 1, :], axis=-1, keepdims=True)
        m_old = m_sc[p, 0:1, 0:1]
        w_end = b_last - bcum_row + i_row
        m_new = jnp.maximum(b_last + m_old, jnp.max(w_end, axis=-1, keepdims=True))
        a_row = jnp.exp(w_end - m_new)
        dec = jnp.exp(b_last + m_old - m_new)
        dmat = jnp.where(masks[d][0], bcum_col - bcum_row + i_row, -jnp.inf)
        inter = bcum_col + m_sc[p, 0:1, 0:CHUNK]
        m_t = jnp.maximum(inter, jnp.max(dmat, axis=-1, keepdims=True))
        st.append(dict(p=p, q=x_ref[b, h], k=x_ref[b, MLSTM_HEADS + h], v=x_ref[b, 2 * MLSTM_HEADS + h],
                       k_t=kt_ref[b, h, 0], c_old=c_sc[p], n_old=n_sc[p], a_row=a_row, dec=dec, m_new=m_new,
                       m_t=m_t, w_in=jnp.exp(inter - m_t), dexp=jnp.exp(dmat - m_t), out=refs[d][4], b=b, h=h))
    for e in st:
        e["s"] = _dot(e["q"], e["k_t"]) * e["dexp"]
    for e in st:
        e["qc"] = _dot(e["q"], e["c_old"])
    for e in st:
        e["sv"] = _dot(e["s"], e["v"])
    for e in st:
        e["kv"] = _dot(e["k_t"] * e["a_row"], e["v"])
    for e in st:
        e["ak"] = _dot(jnp.broadcast_to(e["a_row"], (SUBLANES, CHUNK)), e["k"])
    for e in st:
        num = e["w_in"] * e["qc"] + e["sv"]
        den = (e["w_in"] * jnp.sum(e["q"] * e["n_old"][0:1, :], axis=-1, keepdims=True)
               + jnp.sum(e["s"], axis=-1, keepdims=True))
        e["out"][e["b"], e["h"]] = num / jnp.maximum(jnp.abs(den), jnp.exp(-e["m_t"]))
    for e in st:
        p = e["p"]
        c_sc[p] = e["dec"] * e["c_old"] + e["kv"]
        n_sc[p] = e["dec"] * e["n_old"] + e["ak"]
        m_sc[p] = jnp.broadcast_to(e["m_new"], (SUBLANES, LANES))


def _mlstm_scan(mqkv, k_t, gates, gates_t, brow, bcol, ctx_len):
    b, _, t, _ = mqkv.shape
    nc_c, nc_x = ctx_len // CHUNK, (t - ctx_len) // CHUNK
    fwd, rev = _chunk_maps(nc_c, nc_x)
    nprob = N_DIR * b * MLSTM_HEADS
    xs = lambda f: pl.BlockSpec((b, 12, CHUNK, HEAD_DIM), lambda j: (0, 0, f(j), 0))
    ks = lambda f: pl.BlockSpec((b, MLSTM_HEADS, 1, HEAD_DIM, CHUNK), lambda j: (0, 0, f(j), 0, 0))
    gs = lambda f: pl.BlockSpec((b, CHUNK, GATE_W), lambda j: (0, f(j), 0))
    gts = lambda f: pl.BlockSpec((b, 1, GATE_ROWS, CHUNK), lambda j: (0, f(j), 0, 0))
    hs = lambda f: pl.BlockSpec((b, MLSTM_HEADS, CHUNK, HEAD_DIM), lambda j: (0, 0, f(j), 0))
    out = jax.ShapeDtypeStruct((b, MLSTM_HEADS, t, HEAD_DIM), F32)
    return pl.pallas_call(
        functools.partial(_mlstm_kernel, nb=b),
        grid=(nc_c + nc_x,),
        in_specs=[xs(fwd), xs(rev), ks(fwd), ks(rev), gs(fwd), gs(rev), gts(fwd), gts(rev),
                  pl.BlockSpec((1, GATE_W), lambda j: (0, 0)),
                  pl.BlockSpec((16, CHUNK), lambda j: (0, 0))],
        out_specs=(hs(fwd), hs(rev)),
        out_shape=(out, out),
        scratch_shapes=[pltpu.VMEM((nprob, HEAD_DIM, HEAD_DIM), F32),
                        pltpu.VMEM((nprob, SUBLANES, HEAD_DIM), F32),
                        pltpu.VMEM((nprob, SUBLANES, LANES), F32)],
        compiler_params=_cparams(("arbitrary",)),
        name="mlstm_scan",
    )(mqkv, mqkv, k_t, k_t, gates, gates, gates_t, gates_t, brow, bcol)


def _gdn_prep_kernel(x_ref, w_ref, o_ref, pad_sc, *, ctx_len, t):
    part = pl.program_id(1) // GDN_HEADS
    zeros = jnp.zeros((PAD_ROWS, HEAD_DIM), F32)
    pad_sc[0:PAD_ROWS] = zeros
    pad_sc[PAD_ROWS:PAD_ROWS + ctx_len] = x_ref[0, 0, 0:ctx_len]
    pad_sc[PAD_ROWS + ctx_len:2 * PAD_ROWS + ctx_len] = zeros
    pad_sc[2 * PAD_ROWS + ctx_len:2 * PAD_ROWS + t] = x_ref[0, 0, ctx_len:t]
    pad_sc[2 * PAD_ROWS + t:3 * PAD_ROWS + t] = zeros
    w = w_ref[0]
    is_qk = part < 2
    scale = jnp.where(part == 0, HEAD_DIM ** -0.5, 1.0)
    for c in range(t // CONV_ROWS):
        r0 = c * CONV_ROWS
        base = r0 + (PAD_ROWS if r0 < ctx_len else 2 * PAD_ROWS) - CONV_K // 2
        y = w[0:1, :] * pad_sc[base:base + CONV_ROWS]
        for j in range(1, CONV_K):
            y = y + w[j:j + 1, :] * pad_sc[base + j:base + j + CONV_ROWS]
        y = _silu(y)
        yn = y * lax.rsqrt(jnp.sum(y * y, axis=-1, keepdims=True) + EPS) * scale
        o_ref[0, 0, r0:r0 + CONV_ROWS] = jnp.where(is_qk, yn, y)


def _gdn_prep(gqkv, conv_w12, ctx_len):
    b, np_, t, _ = gqkv.shape
    return pl.pallas_call(
        functools.partial(_gdn_prep_kernel, ctx_len=ctx_len, t=t),
        grid=(b, np_),
        in_specs=[pl.BlockSpec((1, 1, t, HEAD_DIM), lambda bi, p: (bi, p, 0, 0)),
                  pl.BlockSpec((1, CONV_K, HEAD_DIM), lambda bi, p: (p, 0, 0))],
        out_specs=pl.BlockSpec((1, 1, t, HEAD_DIM), lambda bi, p: (bi, p, 0, 0)),
        out_shape=jax.ShapeDtypeStruct(gqkv.shape, F32),
        scratch_shapes=[pltpu.VMEM((t + 3 * PAD_ROWS, HEAD_DIM), F32)],
        compiler_params=_cparams(("parallel", "parallel")),
        name="gdn_prep",
    )(gqkv, conv_w12)


def _gdn_kernel(xf_ref, xr_ref, gf_ref, gr_ref, gtf_ref, gtr_ref, arow_ref, drow_ref, acol_ref, dcol_ref,
                of_ref, or_ref, s_sc, rhs_sc, *, nb):
    @pl.when(pl.program_id(0) == 0)
    def _():
        s_sc[...] = jnp.zeros(s_sc.shape, F32)

    refs = ((xf_ref, gf_ref, gtf_ref, of_ref), (xr_ref, gr_ref, gtr_ref, or_ref))
    masks = [_dir_masks(d) for d in range(N_DIR)]
    gate = {}
    for d in range(N_DIR):
        g_ref, gt_ref = refs[d][1], refs[d][2]
        _, _, tri, tri_t = masks[d]
        for b in range(nb):
            g = g_ref[b]
            gval = arow_ref[...] * _softplus(g + drow_ref[...])
            g_t = gt_ref[b, 0, GC_GA:GC_GA + 16, :]
            gval_t = acol_ref[...] * _softplus(g_t + dcol_ref[...])
            gate[d, b] = (_dot(tri, gval, precision=HIGHEST), _sigmoid(g), gval_t,
                          _dot(gval_t, tri_t, precision=HIGHEST))

    probs = [(d, b, h) for d in range(N_DIR) for b in range(nb) for h in range(GDN_HEADS)]
    eye = _eye_rows(HEAD_DIM, HEAD_DIM, 0).astype(BF16)
    st = []
    for d, b, h in probs:
        x_ref = refs[d][0]
        st.append(dict(p=(d * nb + b) * GDN_HEADS + h, d=d, b=b, h=h, q=x_ref[b, h], k=x_ref[b, GDN_HEADS + h],
                       v=x_ref[b, 2 * GDN_HEADS + h]))
    for e in st:
        e["k_t"] = sum(_dot_nt(eye, piece) for piece in _split_bf16(e["k"]))
    for e in st:
        d, b, h, p = e["d"], e["b"], e["h"], e["p"]
        incl, strict, _, _ = masks[d]
        gcum, beta_all, gval_t, gcum_t = gate[d, b]
        ra = d * GDN_HEADS + h
        g_col = gcum[:, GC_GA + ra:GC_GA + ra + 1]
        g_row = gcum_t[ra:ra + 1, :]
        beta = beta_all[:, GC_GB + ra:GC_GB + ra + 1]
        g_last = jnp.sum(gval_t[ra:ra + 1, :], axis=-1, keepdims=True)
        eg = jnp.exp(g_col)
        kb = e["k"] * beta
        rhs_sc[p, :, 0:HEAD_DIM] = e["v"] * beta
        rhs_sc[p, :, HEAD_DIM:] = kb * eg
        e.update(out=refs[d][3], strict=strict, eg_last=jnp.exp(g_last), kb=kb, qg=e["q"] * eg,
                 decay=jnp.where(incl, jnp.exp(jnp.where(incl, g_col - g_row, 0.0)), 0.0),
                 ktg=e["k_t"] * jnp.exp(g_last - g_row), s_old=s_sc[p])
    for e in st:
        e["sol"] = rhs_sc[e["p"]]
    for e in st:
        e["pw"] = -jnp.where(e["strict"], _dot3(e["kb"], e["k_t"]) * e["decay"], 0.0)
    n_fac = CHUNK.bit_length() - 1
    for it in range(n_fac):
        for e in st:
            e["sol"] = e["sol"] + _dot3(e["pw"], e["sol"])
        if it < n_fac - 1:
            for e in st:
                e["pw"] = _dot3(e["pw"], e["pw"])
    for e in st:
        e["ws"] = _dot(e["sol"][:, HEAD_DIM:], e["s_old"])
    for e in st:
        e["qs"] = _dot(e["qg"], e["s_old"])
    for e in st:
        e["qk"] = _dot(e["q"], e["k_t"]) * e["decay"]
    for e in st:
        e["v_new"] = e["sol"][:, :HEAD_DIM] - e["ws"]
    for e in st:
        e["out"][e["b"], e["h"]] = e["qs"] + _dot(e["qk"], e["v_new"])
    for e in st:
        s_sc[e["p"]] = e["eg_last"] * e["s_old"] + _dot(e["ktg"], e["v_new"])


def _gdn_scan(gp, gates, gates_t, arow, drow, acol, dcol, ctx_len):
    b, _, t, _ = gp.shape
    nc_c, nc_x = ctx_len // CHUNK, (t - ctx_len) // CHUNK
    fwd, rev = _chunk_maps(nc_c, nc_x)
    nprob = N_DIR * b * GDN_HEADS
    xs = lambda f: pl.BlockSpec((b, 3 * GDN_HEADS, CHUNK, HEAD_DIM), lambda j: (0, 0, f(j), 0))
    gs = lambda f: pl.BlockSpec((b, CHUNK, GATE_W), lambda j: (0, f(j), 0))
    gts = lambda f: pl.BlockSpec((b, 1, GATE_ROWS, CHUNK), lambda j: (0, f(j), 0, 0))
    hs = lambda f: pl.BlockSpec((b, GDN_HEADS, CHUNK, HEAD_DIM), lambda j: (0, 0, f(j), 0))
    row = pl.BlockSpec((1, GATE_W), lambda j: (0, 0))
    col = pl.BlockSpec((16, CHUNK), lambda j: (0, 0))
    out = jax.ShapeDtypeStruct((b, GDN_HEADS, t, HEAD_DIM), F32)
    return pl.pallas_call(
        functools.partial(_gdn_kernel, nb=b),
        grid=(nc_c + nc_x,),
        in_specs=[xs(fwd), xs(rev), gs(fwd), gs(rev), gts(fwd), gts(rev), row, row, col, col],
        out_specs=(hs(fwd), hs(rev)),
        out_shape=(out, out),
        scratch_shapes=[pltpu.VMEM((nprob, HEAD_DIM, HEAD_DIM), F32),
                        pltpu.VMEM((nprob, CHUNK, 2 * HEAD_DIM), F32)],
        compiler_params=_cparams(("arbitrary",)),
        name="gdn_scan",
    )(gp, gp, gates, gates, gates_t, gates_t, arow, drow, acol, dcol)


def _outproj_kernel(x_ref, mod_ref, ao_ref, mhf_ref, mhr_ref, mo_ref, mg_ref, ghf_ref, ghr_ref, gz_ref, gg_ref,
                    w_ref, n2_ref, rw_ref, x1_ref, hp_ref, aff_ref, mix_sc, *, d_model):
    d = d_model

    def put(j, val):
        mix_sc[:, j * HEAD_DIM:(j + 1) * HEAD_DIM] = val.astype(BF16)

    for h in range(ATTN_HEADS):
        put(h, ao_ref[0, h])
    for h in range(MLSTM_HEADS):
        hh = mhf_ref[0, h] + mhr_ref[0, h]
        hn = hh * lax.rsqrt(jnp.mean(hh * hh, axis=-1, keepdims=True) + EPS) * mg_ref[h:h + 1, :]
        put(ATTN_HEADS + h, _sigmoid(mo_ref[0, h]) * hn)
    for h in range(GDN_HEADS):
        oo = ghf_ref[0, h] + ghr_ref[0, h]
        on = oo * lax.rsqrt(jnp.mean(oo * oo, axis=-1, keepdims=True) + EPS) * gg_ref[...]
        put(ATTN_HEADS + MLSTM_HEADS + h, on * _silu(gz_ref[0, h]))
    x1 = x_ref[0] + mod_ref[:, 2 * d:3 * d] * _dot(mix_sc[...], w_ref[...])
    x1_ref[0] = x1
    xn = x1 * lax.rsqrt(jnp.mean(x1 * x1, axis=-1, keepdims=True) + EPS) * n2_ref[...]
    h2 = xn * (1.0 + mod_ref[:, 4 * d:5 * d]) + mod_ref[:, 3 * d:4 * d]
    logits = _dot_nt(rw_ref[...], h2, precision=HIGHEST)
    e = jnp.exp(logits - jnp.max(logits, axis=0, keepdims=True))
    aff_ref[0] = e / jnp.sum(e, axis=0, keepdims=True)
    hp_ref[0] = h2


def _outproj(xa, modsel, ao, mhf, mhr, mo, mg, ghf, ghr, gz, gg, w_hm, n2, rw_t, tm, blk0):
    b, t, d = xa.shape
    nb = t // tm - blk0
    t_out = nb * tm
    hm = lambda nh: pl.BlockSpec((1, nh, tm, HEAD_DIM), lambda bi, i: (bi, 0, i + blk0, 0))
    full = lambda a: pl.BlockSpec(a.shape, lambda bi, i: (0,) * a.ndim)
    return pl.pallas_call(
        functools.partial(_outproj_kernel, d_model=d),
        grid=(b, nb),
        in_specs=[
            pl.BlockSpec((1, tm, d), lambda bi, i: (bi, i + blk0, 0)),
            pl.BlockSpec((None, None, 1, N_MOD * d), lambda bi, i: (bi, jnp.minimum(i + blk0, 1), 0, 0)),
            pl.BlockSpec((1, ATTN_HEADS, tm, HEAD_DIM), lambda bi, i: (bi, 0, i, 0)),
            hm(4), hm(4), hm(4), full(mg), hm(4), hm(4), hm(4), full(gg),
            full(w_hm), full(n2), full(rw_t),
        ],
        out_specs=(pl.BlockSpec((1, tm, d), lambda bi, i: (bi, i, 0)),
                   pl.BlockSpec((1, tm, d), lambda bi, i: (bi, i, 0)),
                   pl.BlockSpec((1, N_EXPERTS, tm), lambda bi, i: (bi, 0, i))),
        out_shape=(jax.ShapeDtypeStruct((b, t_out, d), F32),
                   jax.ShapeDtypeStruct((b, t_out, d), F32),
                   jax.ShapeDtypeStruct((b, N_EXPERTS, t_out), F32)),
        scratch_shapes=[pltpu.VMEM((tm, w_hm.shape[0]), BF16)],
        compiler_params=_cparams(("parallel", "parallel")),
        name="outproj",
    )(xa, modsel, ao, mhf, mhr, mo, mg, ghf, ghr, gz, gg, w_hm, n2, rw_t)


def _route_kernel(aff_ref, idx_ref, val_ref, *, cap, rows):
    ne = N_EXPERTS
    a = aff_ref[0]

    def count(mask):
        return jnp.sum(jnp.sum(mask.astype(I32), axis=2, keepdims=True), axis=1, keepdims=True)

    tau_bits = jnp.zeros((ne, 1, 1), I32)
    for bit in range(30, -1, -1):
        cand = tau_bits | (1 << bit)
        keep = count(a >= lax.bitcast_convert_type(cand, F32)) >= cap
        tau_bits = jnp.where(keep, cand, tau_bits)
    tau = lax.bitcast_convert_type(tau_bits, F32)
    gt = a > tau
    eq = a == tau
    need = cap - count(gt)

    triu = (_iota((LANES, LANES), 0) <= _iota((LANES, LANES), 1)).astype(BF16)
    strict_lower = (_iota((rows, rows), 0) > _iota((rows, rows), 1)).astype(BF16)
    triu_r = (_iota((rows, rows), 0) <= _iota((rows, rows), 1)).astype(BF16)
    ones_r = jnp.ones((SUBLANES, LANES), BF16)

    def prefix(mask2d):
        m = mask2d.astype(BF16)
        within = _dot(m, triu)
        tot = jnp.broadcast_to(within[:, LANES - 1:LANES], (rows, LANES)).astype(BF16)
        return within, _dot(strict_lower, tot)

    lane_r = _iota((cap, rows), 1).astype(F32)
    lane_l = _iota((cap, LANES), 1).astype(F32)
    slot = _iota((cap, 1), 0).astype(F32)
    for e in range(ne):
        eq_e = eq[e]
        w_eq, before_eq = prefix(eq_e)
        rank_eq = w_eq - eq_e.astype(F32) + before_eq
        sel = gt[e] | (eq_e & (rank_eq < need[e].astype(F32)))
        rel, _ = prefix(sel)
        sel_b = sel.astype(BF16)
        row_tot = _dot_nt(ones_r, sel_b)
        row_incl = _dot(row_tot.astype(BF16), triu_r)
        row_excl = row_incl - row_tot
        kstar = jnp.sum((row_incl[0:1, :] <= slot).astype(F32), axis=-1, keepdims=True)
        onehot = (lane_r == kstar).astype(F32)
        base = jnp.sum(onehot * row_excl[0:1, :], axis=-1, keepdims=True)
        g_rel = _dot(onehot.astype(BF16), rel.astype(BF16))
        within = jnp.sum((g_rel <= slot - base).astype(F32), axis=-1, keepdims=True)
        g_aff = _dot(onehot, a[e], precision=HIGHEST)
        val_ref[0, e] = jnp.sum(jnp.where(lane_l == within, g_aff, 0.0), axis=-1, keepdims=True)
        idx_ref[0, e] = (kstar * LANES + within).astype(I32)


def _route(aff_tiles, cap):
    b, ne, rows, _ = aff_tiles.shape
    return pl.pallas_call(
        functools.partial(_route_kernel, cap=cap, rows=rows),
        grid=(b,),
        in_specs=[pl.BlockSpec((1, ne, rows, LANES), lambda bi: (bi, 0, 0, 0))],
        out_specs=(pl.BlockSpec((1, ne, cap, 1), lambda bi: (bi, 0, 0, 0)),
                   pl.BlockSpec((1, ne, cap, 1), lambda bi: (bi, 0, 0, 0))),
        out_shape=(jax.ShapeDtypeStruct((b, ne, cap, 1), I32), jax.ShapeDtypeStruct((b, ne, cap, 1), F32)),
        compiler_params=_cparams(("parallel",)),
        name="route",
    )(aff_tiles)


def _gather_kernel(idx_ref, h_ref, o_ref, rows_sc, *, cap, row_off):
    base = (pl.program_id(0) * pl.num_programs(2) + pl.program_id(2)) * cap

    def body(s, carry):
        t = idx_ref[base + s] + row_off
        rows_sc[pl.ds(s, 1), :] = h_ref[0, pl.ds(t, 1), :]
        return carry

    lax.fori_loop(0, cap, body, 0, unroll=8)
    o_ref[0, 0] = rows_sc[...].astype(BF16)


def _gather(idx_flat, h2, cap, rows_block, row_off):
    b, _, d = h2.shape
    dh = d // 2
    return pl.pallas_call(
        functools.partial(_gather_kernel, cap=cap, row_off=row_off),
        grid_spec=pltpu.PrefetchScalarGridSpec(
            num_scalar_prefetch=1,
            grid=(b, 2, N_EXPERTS),
            in_specs=[pl.BlockSpec((1, rows_block, dh), lambda bi, c, e, idx: (bi, 0, c))],
            out_specs=pl.BlockSpec((1, 1, cap, dh), lambda bi, c, e, idx: (e, bi, 0, c)),
            scratch_shapes=[pltpu.VMEM((cap, dh), F32)],
        ),
        out_shape=jax.ShapeDtypeStruct((N_EXPERTS, b, cap, d), BF16),
        compiler_params=_cparams(("arbitrary", "arbitrary", "arbitrary")),
        name="moe_gather",
    )(idx_flat, h2)


def _ffn_kernel(*refs, n_streams):
    xg_refs = refs[0:n_streams]
    val_refs = refs[n_streams:2 * n_streams]
    w1_ref, w3_ref, w2_ref = refs[2 * n_streams:2 * n_streams + 3]
    y_refs = refs[2 * n_streams + 3:]
    f = pl.program_id(1)
    w1 = w1_ref[0].astype(BF16)
    w3 = w3_ref[0].astype(BF16)
    w2 = w2_ref[0].astype(BF16)
    for xg_ref, val_ref, y_ref in zip(xg_refs, val_refs, y_refs):
        cap = xg_ref.shape[2]
        for bi in range(xg_ref.shape[1]):
            rows = slice(bi * cap, (bi + 1) * cap)
            xg = xg_ref[0, bi]
            y = _dot((_silu(_dot(xg, w1)) * _dot(xg, w3)).astype(BF16), w2)

            @pl.when(f == 0)
            def _():
                y_ref[0, rows] = y

            @pl.when(f > 0)
            def _():
                y_ref[0, rows] = y_ref[0, rows] + y

            @pl.when(f == pl.num_programs(1) - 1)
            def _():
                y_ref[0, rows] = y_ref[0, rows] * val_ref[0, rows]


def _ffn(xgs, vals, w1, w3, w2, layer, tf):
    _, ne, d, ff = w1.shape
    n_streams = len(xgs)
    in_specs, out_specs, out_shapes = [], [], []
    for xg in xgs:
        in_specs.append(pl.BlockSpec((1,) + xg.shape[1:], lambda e, f: (e, 0, 0, 0)))
    for v in vals:
        in_specs.append(pl.BlockSpec((1,) + v.shape[1:], lambda e, f: (e, 0, 0)))
    in_specs += [pl.BlockSpec((None, 1, d, tf), lambda e, f: (layer, e, 0, f)),
                 pl.BlockSpec((None, 1, d, tf), lambda e, f: (layer, e, 0, f)),
                 pl.BlockSpec((None, 1, tf, d), lambda e, f: (layer, e, f, 0))]
    for xg in xgs:
        m = xg.shape[1] * xg.shape[2]
        out_specs.append(pl.BlockSpec((1, m, d), lambda e, f: (e, 0, 0)))
        out_shapes.append(jax.ShapeDtypeStruct((ne, m, d), F32))
    return pl.pallas_call(
        functools.partial(_ffn_kernel, n_streams=n_streams),
        grid=(ne, ff // tf),
        in_specs=in_specs,
        out_specs=tuple(out_specs),
        out_shape=tuple(out_shapes),
        compiler_params=_cparams(("parallel", "arbitrary")),
        name="moe_ffn",
    )(*xgs, *vals, w1, w3, w2)


def _combine_kernel(idx_ref, y_ref, o_ref, *, cap):
    e = pl.program_id(2)

    @pl.when(e == 0)
    def _():
        o_ref[...] = jnp.zeros(o_ref.shape, F32)

    base = (pl.program_id(0) * pl.num_programs(2) + e) * cap

    def body(g, carry):
        s0 = pl.multiple_of(g * COMBINE_GROUP, COMBINE_GROUP)
        ts = [idx_ref[base + s0 + r] for r in range(COMBINE_GROUP)]
        ys = y_ref[0, pl.ds(s0, COMBINE_GROUP), :]
        rows = [o_ref[0, pl.ds(t, 1), :] for t in ts]
        for r, t in enumerate(ts):
            o_ref[0, pl.ds(t, 1), :] = rows[r] + ys[r:r + 1, :]
        return carry

    lax.fori_loop(0, cap // COMBINE_GROUP, body, 0)


def _combine(idx_flat, y, b, n_tok, cap):
    ne, _, d = y.shape
    dh = d // 2
    return pl.pallas_call(
        functools.partial(_combine_kernel, cap=cap),
        grid_spec=pltpu.PrefetchScalarGridSpec(
            num_scalar_prefetch=1,
            grid=(b, 2, ne),
            in_specs=[pl.BlockSpec((1, cap, dh), lambda bi, c, e, idx: (e, bi, c))],
            out_specs=pl.BlockSpec((1, n_tok, dh), lambda bi, c, e, idx: (bi, 0, c)),
        ),
        out_shape=jax.ShapeDtypeStruct((b, n_tok, d), F32),
        compiler_params=_cparams(("arbitrary", "arbitrary", "arbitrary")),
        name="moe_combine",
    )(idx_flat, y)


def _residual_kernel(x1_ref, mod_ref, mx_ref, mc_ref, o_ref, *, d_model, blk0):
    i = pl.program_id(1) + blk0
    moe = jnp.where(i == 0, mc_ref[0], mx_ref[0])
    o_ref[0] = x1_ref[0] + mod_ref[:, 5 * d_model:6 * d_model] * moe


def _residual(x1, modsel, moe_x, moe_c, tm, blk0):
    b, t, d = x1.shape
    nb = t // tm
    return pl.pallas_call(
        functools.partial(_residual_kernel, d_model=d, blk0=blk0),
        grid=(b, nb),
        in_specs=[pl.BlockSpec((1, tm, d), lambda bi, i: (bi, i, 0)),
                  pl.BlockSpec((None, None, 1, N_MOD * d), lambda bi, i: (bi, jnp.minimum(i + blk0, 1), 0, 0)),
                  pl.BlockSpec((1, tm, d), lambda bi, i: (bi, jnp.maximum(i + blk0 - 1, 0), 0)),
                  pl.BlockSpec((1, tm, d), lambda bi, i: (bi, 0, 0))],
        out_specs=pl.BlockSpec((1, tm, d), lambda bi, i: (bi, i, 0)),
        out_shape=jax.ShapeDtypeStruct((b, nb * tm, d), F32),
        compiler_params=_cparams(("parallel", "parallel")),
        name="moe_residual",
    )(x1, modsel, moe_x, moe_c)


def _rope_tables(ctx_len, seq):
    n = jnp.arange(seq)
    pos = jnp.stack([n // GRID_W, n % GRID_W], axis=-1).astype(F32)
    lane = jnp.arange(LANES) % HEAD_DIM
    axis = lane // 32
    n_freq = HEAD_DIM // 4
    inv = ROPE_THETA ** (-(lane % n_freq).astype(F32) / n_freq)
    ang = pos[:, axis] * inv
    sign = jnp.where((lane % 32) < 16, -1.0, 1.0)
    cos_t = jnp.concatenate([jnp.ones((ctx_len, LANES), F32), jnp.cos(ang)], axis=0)
    sin_t = jnp.concatenate([jnp.zeros((ctx_len, LANES), F32), jnp.sin(ang) * sign], axis=0)
    return cos_t, sin_t


def _reorder_w_in(w):
    d = w.shape[0]
    qk = w[:, :ATTN_W + KV_W]
    v = w[:, ATTN_W + KV_W:ATTN_W + 2 * KV_W]
    o = ATTN_W + 2 * KV_W
    ml = w[:, o:o + 4 * MLSTM_W]
    o += 4 * MLSTM_W
    mgate = w[:, o:o + 2 * N_DIR * MLSTM_HEADS]
    o += 2 * N_DIR * MLSTM_HEADS
    gd = w[:, o:o + 4 * GDN_W]
    o += 4 * GDN_W
    ggate = w[:, o:o + 2 * N_DIR * GDN_HEADS]
    pad = jnp.zeros((d, GATE_W - mgate.shape[1] - ggate.shape[1]), w.dtype)
    w_r = jnp.concatenate([qk, ml, gd, mgate, ggate, pad], axis=1)
    w_t = jnp.concatenate([v, ml[:, MLSTM_W:2 * MLSTM_W], mgate, ggate], axis=1).T
    return w_r.astype(BF16), w_t.astype(BF16)


def _gate_row(vals, col0):
    flat = vals.reshape(-1).astype(F32)
    return jnp.zeros((1, GATE_W), F32).at[0, col0:col0 + flat.shape[0]].set(flat)


def _gate_col(first, second):
    flat = jnp.concatenate([first.reshape(-1), second.reshape(-1)]).astype(F32)
    return jnp.broadcast_to(flat[:, None], (flat.shape[0], CHUNK))


def kernel(x, c, ctx, c_ctx, mod_w, mod_b, norm1_g, w_in, q_norm_g, k_norm_g, mlstm_i_bias, mlstm_f_bias,
           mlstm_out_g, gdn_conv_w, gdn_a_log, gdn_dt_bias, gdn_out_g, w_out, norm2_g, router_w, w1, w3, w2):
    b, seq, d = x.shape
    ctx_len = ctx.shape[1]
    depth = mod_w.shape[0]
    tm = ctx_len
    t = ctx_len + seq
    ne = N_EXPERTS
    cap_x = CAPACITY_FACTOR * seq // ne
    cap_c = CAPACITY_FACTOR * ctx_len // ne
    rows_c = 2 * SUBLANES

    cvec = jnp.concatenate([c, c_ctx[None, :], jnp.zeros((SUBLANES - b - 1, d), F32)], axis=0)
    mod = _modulation(cvec, mod_w, mod_b)
    cos_t, sin_t = _rope_tables(ctx_len, seq)
    bd = jnp.kron(jnp.eye(LANES // HEAD_DIM, dtype=F32), jnp.full((HEAD_DIM, HEAD_DIM), 1.0 / HEAD_DIM, F32)).astype(BF16)
    xa = jnp.concatenate([ctx, x], axis=1)
    zeros8 = jnp.zeros((N_DIR, MLSTM_HEADS), F32)

    for l in range(depth):
        need_ctx = l < depth - 1
        blk0 = 0 if need_ctx else 1
        modsel = jnp.stack([jnp.broadcast_to(mod[l, b], (b, N_MOD * d)), mod[l, :b]], axis=1)[:, :, None, :]
        w_r, w_t = _reorder_w_in(w_in[l])
        qg = jnp.tile(q_norm_g[l], LANES // HEAD_DIM)[None, :]
        kg = jnp.tile(k_norm_g[l], LANES // HEAD_DIM)[None, :]
        q, k, vt, mqkv, mkt, mo, gqkv, gz, gates, gates_t = _inproj(
            xa, modsel, norm1_g[l][None, :], w_r, w_t, cos_t, sin_t, qg, kg, bd, tm)

        ao = _attention(q, k, vt, ctx_len, tm, blk0)

        brow = _gate_row(mlstm_i_bias[l], GC_MI) + _gate_row(mlstm_f_bias[l], GC_MF)
        bcol = _gate_col(mlstm_i_bias[l], mlstm_f_bias[l])
        mhf, mhr = _mlstm_scan(mqkv, mkt, gates, gates_t, brow, bcol, ctx_len)

        conv12 = gdn_conv_w[l].reshape(CONV_K, 3 * GDN_HEADS, HEAD_DIM).transpose(1, 0, 2)
        gp = _gdn_prep(gqkv, conv12, ctx_len)
        neg_a = -jnp.exp(gdn_a_log[l].astype(F32))
        ghf, ghr = _gdn_scan(gp, gates, gates_t, _gate_row(neg_a, GC_GA), _gate_row(gdn_dt_bias[l], GC_GA),
                             _gate_col(neg_a, zeros8), _gate_col(gdn_dt_bias[l], zeros8), ctx_len)

        w_hm = w_out[l].astype(BF16)
        x1, hp, aff = _outproj(xa, modsel, ao, mhf, mhr, mo, mlstm_out_g[l], ghf, ghr, gz, gdn_out_g[l][None, :],
                               w_hm, norm2_g[l][None, :], router_w[l].T, tm, blk0)

        x_row0 = ctx_len - blk0 * tm
        idx_x, val_x = _route(aff[:, :, x_row0:].reshape(b, ne, seq // LANES, LANES), cap_x)
        idx_xf = idx_x.reshape(-1)
        xgs = [_gather(idx_xf, hp, cap_x, hp.shape[1], x_row0)]
        vals = [val_x.transpose(1, 0, 2, 3).reshape(ne, b * cap_x, 1)]
        if need_ctx:
            aff_c = jnp.pad(aff[:, :, :ctx_len], ((0, 0), (0, 0), (0, rows_c * LANES - ctx_len)), constant_values=-1.0)
            idx_c, val_c = _route(aff_c.reshape(b, ne, rows_c, LANES), cap_c)
            idx_cf = idx_c.reshape(-1)
            xgs.append(_gather(idx_cf, hp, cap_c, ctx_len, 0))
            vals.append(val_c.transpose(1, 0, 2, 3).reshape(ne, b * cap_c, 1))
        ys = _ffn(xgs, vals, w1, w3, w2, l, min(FFN_TF, w1.shape[3]))
        moe_x = _combine(idx_xf, ys[0], b, seq, cap_x)
        moe_c = _combine(idx_cf, ys[1], b, ctx_len, cap_c) if need_ctx else moe_x
        xa = _residual(x1, modsel, moe_x, moe_c, tm, blk0)
    return xa
```

```python
import functools
import math

import jax
import jax.numpy as jnp
from jax import lax
from jax.experimental import pallas as pl
from jax.experimental.pallas import tpu as pltpu

F32 = jnp.float32
BF16 = jnp.bfloat16
I32 = jnp.int32
U32 = jnp.uint32
HIGHEST = lax.Precision.HIGHEST

HEAD_DIM = 64
ATTN_HEADS = 8
ATTN_KV_HEADS = 2
ATTN_REP = ATTN_HEADS // ATTN_KV_HEADS
MLSTM_HEADS = 4
GDN_HEADS = 4
N_DIR = 2
CHUNK = 64
CONV_K = 5
GRID_W = 64
ROPE_THETA = 10000.0
N_EXPERTS = 16
CAPACITY_FACTOR = 2
N_MOD = 6
EPS = 1e-6
ATTN_W = ATTN_HEADS * HEAD_DIM
KV_W = ATTN_KV_HEADS * HEAD_DIM
MLSTM_W = MLSTM_HEADS * HEAD_DIM
GDN_W = GDN_HEADS * HEAD_DIM
LANES = 128
SUBLANES = 8
GATE_W = LANES
GATE_ROWS = 32
GC_MI, GC_MF, GC_GA, GC_GB = 0, 8, 16, 24
VMEM_LIMIT = 56 * 1024 * 1024
ATTN_COLS = 512
ATTN_VPAD = 16
LOG2E = 1.4426950408889634
FFN_TF = 512
CONV_ROWS = 256
PAD_ROWS = 8
COMBINE_GROUP = 8


def _cparams(sem):
    return pltpu.CompilerParams(dimension_semantics=sem, vmem_limit_bytes=VMEM_LIMIT)


def _sigmoid(x):
    return 1.0 / (1.0 + jnp.exp(-x))


def _silu(x):
    return x * _sigmoid(x)


def _log_sigmoid(x):
    return jnp.minimum(x, 0.0) - jnp.log1p(jnp.exp(-jnp.abs(x)))


def _softplus(x):
    return jnp.maximum(x, 0.0) + jnp.log1p(jnp.exp(-jnp.abs(x)))


def _dot(a, b, precision=None):
    return jnp.dot(a, b, preferred_element_type=F32, precision=precision)


def _dot_nt(a, b, precision=None):
    return lax.dot_general(a, b, (((1,), (1,)), ((), ())), preferred_element_type=F32, precision=precision)


def _split_bf16(a):
    hi = a.astype(BF16)
    lo = (a - hi.astype(F32)).astype(BF16)
    return hi, lo


def _split_bf16_3(a):
    hi = a.astype(BF16)
    r = a - hi.astype(F32)
    mid = r.astype(BF16)
    return hi, mid, (r - mid.astype(F32)).astype(BF16)


def _dot3(a, b):
    ah, al = _split_bf16(a)
    bh, bl = _split_bf16(b)
    return _dot(ah, bh) + (_dot(ah, bl) + _dot(al, bh))


def _iota(shape, dim):
    return lax.broadcasted_iota(I32, shape, dim)


def _eye_rows(rows, cols, first):
    return (_iota((rows, cols), 0) + first == _iota((rows, cols), 1)).astype(F32)


def _mod_kernel(c_ref, w_ref, b_ref, o_ref):
    s = _silu(c_ref[...])
    o_ref[0] = _dot(s, w_ref[0], precision=HIGHEST) + b_ref[0]


def _modulation(cvec, mod_w, mod_b):
    depth, d, n = mod_w.shape
    rows = cvec.shape[0]
    tn = d
    return pl.pallas_call(
        _mod_kernel,
        grid=(depth, n // tn),
        in_specs=[
            pl.BlockSpec((rows, d), lambda l, j: (0, 0)),
            pl.BlockSpec((1, d, tn), lambda l, j: (l, 0, j)),
            pl.BlockSpec((1, 1, tn), lambda l, j: (l, 0, j)),
        ],
        out_specs=pl.BlockSpec((1, rows, tn), lambda l, j: (l, 0, j)),
        out_shape=jax.ShapeDtypeStruct((depth, rows, n), F32),
        compiler_params=_cparams(("parallel", "parallel")),
        name="modulation",
    )(cvec, mod_w, mod_b.reshape(depth, 1, n))


def _inproj_kernel(x_ref, mod_ref, g_ref, w_ref, wt_ref, cos_ref, sin_ref, qg_ref, kg_ref, bd_ref,
                   q_ref, k_ref, vt_ref, ml_ref, mkt_ref, mo_ref, gd_ref, gz_ref, gate_ref, gatet_ref, *, d_model):
    x = x_ref[0]
    tm = x.shape[0]
    sh = mod_ref[:, 0:d_model]
    sc = mod_ref[:, d_model:2 * d_model]
    xn = x * lax.rsqrt(jnp.mean(x * x, axis=-1, keepdims=True) + EPS) * g_ref[...]
    h = (xn * (1.0 + sc) + sh).astype(BF16)
    p = _dot(h, w_ref[...])
    pt = _dot_nt(wt_ref[...], h)
    for g in range(ATTN_KV_HEADS):
        vt_ref[0, g, 0:HEAD_DIM, :] = pt[g * HEAD_DIM:(g + 1) * HEAD_DIM].astype(BF16)
        vt_ref[0, g, HEAD_DIM:, :] = (_iota((ATTN_VPAD, tm), 0) == 0).astype(BF16)
    for c in range(tm // CHUNK):
        cs = slice(c * CHUNK, (c + 1) * CHUNK)
        for j in range(MLSTM_HEADS):
            mkt_ref[0, j, c] = pt[KV_W + j * HEAD_DIM:KV_W + (j + 1) * HEAD_DIM, cs] * HEAD_DIM ** -0.5
        gatet_ref[0, c] = pt[KV_W + MLSTM_W:KV_W + MLSTM_W + GATE_ROWS, cs]

    cos = cos_ref[...]
    sin = sin_ref[...]
    first_half = (_iota(cos.shape, 1) % 32) < 16

    def norm_rope(xs, g, scale):
        ms = sum(_dot(piece, bd_ref[...]) for piece in _split_bf16(xs * xs))
        xn_ = xs * lax.rsqrt(ms + EPS) * g
        sw = jnp.where(first_half, pltpu.roll(xn_, LANES - 16, 1), pltpu.roll(xn_, 16, 1))
        return (xn_ * cos + sw * sin) * scale

    for j in range(ATTN_W // LANES):
        qs = norm_rope(p[:, j * LANES:(j + 1) * LANES], qg_ref[...], LOG2E * HEAD_DIM ** -0.5).astype(BF16)
        q_ref[0, 2 * j] = qs[:, 0:HEAD_DIM]
        q_ref[0, 2 * j + 1] = qs[:, HEAD_DIM:LANES]
    ks = norm_rope(p[:, ATTN_W:ATTN_W + KV_W], kg_ref[...], 1.0).astype(BF16)
    k_ref[0, 0] = ks[:, 0:HEAD_DIM]
    k_ref[0, 1] = ks[:, HEAD_DIM:LANES]

    off = ATTN_W + KV_W

    def head(j):
        return p[:, off + j * HEAD_DIM: off + (j + 1) * HEAD_DIM]

    for j in range(12):
        ml_ref[0, j] = head(j) * HEAD_DIM ** -0.5 if 4 <= j < 8 else head(j)
    for j in range(4):
        mo_ref[0, j] = head(12 + j)
    off += 4 * MLSTM_W
    for j in range(12):
        gd_ref[0, j] = head(j)
    for j in range(4):
        gz_ref[0, j] = head(12 + j)
    off += 4 * GDN_W
    gate_ref[0] = p[:, off:off + GATE_W]


def _inproj(xa, modsel, g1, w_r, w_t, cos_t, sin_t, qg, kg, bd, tm):
    b, t, d = xa.shape
    nb = t // tm
    nc = tm // CHUNK
    kern = functools.partial(_inproj_kernel, d_model=d)
    hm_shape = lambda nh, dt: jax.ShapeDtypeStruct((b, nh, t, HEAD_DIM), dt)
    out_shapes = (hm_shape(ATTN_HEADS, BF16), hm_shape(ATTN_KV_HEADS, BF16),
                  jax.ShapeDtypeStruct((b, ATTN_KV_HEADS, HEAD_DIM + ATTN_VPAD, t), BF16),
                  hm_shape(12, F32), jax.ShapeDtypeStruct((b, MLSTM_HEADS, t // CHUNK, HEAD_DIM, CHUNK), F32),
                  hm_shape(4, F32), hm_shape(12, F32), hm_shape(4, F32),
                  jax.ShapeDtypeStruct((b, t, GATE_W), F32),
                  jax.ShapeDtypeStruct((b, t // CHUNK, GATE_ROWS, CHUNK), F32))
    hm = lambda nh: pl.BlockSpec((1, nh, tm, HEAD_DIM), lambda bi, i: (bi, 0, i, 0))
    const = lambda a: pl.BlockSpec(a.shape, lambda bi, i: (0,) * a.ndim)
    return pl.pallas_call(
        kern,
        grid=(b, nb),
        in_specs=[
            pl.BlockSpec((1, tm, d), lambda bi, i: (bi, i, 0)),
            pl.BlockSpec((None, None, 1, N_MOD * d), lambda bi, i: (bi, jnp.minimum(i, 1), 0, 0)),
            const(g1), const(w_r), const(w_t),
            pl.BlockSpec((tm, LANES), lambda bi, i: (i, 0)),
            pl.BlockSpec((tm, LANES), lambda bi, i: (i, 0)),
            const(qg), const(kg), const(bd),
        ],
        out_specs=(hm(ATTN_HEADS), hm(ATTN_KV_HEADS),
                   pl.BlockSpec((1, ATTN_KV_HEADS, HEAD_DIM + ATTN_VPAD, tm), lambda bi, i: (bi, 0, 0, i)),
                   hm(12), pl.BlockSpec((1, MLSTM_HEADS, nc, HEAD_DIM, CHUNK), lambda bi, i: (bi, 0, i, 0, 0)),
                   hm(4), hm(12), hm(4),
                   pl.BlockSpec((1, tm, GATE_W), lambda bi, i: (bi, i, 0)),
                   pl.BlockSpec((1, nc, GATE_ROWS, CHUNK), lambda bi, i: (bi, i, 0, 0))),
        out_shape=out_shapes,
        compiler_params=_cparams(("parallel", "parallel")),
        name="inproj",
    )(xa, modsel, g1, w_r, w_t, cos_t, sin_t, qg, kg, bd)


def _attn_kernel(q_ref, k_ref, vt_ref, o_ref, sa_sc, sb_sc, xa_sc, xb_sc, m_sc, acc_sc, *, tq, tk, n_pairs, blk0):
    i = pl.program_id(2) + blk0
    q = q_ref[0].reshape(ATTN_REP * tq, HEAD_DIM)
    cols = [slice(c * ATTN_COLS, (c + 1) * ATTN_COLS) for c in range(ATTN_REP * tq // ATTN_COLS)]

    m_sc[...] = jnp.full(m_sc.shape, -jnp.inf, F32)
    acc_sc[...] = jnp.zeros(acc_sc.shape, F32)

    def score(s_ref, x_ref, tile):
        start = pl.multiple_of(tile * tk, tk)
        k = k_ref[0, 0, pl.ds(start, tk), :]
        for cs in cols:
            s = _dot_nt(k, q[cs])
            s_ref[:, cs] = s
            x_ref[:, cs] = jnp.max(s, axis=0, keepdims=True)

    def consume(s_ref, x_ref, tile, half):
        start = pl.multiple_of(tile * tk, tk)
        vt = vt_ref[0, 0, :, pl.ds(start, tk)]
        for cs in cols:
            mo = m_sc[half, :, cs]
            mn = jnp.maximum(mo, x_ref[:, cs])
            p = jnp.exp2(s_ref[:, cs] - mn)
            acc_sc[half, :, cs] = jnp.exp2(mo - mn) * acc_sc[half, :, cs] + _dot(vt, p.astype(BF16))
            m_sc[half, :, cs] = mn

    score(sa_sc, xa_sc, 0)

    def body(j, carry):
        score(sb_sc, xb_sc, 2 * j + 1)
        consume(sa_sc, xa_sc, 2 * j, 0)
        score(sa_sc, xa_sc, 2 * j + 2)
        consume(sb_sc, xb_sc, 2 * j + 1, 1)
        return carry

    n = jnp.where(i == 0, 0, n_pairs)
    lax.fori_loop(0, n, body, 0)
    consume(sa_sc, xa_sc, 2 * n, 0)
    m = jnp.maximum(m_sc[0], m_sc[1])
    acc = jnp.exp2(m_sc[0] - m) * acc_sc[0] + jnp.exp2(m_sc[1] - m) * acc_sc[1]
    o_t = (acc[0:HEAD_DIM, :] / acc[HEAD_DIM:HEAD_DIM + 1, :]).astype(BF16)
    eye = (_iota((tq, tq), 0) == _iota((tq, tq), 1)).astype(BF16)
    for r in range(ATTN_REP):
        o_ref[0, r] = _dot_nt(eye, o_t[:, r * tq:(r + 1) * tq]).astype(o_ref.dtype)


def _attention(q, k, vt, ctx_len, tq, blk0):
    b, _, t, _ = q.shape
    tk = ctx_len
    assert (t // tk) % 2 == 1
    nq = t // tq - blk0
    kern = functools.partial(_attn_kernel, tq=tq, tk=tk, n_pairs=(t // tk - 1) // 2, blk0=blk0)
    rows = ATTN_REP * tq
    return pl.pallas_call(
        kern,
        grid=(b, ATTN_KV_HEADS, nq),
        in_specs=[
            pl.BlockSpec((1, ATTN_REP, tq, HEAD_DIM), lambda bi, g, i: (bi, g, i + blk0, 0)),
            pl.BlockSpec((1, 1, t, HEAD_DIM), lambda bi, g, i: (bi, g, 0, 0)),
            pl.BlockSpec((1, 1, HEAD_DIM + ATTN_VPAD, t), lambda bi, g, i: (bi, g, 0, 0)),
        ],
        out_specs=pl.BlockSpec((1, ATTN_REP, tq, HEAD_DIM), lambda bi, g, i: (bi, g, i, 0)),
        out_shape=jax.ShapeDtypeStruct((b, ATTN_HEADS, nq * tq, HEAD_DIM), BF16),
        scratch_shapes=[pltpu.VMEM((tk, rows), F32), pltpu.VMEM((tk, rows), F32),
                        pltpu.VMEM((1, rows), F32), pltpu.VMEM((1, rows), F32),
                        pltpu.VMEM((2, 1, rows), F32), pltpu.VMEM((2, HEAD_DIM + ATTN_VPAD, rows), F32)],
        compiler_params=_cparams(("parallel", "parallel", "arbitrary")),
        name="attention",
    )(q, k, vt)


def _chunk_maps(nc_c, nc_x):
    fwd = lambda j: j
    rev = lambda j: jnp.where(j < nc_c, nc_c - 1 - j, 2 * nc_c + nc_x - 1 - j)
    return fwd, rev


def _dir_masks(d):
    r = _iota((CHUNK, CHUNK), 0)
    c = _iota((CHUNK, CHUNK), 1)
    incl = (r >= c) if d == 0 else (r <= c)
    strict = (r > c) if d == 0 else (r < c)
    incl_t = (r <= c) if d == 0 else (r >= c)
    return incl, strict, incl.astype(F32), incl_t.astype(F32)


def _mlstm_kernel(xf_ref, xr_ref, ktf_ref, ktr_ref, gf_ref, gr_ref, gtf_ref, gtr_ref, brow_ref, bcol_ref,
                  hf_ref, hr_ref, c_sc, n_sc, m_sc, *, nb):
    @pl.when(pl.program_id(0) == 0)
    def _():
        c_sc[...] = jnp.zeros(c_sc.shape, F32)
        n_sc[...] = jnp.zeros(n_sc.shape, F32)
        m_sc[...] = jnp.zeros(m_sc.shape, F32)

    refs = ((xf_ref, ktf_ref, gf_ref, gtf_ref, hf_ref), (xr_ref, ktr_ref, gr_ref, gtr_ref, hr_ref))
    masks = [_dir_masks(d) for d in range(N_DIR)]
    gate = {}
    for d in range(N_DIR):
        _, _, g_ref, gt_ref, _ = refs[d]
        _, _, tri, tri_t = masks[d]
        sel = (_iota((GATE_W, MLSTM_HEADS * LANES), 0)
               == GC_MF + d * MLSTM_HEADS + _iota((GATE_W, MLSTM_HEADS * LANES), 1) // LANES).astype(BF16)
        for b in range(nb):
            cum = _dot(tri, _log_sigmoid(g_ref[b] + brow_ref[...]), precision=HIGHEST)
            g_t = gt_ref[b, 0, GC_MI:GC_MI + 16, :] + bcol_ref[...]
            lf_t = _log_sigmoid(g_t)
            cum_rep = sum(_dot(piece, sel) for piece in _split_bf16_3(cum))
            gate[d, b] = (cum_rep, g_t, lf_t, _dot(lf_t, tri_t, precision=HIGHEST))

    probs = [(d, b, h) for d in range(N_DIR) for b in range(nb) for h in range(MLSTM_HEADS)]
    st = []
    for d, b, h in probs:
        p = (d * nb + b) * MLSTM_HEADS + h
        x_ref, kt_ref = refs[d][0], refs[d][1]
        cum_rep, g_t, lf_t, cum_t = gate[d, b]
        ri = d * MLSTM_HEADS + h
        rf = GC_MF + ri
        bcum_col = cum_rep[:, h * LANES:h * LANES + CHUNK]
        bcum_row = cum_t[rf - GC_MI:rf - GC_MI + 1, :]
        i_row = g_t[ri:ri + 1, :]
        b_last = jnp.sum(lf_t[rf - GC_MI:rf - GC_MI + 1, :], axis=-1, keepdims=True)
        m_old = m_sc[p, 0:1, 0:1]
        w_end = b_last - bcum_row + i_row
        m_new = jnp.maximum(b_last + m_old, jnp.max(w_end, axis=-1, keepdims=True))
        a_row = jnp.exp(w_end - m_new)
        dec = jnp.exp(b_last + m_old - m_new)
        dmat = jnp.where(masks[d][0], bcum_col - bcum_row + i_row, -jnp.inf)
        inter = bcum_col + m_sc[p, 0:1, 0:CHUNK]
        m_t = jnp.maximum(inter, jnp.max(dmat, axis=-1, keepdims=True))
        st.append(dict(p=p, q=x_ref[b, h], k=x_ref[b, MLSTM_HEADS + h], v=x_ref[b, 2 * MLSTM_HEADS + h],
                       k_t=kt_ref[b, h, 0], c_old=c_sc[p], n_old=n_sc[p], a_row=a_row, dec=dec, m_new=m_new,
                       m_t=m_t, w_in=jnp.exp(inter - m_t), dexp=jnp.exp(dmat - m_t), out=refs[d][4], b=b, h=h))
    for e in st:
        e["s"] = _dot(e["q"], e["k_t"]) * e["dexp"]
    for e in st:
        e["qc"] = _dot(e["q"], e["c_old"])
    for e in st:
        e["sv"] = _dot(e["s"], e["v"])
    for e in st:
        e["kv"] = _dot(e["k_t"] * e["a_row"], e["v"])
    for e in st:
        e["ak"] = _dot(jnp.broadcast_to(e["a_row"], (SUBLANES, CHUNK)), e["k"])
    for e in st:
        num = e["w_in"] * e["qc"] + e["sv"]
        den = (e["w_in"] * jnp.sum(e["q"] * e["n_old"][0:1, :], axis=-1, keepdims=True)
               + jnp.sum(e["s"], axis=-1, keepdims=True))
        e["out"][e["b"], e["h"]] = num / jnp.maximum(jnp.abs(den), jnp.exp(-e["m_t"]))
    for e in st:
        p = e["p"]
        c_sc[p] = e["dec"] * e["c_old"] + e["kv"]
        n_sc[p] = e["dec"] * e["n_old"] + e["ak"]
        m_sc[p] = jnp.broadcast_to(e["m_new"], (SUBLANES, LANES))


def _mlstm_scan(mqkv, k_t, gates, gates_t, brow, bcol, ctx_len):
    b, _, t, _ = mqkv.shape
    nc_c, nc_x = ctx_len // CHUNK, (t - ctx_len) // CHUNK
    fwd, rev = _chunk_maps(nc_c, nc_x)
    nprob = N_DIR * b * MLSTM_HEADS
    xs = lambda f: pl.BlockSpec((b, 12, CHUNK, HEAD_DIM), lambda j: (0, 0, f(j), 0))
    ks = lambda f: pl.BlockSpec((b, MLSTM_HEADS, 1, HEAD_DIM, CHUNK), lambda j: (0, 0, f(j), 0, 0))
    gs = lambda f: pl.BlockSpec((b, CHUNK, GATE_W), lambda j: (0, f(j), 0))
    gts = lambda f: pl.BlockSpec((b, 1, GATE_ROWS, CHUNK), lambda j: (0, f(j), 0, 0))
    hs = lambda f: pl.BlockSpec((b, MLSTM_HEADS, CHUNK, HEAD_DIM), lambda j: (0, 0, f(j), 0))
    out = jax.ShapeDtypeStruct((b, MLSTM_HEADS, t, HEAD_DIM), F32)
    return pl.pallas_call(
        functools.partial(_mlstm_kernel, nb=b),
        grid=(nc_c + nc_x,),
        in_specs=[xs(fwd), xs(rev), ks(fwd), ks(rev), gs(fwd), gs(rev), gts(fwd), gts(rev),
                  pl.BlockSpec((1, GATE_W), lambda j: (0, 0)),
                  pl.BlockSpec((16, CHUNK), lambda j: (0, 0))],
        out_specs=(hs(fwd), hs(rev)),
        out_shape=(out, out),
        scratch_shapes=[pltpu.VMEM((nprob, HEAD_DIM, HEAD_DIM), F32),
                        pltpu.VMEM((nprob, SUBLANES, HEAD_DIM), F32),
                        pltpu.VMEM((nprob, SUBLANES, LANES), F32)],
        compiler_params=_cparams(("arbitrary",)),
        name="mlstm_scan",
    )(mqkv, mqkv, k_t, k_t, gates, gates, gates_t, gates_t, brow, bcol)


def _gdn_prep_kernel(x_ref, w_ref, o_ref, pad_sc, *, ctx_len, t):
    part = pl.program_id(1) // GDN_HEADS
    zeros = jnp.zeros((PAD_ROWS, HEAD_DIM), F32)
    pad_sc[0:PAD_ROWS] = zeros
    pad_sc[PAD_ROWS:PAD_ROWS + ctx_len] = x_ref[0, 0, 0:ctx_len]
    pad_sc[PAD_ROWS + ctx_len:2 * PAD_ROWS + ctx_len] = zeros
    pad_sc[2 * PAD_ROWS + ctx_len:2 * PAD_ROWS + t] = x_ref[0, 0, ctx_len:t]
    pad_sc[2 * PAD_ROWS + t:3 * PAD_ROWS + t] = zeros
    w = w_ref[0]
    is_qk = part < 2
    scale = jnp.where(part == 0, HEAD_DIM ** -0.5, 1.0)
    for c in range(t // CONV_ROWS):
        r0 = c * CONV_ROWS
        base = r0 + (PAD_ROWS if r0 < ctx_len else 2 * PAD_ROWS) - CONV_K // 2
        y = w[0:1, :] * pad_sc[base:base + CONV_ROWS]
        for j in range(1, CONV_K):
            y = y + w[j:j + 1, :] * pad_sc[base + j:base + j + CONV_ROWS]
        y = _silu(y)
        yn = y * lax.rsqrt(jnp.sum(y * y, axis=-1, keepdims=True) + EPS) * scale
        o_ref[0, 0, r0:r0 + CONV_ROWS] = jnp.where(is_qk, yn, y)


def _gdn_prep(gqkv, conv_w12, ctx_len):
    b, np_, t, _ = gqkv.shape
    return pl.pallas_call(
        functools.partial(_gdn_prep_kernel, ctx_len=ctx_len, t=t),
        grid=(b, np_),
        in_specs=[pl.BlockSpec((1, 1, t, HEAD_DIM), lambda bi, p: (bi, p, 0, 0)),
                  pl.BlockSpec((1, CONV_K, HEAD_DIM), lambda bi, p: (p, 0, 0))],
        out_specs=pl.BlockSpec((1, 1, t, HEAD_DIM), lambda bi, p: (bi, p, 0, 0)),
        out_shape=jax.ShapeDtypeStruct(gqkv.shape, F32),
        scratch_shapes=[pltpu.VMEM((t + 3 * PAD_ROWS, HEAD_DIM), F32)],
        compiler_params=_cparams(("parallel", "parallel")),
        name="gdn_prep",
    )(gqkv, conv_w12)


def _gdn_kernel(xf_ref, xr_ref, gf_ref, gr_ref, gtf_ref, gtr_ref, arow_ref, drow_ref, acol_ref, dcol_ref,
                of_ref, or_ref, s_sc, rhs_sc, *, nb):
    @pl.when(pl.program_id(0) == 0)
    def _():
        s_sc[...] = jnp.zeros(s_sc.shape, F32)

    refs = ((xf_ref, gf_ref, gtf_ref, of_ref), (xr_ref, gr_ref, gtr_ref, or_ref))
    masks = [_dir_masks(d) for d in range(N_DIR)]
    gate = {}
    for d in range(N_DIR):
        g_ref, gt_ref = refs[d][1], refs[d][2]
        _, _, tri, tri_t = masks[d]
        for b in range(nb):
            g = g_ref[b]
            gval = arow_ref[...] * _softplus(g + drow_ref[...])
            g_t = gt_ref[b, 0, GC_GA:GC_GA + 16, :]
            gval_t = acol_ref[...] * _softplus(g_t + dcol_ref[...])
            gate[d, b] = (_dot(tri, gval, precision=HIGHEST), _sigmoid(g), gval_t,
                          _dot(gval_t, tri_t, precision=HIGHEST))

    probs = [(d, b, h) for d in range(N_DIR) for b in range(nb) for h in range(GDN_HEADS)]
    eye = _eye_rows(HEAD_DIM, HEAD_DIM, 0).astype(BF16)
    st = []
    for d, b, h in probs:
        x_ref = refs[d][0]
        st.append(dict(p=(d * nb + b) * GDN_HEADS + h, d=d, b=b, h=h, q=x_ref[b, h], k=x_ref[b, GDN_HEADS + h],
                       v=x_ref[b, 2 * GDN_HEADS + h]))
    for e in st:
        e["k_t"] = sum(_dot_nt(eye, piece) for piece in _split_bf16(e["k"]))
    for e in st:
        d, b, h, p = e["d"], e["b"], e["h"], e["p"]
        incl, strict, _, _ = masks[d]
        gcum, beta_all, gval_t, gcum_t = gate[d, b]
        ra = d * GDN_HEADS + h
        g_col = gcum[:, GC_GA + ra:GC_GA + ra + 1]
        g_row = gcum_t[ra:ra + 1, :]
        beta = beta_all[:, GC_GB + ra:GC_GB + ra + 1]
        g_last = jnp.sum(gval_t[ra:ra + 1, :], axis=-1, keepdims=True)
        eg = jnp.exp(g_col)
        kb = e["k"] * beta
        rhs_sc[p, :, 0:HEAD_DIM] = e["v"] * beta
        rhs_sc[p, :, HEAD_DIM:] = kb * eg
        e.update(out=refs[d][3], strict=strict, eg_last=jnp.exp(g_last), kb=kb, qg=e["q"] * eg,
                 decay=jnp.where(incl, jnp.exp(jnp.where(incl, g_col - g_row, 0.0)), 0.0),
                 ktg=e["k_t"] * jnp.exp(g_last - g_row), s_old=s_sc[p])
    for e in st:
        e["sol"] = rhs_sc[e["p"]]
    for e in st:
        e["pw"] = -jnp.where(e["strict"], _dot3(e["kb"], e["k_t"]) * e["decay"], 0.0)
    n_fac = CHUNK.bit_length() - 1
    for it in range(n_fac):
        for e in st:
            e["sol"] = e["sol"] + _dot3(e["pw"], e["sol"])
        if it < n_fac - 1:
            for e in st:
                e["pw"] = _dot3(e["pw"], e["pw"])
    for e in st:
        e["ws"] = _dot(e["sol"][:, HEAD_DIM:], e["s_old"])
    for e in st:
        e["qs"] = _dot(e["qg"], e["s_old"])
    for e in st:
        e["qk"] = _dot(e["q"], e["k_t"]) * e["decay"]
    for e in st:
        e["v_new"] = e["sol"][:, :HEAD_DIM] - e["ws"]
    for e in st:
        e["out"][e["b"], e["h"]] = e["qs"] + _dot(e["qk"], e["v_new"])
    for e in st:
        s_sc[e["p"]] = e["eg_last"] * e["s_old"] + _dot(e["ktg"], e["v_new"])


def _gdn_scan(gp, gates, gates_t, arow, drow, acol, dcol, ctx_len):
    b, _, t, _ = gp.shape
    nc_c, nc_x = ctx_len // CHUNK, (t - ctx_len) // CHUNK
    fwd, rev = _chunk_maps(nc_c, nc_x)
    nprob = N_DIR * b * GDN_HEADS
    xs = lambda f: pl.BlockSpec((b, 3 * GDN_HEADS, CHUNK, HEAD_DIM), lambda j: (0, 0, f(j), 0))
    gs = lambda f: pl.BlockSpec((b, CHUNK, GATE_W), lambda j: (0, f(j), 0))
    gts = lambda f: pl.BlockSpec((b, 1, GATE_ROWS, CHUNK), lambda j: (0, f(j), 0, 0))
    hs = lambda f: pl.BlockSpec((b, GDN_HEADS, CHUNK, HEAD_DIM), lambda j: (0, 0, f(j), 0))
    row = pl.BlockSpec((1, GATE_W), lambda j: (0, 0))
    col = pl.BlockSpec((16, CHUNK), lambda j: (0, 0))
    out = jax.ShapeDtypeStruct((b, GDN_HEADS, t, HEAD_DIM), F32)
    return pl.pallas_call(
        functools.partial(_gdn_kernel, nb=b),
        grid=(nc_c + nc_x,),
        in_specs=[xs(fwd), xs(rev), gs(fwd), gs(rev), gts(fwd), gts(rev), row, row, col, col],
        out_specs=(hs(fwd), hs(rev)),
        out_shape=(out, out),
        scratch_shapes=[pltpu.VMEM((nprob, HEAD_DIM, HEAD_DIM), F32),
                        pltpu.VMEM((nprob, CHUNK, 2 * HEAD_DIM), F32)],
        compiler_params=_cparams(("arbitrary",)),
        name="gdn_scan",
    )(gp, gp, gates, gates, gates_t, gates_t, arow, drow, acol, dcol)


def _outproj_kernel(x_ref, mod_ref, ao_ref, mhf_ref, mhr_ref, mo_ref, mg_ref, ghf_ref, ghr_ref, gz_ref, gg_ref,
                    w_ref, n2_ref, rw_ref, x1_ref, hp_ref, aff_ref, mix_sc, *, d_model):
    d = d_model

    def put(j, val):
        mix_sc[:, j * HEAD_DIM:(j + 1) * HEAD_DIM] = val.astype(BF16)

    for h in range(ATTN_HEADS):
        put(h, ao_ref[0, h])
    for h in range(MLSTM_HEADS):
        hh = mhf_ref[0, h] + mhr_ref[0, h]
        hn = hh * lax.rsqrt(jnp.mean(hh * hh, axis=-1, keepdims=True) + EPS) * mg_ref[h:h + 1, :]
        put(ATTN_HEADS + h, _sigmoid(mo_ref[0, h]) * hn)
    for h in range(GDN_HEADS):
        oo = ghf_ref[0, h] + ghr_ref[0, h]
        on = oo * lax.rsqrt(jnp.mean(oo * oo, axis=-1, keepdims=True) + EPS) * gg_ref[...]
        put(ATTN_HEADS + MLSTM_HEADS + h, on * _silu(gz_ref[0, h]))
    x1 = x_ref[0] + mod_ref[:, 2 * d:3 * d] * _dot(mix_sc[...], w_ref[...])
    x1_ref[0] = x1
    xn = x1 * lax.rsqrt(jnp.mean(x1 * x1, axis=-1, keepdims=True) + EPS) * n2_ref[...]
    h2 = xn * (1.0 + mod_ref[:, 4 * d:5 * d]) + mod_ref[:, 3 * d:4 * d]
    logits = _dot_nt(rw_ref[...], h2, precision=HIGHEST)
    e = jnp.exp(logits - jnp.max(logits, axis=0, keepdims=True))
    aff_ref[0] = e / jnp.sum(e, axis=0, keepdims=True)
    hp_ref[0] = h2


def _outproj(xa, modsel, ao, mhf, mhr, mo, mg, ghf, ghr, gz, gg, w_hm, n2, rw_t, tm, blk0):
    b, t, d = xa.shape
    nb = t // tm - blk0
    t_out = nb * tm
    hm = lambda nh: pl.BlockSpec((1, nh, tm, HEAD_DIM), lambda bi, i: (bi, 0, i + blk0, 0))
    full = lambda a: pl.BlockSpec(a.shape, lambda bi, i: (0,) * a.ndim)
    return pl.pallas_call(
        functools.partial(_outproj_kernel, d_model=d),
        grid=(b, nb),
        in_specs=[
            pl.BlockSpec((1, tm, d), lambda bi, i: (bi, i + blk0, 0)),
            pl.BlockSpec((None, None, 1, N_MOD * d), lambda bi, i: (bi, jnp.minimum(i + blk0, 1), 0, 0)),
            pl.BlockSpec((1, ATTN_HEADS, tm, HEAD_DIM), lambda bi, i: (bi, 0, i, 0)),
            hm(4), hm(4), hm(4), full(mg), hm(4), hm(4), hm(4), full(gg),
            full(w_hm), full(n2), full(rw_t),
        ],
        out_specs=(pl.BlockSpec((1, tm, d), lambda bi, i: (bi, i, 0)),
                   pl.BlockSpec((1, tm, d), lambda bi, i: (bi, i, 0)),
                   pl.BlockSpec((1, N_EXPERTS, tm), lambda bi, i: (bi, 0, i))),
        out_shape=(jax.ShapeDtypeStruct((b, t_out, d), F32),
                   jax.ShapeDtypeStruct((b, t_out, d), F32),
                   jax.ShapeDtypeStruct((b, N_EXPERTS, t_out), F32)),
        scratch_shapes=[pltpu.VMEM((tm, w_hm.shape[0]), BF16)],
        compiler_params=_cparams(("parallel", "parallel")),
        name="outproj",
    )(xa, modsel, ao, mhf, mhr, mo, mg, ghf, ghr, gz, gg, w_hm, n2, rw_t)


def _route_kernel(aff_ref, idx_ref, val_ref, *, cap, rows):
    ne = N_EXPERTS
    a = aff_ref[0]

    def count(mask):
        return jnp.sum(jnp.sum(mask.astype(I32), axis=2, keepdims=True), axis=1, keepdims=True)

    tau_bits = jnp.zeros((ne, 1, 1), I32)
    for bit in range(30, -1, -1):
        cand = tau_bits | (1 << bit)
        keep = count(a >= lax.bitcast_convert_type(cand, F32)) >= cap
        tau_bits = jnp.where(keep, cand, tau_bits)
    tau = lax.bitcast_convert_type(tau_bits, F32)
    gt = a > tau
    eq = a == tau
    need = cap - count(gt)

    triu = (_iota((LANES, LANES), 0) <= _iota((LANES, LANES), 1)).astype(BF16)
    strict_lower = (_iota((rows, rows), 0) > _iota((rows, rows), 1)).astype(BF16)
    triu_r = (_iota((rows, rows), 0) <= _iota((rows, rows), 1)).astype(BF16)
    ones_r = jnp.ones((SUBLANES, LANES), BF16)

    def prefix(mask2d):
        m = mask2d.astype(BF16)
        within = _dot(m, triu)
        tot = jnp.broadcast_to(within[:, LANES - 1:LANES], (rows, LANES)).astype(BF16)
        return within, _dot(strict_lower, tot)

    lane_r = _iota((cap, rows), 1).astype(F32)
    lane_l = _iota((cap, LANES), 1).astype(F32)
    slot = _iota((cap, 1), 0).astype(F32)
    for e in range(ne):
        eq_e = eq[e]
        w_eq, before_eq = prefix(eq_e)
        rank_eq = w_eq - eq_e.astype(F32) + before_eq
        sel = gt[e] | (eq_e & (rank_eq < need[e].astype(F32)))
        rel, _ = prefix(sel)
        sel_b = sel.astype(BF16)
        row_tot = _dot_nt(ones_r, sel_b)
        row_incl = _dot(row_tot.astype(BF16), triu_r)
        row_excl = row_incl - row_tot
        kstar = jnp.sum((row_incl[0:1, :] <= slot).astype(F32), axis=-1, keepdims=True)
        onehot = (lane_r == kstar).astype(F32)
        base = jnp.sum(onehot * row_excl[0:1, :], axis=-1, keepdims=True)
        g_rel = _dot(onehot.astype(BF16), rel.astype(BF16))
        within = jnp.sum((g_rel <= slot - base).astype(F32), axis=-1, keepdims=True)
        g_aff = _dot(onehot, a[e], precision=HIGHEST)
        val_ref[0, e] = jnp.sum(jnp.where(lane_l == within, g_aff, 0.0), axis=-1, keepdims=True)
        idx_ref[0, e] = (kstar * LANES + within).astype(I32)


def _route(aff_tiles, cap):
    b, ne, rows, _ = aff_tiles.shape
    return pl.pallas_call(
        functools.partial(_route_kernel, cap=cap, rows=rows),
        grid=(b,),
        in_specs=[pl.BlockSpec((1, ne, rows, LANES), lambda bi: (bi, 0, 0, 0))],
        out_specs=(pl.BlockSpec((1, ne, cap, 1), lambda bi: (bi, 0, 0, 0)),
                   pl.BlockSpec((1, ne, cap, 1), lambda bi: (bi, 0, 0, 0))),
        out_shape=(jax.ShapeDtypeStruct((b, ne, cap, 1), I32), jax.ShapeDtypeStruct((b, ne, cap, 1), F32)),
        compiler_params=_cparams(("parallel",)),
        name="route",
    )(aff_tiles)


def _gather_kernel(idx_ref, h_ref, o_ref, rows_sc, *, cap):
    base = (pl.program_id(0) * pl.num_programs(2) + pl.program_id(2)) * cap

    def body(g, carry):
        s0 = pl.multiple_of(g * SUBLANES, SUBLANES)
        rows = [h_ref[0, pl.ds(idx_ref[base + s0 + r], 1), :] for r in range(SUBLANES)]
        rows_sc[pl.ds(s0, SUBLANES), :] = jnp.concatenate(rows, axis=0)
        return carry

    lax.fori_loop(0, cap // SUBLANES, body, 0)
    o_ref[0, 0] = rows_sc[...].astype(BF16)


def _gather(idx_flat, h2, cap, rows_block, row_off):
    b, _, d = h2.shape
    dh = d // 2
    return pl.pallas_call(
        functools.partial(_gather_kernel, cap=cap),
        grid_spec=pltpu.PrefetchScalarGridSpec(
            num_scalar_prefetch=1,
            grid=(b, 2, N_EXPERTS),
            in_specs=[pl.BlockSpec((1, rows_block, dh), lambda bi, c, e, idx: (bi, 0, c))],
            out_specs=pl.BlockSpec((1, 1, cap, dh), lambda bi, c, e, idx: (e, bi, 0, c)),
            scratch_shapes=[pltpu.VMEM((cap, dh), F32)],
        ),
        out_shape=jax.ShapeDtypeStruct((N_EXPERTS, b, cap, d), BF16),
        compiler_params=_cparams(("arbitrary", "arbitrary", "arbitrary")),
        name="moe_gather",
    )(idx_flat + row_off, h2)


def _ffn_kernel(*refs, n_streams):
    xg_refs = refs[0:n_streams]
    val_refs = refs[n_streams:2 * n_streams]
    w1_ref, w3_ref, w2_ref = refs[2 * n_streams:2 * n_streams + 3]
    y_refs = refs[2 * n_streams + 3:]
    f = pl.program_id(1)
    w1 = w1_ref[0].astype(BF16)
    w3 = w3_ref[0].astype(BF16)
    w2 = w2_ref[0].astype(BF16)
    for xg_ref, val_ref, y_ref in zip(xg_refs, val_refs, y_refs):
        cap = xg_ref.shape[2]
        for bi in range(xg_ref.shape[1]):
            rows = slice(bi * cap, (bi + 1) * cap)
            xg = xg_ref[0, bi]
            y = _dot((_silu(_dot(xg, w1)) * _dot(xg, w3)).astype(BF16), w2)

            @pl.when(f == 0)
            def _():
                y_ref[0, rows] = y

            @pl.when(f > 0)
            def _():
                y_ref[0, rows] = y_ref[0, rows] + y

            @pl.when(f == pl.num_programs(1) - 1)
            def _():
                y_ref[0, rows] = y_ref[0, rows] * val_ref[0, rows]


def _ffn(xgs, vals, w1, w3, w2, layer, tf):
    _, ne, d, ff = w1.shape
    n_streams = len(xgs)
    in_specs, out_specs, out_shapes = [], [], []
    for xg in xgs:
        in_specs.append(pl.BlockSpec((1,) + xg.shape[1:], lambda e, f: (e, 0, 0, 0)))
    for v in vals:
        in_specs.append(pl.BlockSpec((1,) + v.shape[1:], lambda e, f: (e, 0, 0)))
    in_specs += [pl.BlockSpec((None, 1, d, tf), lambda e, f: (layer, e, 0, f)),
                 pl.BlockSpec((None, 1, d, tf), lambda e, f: (layer, e, 0, f)),
                 pl.BlockSpec((None, 1, tf, d), lambda e, f: (layer, e, f, 0))]
    for xg in xgs:
        m = xg.shape[1] * xg.shape[2]
        out_specs.append(pl.BlockSpec((1, m, d), lambda e, f: (e, 0, 0)))
        out_shapes.append(jax.ShapeDtypeStruct((ne, m, d), F32))
    return pl.pallas_call(
        functools.partial(_ffn_kernel, n_streams=n_streams),
        grid=(ne, ff // tf),
        in_specs=in_specs,
        out_specs=tuple(out_specs),
        out_shape=tuple(out_shapes),
        compiler_params=_cparams(("parallel", "arbitrary")),
        name="moe_ffn",
    )(*xgs, *vals, w1, w3, w2)


def _combine_kernel(idx_ref, y_ref, o_ref, *, cap):
    e = pl.program_id(2)

    @pl.when(e == 0)
    def _():
        o_ref[...] = jnp.zeros(o_ref.shape, F32)

    base = (pl.program_id(0) * pl.num_programs(2) + e) * cap

    def body(g, carry):
        s0 = pl.multiple_of(g * COMBINE_GROUP, COMBINE_GROUP)
        ts = [idx_ref[base + s0 + r] for r in range(COMBINE_GROUP)]
        ys = y_ref[0, pl.ds(s0, COMBINE_GROUP), :]
        rows = [o_ref[0, pl.ds(t, 1), :] for t in ts]
        for r, t in enumerate(ts):
            o_ref[0, pl.ds(t, 1), :] = rows[r] + ys[r:r + 1, :]
        return carry

    lax.fori_loop(0, cap // COMBINE_GROUP, body, 0)


def _combine(idx_flat, y, b, n_tok, cap):
    ne, _, d = y.shape
    dh = d // 2
    return pl.pallas_call(
        functools.partial(_combine_kernel, cap=cap),
        grid_spec=pltpu.PrefetchScalarGridSpec(
            num_scalar_prefetch=1,
            grid=(b, 2, ne),
            in_specs=[pl.BlockSpec((1, cap, dh), lambda bi, c, e, idx: (e, bi, c))],
            out_specs=pl.BlockSpec((1, n_tok, dh), lambda bi, c, e, idx: (bi, 0, c)),
        ),
        out_shape=jax.ShapeDtypeStruct((b, n_tok, d), F32),
        compiler_params=_cparams(("arbitrary", "arbitrary", "arbitrary")),
        name="moe_combine",
    )(idx_flat, y)


def _residual_kernel(x1_ref, mod_ref, mx_ref, mc_ref, o_ref, *, d_model, blk0):
    i = pl.program_id(1) + blk0
    moe = jnp.where(i == 0, mc_ref[0], mx_ref[0])
    o_ref[0] = x1_ref[0] + mod_ref[:, 5 * d_model:6 * d_model] * moe


def _residual(x1, modsel, moe_x, moe_c, tm, blk0):
    b, t, d = x1.shape
    nb = t // tm
    return pl.pallas_call(
        functools.partial(_residual_kernel, d_model=d, blk0=blk0),
        grid=(b, nb),
        in_specs=[pl.BlockSpec((1, tm, d), lambda bi, i: (bi, i, 0)),
                  pl.BlockSpec((None, None, 1, N_MOD * d), lambda bi, i: (bi, jnp.minimum(i + blk0, 1), 0, 0)),
                  pl.BlockSpec((1, tm, d), lambda bi, i: (bi, jnp.maximum(i + blk0 - 1, 0), 0)),
                  pl.BlockSpec((1, tm, d), lambda bi, i: (bi, 0, 0))],
        out_specs=pl.BlockSpec((1, tm, d), lambda bi, i: (bi, i, 0)),
        out_shape=jax.ShapeDtypeStruct((b, nb * tm, d), F32),
        compiler_params=_cparams(("parallel", "parallel")),
        name="moe_residual",
    )(x1, modsel, moe_x, moe_c)


def _rope_tables(ctx_len, seq):
    n = jnp.arange(seq)
    pos = jnp.stack([n // GRID_W, n % GRID_W], axis=-1).astype(F32)
    lane = jnp.arange(LANES) % HEAD_DIM
    axis = lane // 32
    n_freq = HEAD_DIM // 4
    inv = ROPE_THETA ** (-(lane % n_freq).astype(F32) / n_freq)
    ang = pos[:, axis] * inv
    sign = jnp.where((lane % 32) < 16, -1.0, 1.0)
    cos_t = jnp.concatenate([jnp.ones((ctx_len, LANES), F32), jnp.cos(ang)], axis=0)
    sin_t = jnp.concatenate([jnp.zeros((ctx_len, LANES), F32), jnp.sin(ang) * sign], axis=0)
    return cos_t, sin_t


def _reorder_w_in(w):
    d = w.shape[0]
    qk = w[:, :ATTN_W + KV_W]
    v = w[:, ATTN_W + KV_W:ATTN_W + 2 * KV_W]
    o = ATTN_W + 2 * KV_W
    ml = w[:, o:o + 4 * MLSTM_W]
    o += 4 * MLSTM_W
    mgate = w[:, o:o + 2 * N_DIR * MLSTM_HEADS]
    o += 2 * N_DIR * MLSTM_HEADS
    gd = w[:, o:o + 4 * GDN_W]
    o += 4 * GDN_W
    ggate = w[:, o:o + 2 * N_DIR * GDN_HEADS]
    pad = jnp.zeros((d, GATE_W - mgate.shape[1] - ggate.shape[1]), w.dtype)
    w_r = jnp.concatenate([qk, ml, gd, mgate, ggate, pad], axis=1)
    w_t = jnp.concatenate([v, ml[:, MLSTM_W:2 * MLSTM_W], mgate, ggate], axis=1).T
    return w_r.astype(BF16), w_t.astype(BF16)


def _gate_row(vals, col0):
    flat = vals.reshape(-1).astype(F32)
    return jnp.zeros((1, GATE_W), F32).at[0, col0:col0 + flat.shape[0]].set(flat)


def _gate_col(first, second):
    flat = jnp.concatenate([first.reshape(-1), second.reshape(-1)]).astype(F32)
    return jnp.broadcast_to(flat[:, None], (flat.shape[0], CHUNK))


def kernel(x, c, ctx, c_ctx, mod_w, mod_b, norm1_g, w_in, q_norm_g, k_norm_g, mlstm_i_bias, mlstm_f_bias,
           mlstm_out_g, gdn_conv_w, gdn_a_log, gdn_dt_bias, gdn_out_g, w_out, norm2_g, router_w, w1, w3, w2):
    b, seq, d = x.shape
    ctx_len = ctx.shape[1]
    depth = mod_w.shape[0]
    tm = ctx_len
    t = ctx_len + seq
    ne = N_EXPERTS
    cap_x = CAPACITY_FACTOR * seq // ne
    cap_c = CAPACITY_FACTOR * ctx_len // ne
    rows_c = 2 * SUBLANES

    cvec = jnp.concatenate([c, c_ctx[None, :], jnp.zeros((SUBLANES - b - 1, d), F32)], axis=0)
    mod = _modulation(cvec, mod_w, mod_b)
    cos_t, sin_t = _rope_tables(ctx_len, seq)
    bd = jnp.kron(jnp.eye(LANES // HEAD_DIM, dtype=F32), jnp.full((HEAD_DIM, HEAD_DIM), 1.0 / HEAD_DIM, F32)).astype(BF16)
    xa = jnp.concatenate([ctx, x], axis=1)
    zeros8 = jnp.zeros((N_DIR, MLSTM_HEADS), F32)

    for l in range(depth):
        need_ctx = l < depth - 1
        blk0 = 0 if need_ctx else 1
        modsel = jnp.stack([jnp.broadcast_to(mod[l, b], (b, N_MOD * d)), mod[l, :b]], axis=1)[:, :, None, :]
        w_r, w_t = _reorder_w_in(w_in[l])
        qg = jnp.tile(q_norm_g[l], LANES // HEAD_DIM)[None, :]
        kg = jnp.tile(k_norm_g[l], LANES // HEAD_DIM)[None, :]
        q, k, vt, mqkv, mkt, mo, gqkv, gz, gates, gates_t = _inproj(
            xa, modsel, norm1_g[l][None, :], w_r, w_t, cos_t, sin_t, qg, kg, bd, tm)

        ao = _attention(q, k, vt, ctx_len, tm, blk0)

        brow = _gate_row(mlstm_i_bias[l], GC_MI) + _gate_row(mlstm_f_bias[l], GC_MF)
        bcol = _gate_col(mlstm_i_bias[l], mlstm_f_bias[l])
        mhf, mhr = _mlstm_scan(mqkv, mkt, gates, gates_t, brow, bcol, ctx_len)

        conv12 = gdn_conv_w[l].reshape(CONV_K, 3 * GDN_HEADS, HEAD_DIM).transpose(1, 0, 2)
        gp = _gdn_prep(gqkv, conv12, ctx_len)
        neg_a = -jnp.exp(gdn_a_log[l].astype(F32))
        ghf, ghr = _gdn_scan(gp, gates, gates_t, _gate_row(neg_a, GC_GA), _gate_row(gdn_dt_bias[l], GC_GA),
                             _gate_col(neg_a, zeros8), _gate_col(gdn_dt_bias[l], zeros8), ctx_len)

        w_hm = w_out[l].astype(BF16)
        x1, hp, aff = _outproj(xa, modsel, ao, mhf, mhr, mo, mlstm_out_g[l], ghf, ghr, gz, gdn_out_g[l][None, :],
                               w_hm, norm2_g[l][None, :], router_w[l].T, tm, blk0)

        x_row0 = ctx_len - blk0 * tm
        idx_x, val_x = _route(aff[:, :, x_row0:].reshape(b, ne, seq // LANES, LANES), cap_x)
        idx_xf = idx_x.reshape(-1)
        xgs = [_gather(idx_xf, hp, cap_x, hp.shape[1], x_row0)]
        vals = [val_x.transpose(1, 0, 2, 3).reshape(ne, b * cap_x, 1)]
        if need_ctx:
            aff_c = jnp.pad(aff[:, :, :ctx_len], ((0, 0), (0, 0), (0, rows_c * LANES - ctx_len)), constant_values=-1.0)
            idx_c, val_c = _route(aff_c.reshape(b, ne, rows_c, LANES), cap_c)
            idx_cf = idx_c.reshape(-1)
            xgs.append(_gather(idx_cf, hp, cap_c, ctx_len, 0))
            vals.append(val_c.transpose(1, 0, 2, 3).reshape(ne, b * cap_c, 1))
        ys = _ffn(xgs, vals, w1, w3, w2, l, min(FFN_TF, w1.shape[3]))
        moe_x = _combine(idx_xf, ys[0], b, seq, cap_x)
        moe_c = _combine(idx_cf, ys[1], b, ctx_len, cap_c) if need_ctx else moe_x
        xa = _residual(x1, modsel, moe_x, moe_c, tm, blk0)
    return xa
```
